```python
import math
import jax, jax.numpy as jnp
from jax import lax
import numpy as np

D_MODEL = 1024
BATCH = 4
SEQ = 4096
DEPTH = 2

N_A = DEPTH // 2
N_B = DEPTH - N_A
D_FF = 2816
LRU_WIDTH = D_MODEL
LRU_BLOCKS = 4
LRU_BW = LRU_WIDTH // LRU_BLOCKS
CONV_W = 4
LRU_C = 8.0
N_HEADS = 16
HEAD_DIM = 64
D_ATTN = N_HEADS * HEAD_DIM
Q_BLOCK = 128
EPS = 1e-6

kernel_name = "yoco_rglru_stickbreaking_macaron"


def rms_norm(x, gain):
    xf = x.astype(jnp.float32)
    y = xf * lax.rsqrt(jnp.mean(xf * xf, axis=-1, keepdims=True) + EPS)
    return (y * gain.astype(jnp.float32)).astype(x.dtype)


def swiglu(h, w13, w2):
    gate, up = jnp.split(h @ w13, 2, axis=-1)
    return (jax.nn.silu(gate) * up) @ w2


def causal_depthwise_conv(x, w, b):
    S = x.shape[1]
    xp = jnp.pad(x, ((0, 0), (CONV_W - 1, 0), (0, 0)))
    y = b
    for k in range(CONV_W):
        y = y + w[k] * xp[:, k:k + S]
    return y


def block_diag_linear(x, w, b):
    Bt, S, _ = x.shape
    xb = x.reshape(Bt, S, LRU_BLOCKS, LRU_BW)
    y = jnp.einsum('bsnc,ncd->bsnd', xb, w).reshape(Bt, S, LRU_WIDTH)
    return y + b


def linear_recurrence(a, u):
    def step(h, au):
        a_t, u_t = au
        h = a_t * h + u_t
        return h, h
    h0 = jnp.zeros((a.shape[0], a.shape[2]), jnp.float32)
    _, hs = lax.scan(step, h0, (jnp.swapaxes(a, 0, 1), jnp.swapaxes(u, 0, 1)))
    return jnp.swapaxes(hs, 0, 1)


def rglru_block(h, w_in, conv_w, conv_b, w_r, b_r, w_i, b_i, lam, w_out):
    gate_br, rec = jnp.split(h @ w_in, 2, axis=-1)
    gate_br = jax.nn.gelu(gate_br)
    xc = causal_depthwise_conv(rec, conv_w, conv_b)
    r = jax.nn.sigmoid(block_diag_linear(xc, w_r, b_r))
    i = jax.nn.sigmoid(block_diag_linear(xc, w_i, b_i))
    log_a = -LRU_C * r.astype(jnp.float32) * jax.nn.softplus(-lam.astype(jnp.float32))
    a = jnp.exp(log_a)
    mult = jnp.sqrt(-jnp.expm1(2.0 * log_a))
    u = mult * (i * xc).astype(jnp.float32)
    hs = linear_recurrence(a, u)
    y = hs.astype(h.dtype) * gate_br
    return y @ w_out


def heads(t):
    Bt, S, _ = t.shape
    return t.reshape(Bt, S, N_HEADS, HEAD_DIM).transpose(0, 2, 1, 3)


def shared_kv(x, kv_gain, w_kv, k_gain):
    hkv = rms_norm(x, kv_gain)
    k, v = jnp.split(hkv @ w_kv, 2, axis=-1)
    k = rms_norm(heads(k), k_gain)
    return k, heads(v)


def stick_breaking_attention(q, k, v):
    S = q.shape[2]
    scale = HEAD_DIM ** -0.5
    outs = []
    for blk in range(S // Q_BLOCK):
        q0 = blk * Q_BLOCK
        end = q0 + Q_BLOCK
        z = jnp.einsum('bhqd,bhkd->bhqk', q[:, :, q0:end], k[:, :, :end]).astype(jnp.float32) * scale
        t_pos = q0 + jnp.arange(Q_BLOCK)[:, None]
        s_pos = jnp.arange(end)[None, :]
        mask = s_pos < t_pos
        log_stay = jnp.where(mask, -jax.nn.softplus(z), 0.0)
        stay_after = lax.cumsum(log_stay, axis=3, reverse=True) - log_stay
        log_w = jax.nn.log_sigmoid(z) + stay_after
        w = jnp.where(mask, jnp.exp(log_w), 0.0)
        o = jnp.einsum('bhqk,bhkd->bhqd', w, v[:, :, :end].astype(jnp.float32))
        outs.append(o.astype(v.dtype))
    return jnp.concatenate(outs, axis=2)


def stick_breaking_block(h, k, v, w_q, q_gain, w_o):
    q = rms_norm(heads(h @ w_q), q_gain)
    o = stick_breaking_attention(q, k, v)
    Bt, _, S, _ = o.shape
    o = o.transpose(0, 2, 1, 3).reshape(Bt, S, D_ATTN)
    return o @ w_o


def setup_inputs(seed: int = 0) -> dict:
    key = jax.random.key(seed)
    keys = jax.random.split(key, 32)
    counter = [0]
    f32 = jnp.float32

    def nk():
        k = keys[counter[0]]
        counter[0] += 1
        return k

    def w(shape, fan_in, scale=1.0):
        return jax.random.normal(nk(), shape, f32) * (scale * fan_in ** -0.5)

    def gain(shape):
        return 1.0 + 0.02 * jax.random.normal(nk(), shape, f32)

    def bias(shape):
        return 0.02 * jax.random.normal(nk(), shape, f32)

    a0 = jax.random.uniform(nk(), (N_A, LRU_WIDTH), f32, 0.9, 0.999)
    return {
        "x": jax.random.normal(nk(), (BATCH, SEQ, D_MODEL), f32),
        "ffn1_norm": gain((DEPTH, D_MODEL)),
        "ffn1_w13": w((DEPTH, D_MODEL, 2 * D_FF), D_MODEL),
        "ffn1_w2": w((DEPTH, D_FF, D_MODEL), D_FF, 0.5),
        "mix_norm": gain((DEPTH, D_MODEL)),
        "a_w_in": w((N_A, D_MODEL, 2 * LRU_WIDTH), D_MODEL),
        "a_conv_w": w((N_A, CONV_W, LRU_WIDTH), CONV_W),
        "a_conv_b": bias((N_A, LRU_WIDTH)),
        "a_w_r": w((N_A, LRU_BLOCKS, LRU_BW, LRU_BW), LRU_BW),
        "a_b_r": bias((N_A, LRU_WIDTH)),
        "a_w_i": w((N_A, LRU_BLOCKS, LRU_BW, LRU_BW), LRU_BW),
        "a_b_i": bias((N_A, LRU_WIDTH)),
        "a_lambda": jnp.log(a0) - jnp.log1p(-a0),
        "a_w_out": w((N_A, LRU_WIDTH, D_MODEL), LRU_WIDTH),
        "kv_norm": gain((D_MODEL,)),
        "w_kv": w((D_MODEL, 2 * D_ATTN), D_MODEL),
        "k_norm": gain((HEAD_DIM,)),
        "b_w_q": w((N_B, D_MODEL, D_ATTN), D_MODEL),
        "q_norm": gain((N_B, HEAD_DIM)),
        "b_w_o": w((N_B, D_ATTN, D_MODEL), D_ATTN),
        "ffn2_norm": gain((DEPTH, D_MODEL)),
        "ffn2_w13": w((DEPTH, D_MODEL, 2 * D_FF), D_MODEL),
        "ffn2_w2": w((DEPTH, D_FF, D_MODEL), D_FF, 0.5),
    }


def reference(x, ffn1_norm, ffn1_w13, ffn1_w2, mix_norm, a_w_in, a_conv_w, a_conv_b,
              a_w_r, a_b_r, a_w_i, a_b_i, a_lambda, a_w_out, kv_norm, w_kv, k_norm,
              b_w_q, q_norm, b_w_o, ffn2_norm, ffn2_w13, ffn2_w2):
    k_shared = None
    v_shared = None
    for l in range(DEPTH):
        if l == N_A:
            k_shared, v_shared = shared_kv(x, kv_norm, w_kv, k_norm)
        x = x + 0.5 * swiglu(rms_norm(x, ffn1_norm[l]), ffn1_w13[l], ffn1_w2[l])
        h = rms_norm(x, mix_norm[l])
        if l < N_A:
            x = x + rglru_block(h, a_w_in[l], a_conv_w[l], a_conv_b[l], a_w_r[l], a_b_r[l],
                                a_w_i[l], a_b_i[l], a_lambda[l], a_w_out[l])
        else:
            j = l - N_A
            x = x + stick_breaking_block(h, k_shared, v_shared, b_w_q[j], q_norm[j], b_w_o[j])
        x = x + 0.5 * swiglu(rms_norm(x, ffn2_norm[l]), ffn2_w13[l], ffn2_w2[l])
    return x
```

```python
import functools

import jax
import jax.numpy as jnp
import numpy as np
from jax import lax
from jax.experimental import pallas as pl
from jax.experimental.pallas import tpu as pltpu

EPS = 1e-6
N_HEADS = 16
HEAD_DIM = 64
CONV_W = 4
LRU_C = 8.0
LANES = 128
SUBLANES = 8
VMEM_LIMIT = 56 * 1024 * 1024

FFN_TM = 512
FF_CHUNK = 256
ROW_TM = 512
LRU_TS = 256
ATT_T = 256
LOG_W_ZERO = -104.0

F32 = jnp.float32
BF16 = jnp.bfloat16


def _rms_norm_rows(x, gain_row):
    ms = jnp.mean(x * x, axis=-1, keepdims=True)
    return x * lax.rsqrt(ms + EPS) * gain_row


def _sigmoid(x):
    return 1.0 / (1.0 + jnp.exp(-x))


def _split_bf16(x):
    hi = x.astype(BF16)
    lo = (x - hi.astype(F32)).astype(BF16)
    return hi, lo


def _const_spec(shape):
    nd = len(shape)
    return pl.BlockSpec(shape, lambda *_: (0,) * nd, pipeline_mode=pl.Buffered(1))


def _ffn_kernel(x_ref, g_ref, w13_ref, w2_ref, o_ref, act_ref):
    d_ff = w2_ref.shape[0]
    x = x_ref[...]
    xn = _rms_norm_rows(x, g_ref[...]).astype(BF16)
    for c in range(d_ff // FF_CHUNK):
        lo = c * FF_CHUNK
        gate = jnp.dot(xn, w13_ref[:, lo:lo + FF_CHUNK], preferred_element_type=F32)
        up = jnp.dot(xn, w13_ref[:, d_ff + lo:d_ff + lo + FF_CHUNK], preferred_element_type=F32)
        act_ref[:, lo:lo + FF_CHUNK] = (gate * _sigmoid(gate) * up).astype(BF16)
    y = jnp.dot(act_ref[...], w2_ref[...], preferred_element_type=F32)
    o_ref[...] = x + 0.5 * y


def _ffn(x2d, gain, w13, w2):
    m, d = x2d.shape
    d_ff = w2.shape[0]
    return pl.pallas_call(
        _ffn_kernel,
        grid=(m // FFN_TM,),
        in_specs=[
            pl.BlockSpec((FFN_TM, d), lambda i: (i, 0)),
            _const_spec((1, d)),
            _const_spec((d, 2 * d_ff)),
            _const_spec((d_ff, d)),
        ],
        out_specs=pl.BlockSpec((FFN_TM, d), lambda i: (i, 0)),
        out_shape=jax.ShapeDtypeStruct((m, d), F32),
        scratch_shapes=[pltpu.VMEM((FFN_TM, d_ff), BF16)],
        compiler_params=pltpu.CompilerParams(
            dimension_semantics=("parallel",), vmem_limit_bytes=VMEM_LIMIT),
        name="ffn",
    )(x2d, gain.reshape(1, d), w13, w2)


def _gelu_tanh(x):
    c = np.float32(np.sqrt(2.0 / np.pi))
    return x * (0.5 * (1.0 + jnp.tanh(c * (x + 0.044715 * (x * x * x)))))


def _rglru_kernel(x_ref, g_ref, win_ref, cw_ref, cb_ref, wri_ref, br_ref, bi_ref, lam_ref,
                  wout_ref, o_ref, rec_ref, a_ref, u_ref, gate_ref, hcar_ref):
    ts, w = x_ref.shape
    n_blocks, bw, _ = wri_ref.shape

    @pl.when(pl.program_id(1) == 0)
    def _():
        rec_ref[0:SUBLANES, :] = jnp.zeros((SUBLANES, w), F32)
        hcar_ref[...] = jnp.zeros_like(hcar_ref)

    x = x_ref[...]
    h = _rms_norm_rows(x, g_ref[...]).astype(BF16)
    proj = jnp.dot(h, win_ref[...], preferred_element_type=F32)
    gate_ref[...] = _gelu_tanh(proj[:, :w])
    rec = proj[:, w:]
    rec_ref[SUBLANES:SUBLANES + ts, :] = rec
    xc = cb_ref[...] + cw_ref[CONV_W - 1:CONV_W, :] * rec
    for k in range(CONV_W - 1):
        back = CONV_W - 1 - k
        xc = xc + cw_ref[k:k + 1, :] * rec_ref[SUBLANES - back:SUBLANES - back + ts, :]
    rec_ref[0:SUBLANES, :] = rec_ref[ts:ts + SUBLANES, :]

    xcb = xc.astype(BF16)
    sp_lam = jnp.maximum(-lam_ref[...], 0.0) + jnp.log(1.0 + jnp.exp(-jnp.abs(lam_ref[...])))
    for n in range(n_blocks):
        sl = slice(n * bw, (n + 1) * bw)
        ri = jnp.dot(xcb[:, sl], wri_ref[n], preferred_element_type=F32)
        r = _sigmoid(ri[:, :bw] + br_ref[:, sl])
        i = _sigmoid(ri[:, bw:] + bi_ref[:, sl])
        log_a = (-LRU_C) * r * sp_lam[:, sl]
        a = jnp.exp(log_a)
        a_ref[:, sl] = a
        u_ref[:, sl] = jnp.sqrt(-jnp.tanh(log_a) * (1.0 + a * a)) * (i * xc[:, sl])

    row = lax.broadcasted_iota(jnp.int32, (SUBLANES, w), 0)

    def scan_group(gidx, hprev):
        r0 = pl.multiple_of(gidx * SUBLANES, SUBLANES)
        a = a_ref[pl.ds(r0, SUBLANES), :]
        u = u_ref[pl.ds(r0, SUBLANES), :]
        for sh in (1, 2, 4):
            keep = row >= sh
            u_new = jnp.where(keep, u + a * pltpu.roll(u, sh, 0), u)
            a = jnp.where(keep, a * pltpu.roll(a, sh, 0), a)
            u = u_new
        hs = u + a * hprev
        u_ref[pl.ds(r0, SUBLANES), :] = hs
        return hs[SUBLANES - 1:SUBLANES, :]

    hcar_ref[...] = lax.fori_loop(0, ts // SUBLANES, scan_group, hcar_ref[...])

    y = (u_ref[...] * gate_ref[...]).astype(BF16)
    o_ref[...] = x + jnp.dot(y, wout_ref[...], preferred_element_type=F32)


def _rglru(x, gain, w_in, conv_w, conv_b, w_ri, b_r, b_i, lam, w_out):
    b, s, d = x.shape
    w = w_out.shape[0]
    n_blocks, bw, _ = w_ri.shape
    row_spec = pl.BlockSpec((None, LRU_TS, d), lambda bi, si: (bi, si, 0))
    return pl.pallas_call(
        _rglru_kernel,
        grid=(b, s // LRU_TS),
        in_specs=[
            row_spec,
            _const_spec((1, d)),
            _const_spec((d, 2 * w)),
            _const_spec((CONV_W, w)),
            _const_spec((1, w)),
            _const_spec((n_blocks, bw, 2 * bw)),
            _const_spec((1, w)),
            _const_spec((1, w)),
            _const_spec((1, w)),
            _const_spec((w, d)),
        ],
        out_specs=row_spec,
        out_shape=jax.ShapeDtypeStruct((b, s, d), F32),
        scratch_shapes=[
            pltpu.VMEM((LRU_TS + SUBLANES, w), F32),
            pltpu.VMEM((LRU_TS, w), F32),
            pltpu.VMEM((LRU_TS, w), F32),
            pltpu.VMEM((LRU_TS, w), F32),
            pltpu.VMEM((1, w), F32),
        ],
        compiler_params=pltpu.CompilerParams(
            dimension_semantics=("parallel", "arbitrary"), vmem_limit_bytes=VMEM_LIMIT),
        name="rglru",
    )(x, gain.reshape(1, d), w_in, conv_w, conv_b.reshape(1, w), w_ri, b_r.reshape(1, w),
      b_i.reshape(1, w), lam.reshape(1, w), w_out)


def _head_rms_norm(t, gain_row, pool_ref):
    outs = []
    for g in range(t.shape[1] // LANES):
        tg = t[:, g * LANES:(g + 1) * LANES]
        hi, lo = _split_bf16(tg * tg)
        ms = (jnp.dot(hi, pool_ref[...], preferred_element_type=F32)
              + jnp.dot(lo, pool_ref[...], preferred_element_type=F32))
        outs.append(tg * lax.rsqrt(ms + EPS))
    return jnp.concatenate(outs, axis=1) * gain_row


def _head_pool_matrix():
    idx = np.arange(LANES) // HEAD_DIM
    return jnp.asarray((idx[:, None] == idx[None, :]).astype(np.float32) / HEAD_DIM, BF16)


def _kv_kernel(x_ref, g_ref, wkv_ref, kg_ref, pool_ref, k_ref, v_ref):
    d_attn = k_ref.shape[1]
    h = _rms_norm_rows(x_ref[...], g_ref[...]).astype(BF16)
    kv = jnp.dot(h, wkv_ref[...], preferred_element_type=F32)
    k_ref[...] = _head_rms_norm(kv[:, :d_attn], kg_ref[...], pool_ref).astype(BF16)
    v_ref[...] = kv[:, d_attn:].astype(BF16)


def _kv(x2d, gain, w_kv, k_gain_row, pool):
    m, d = x2d.shape
    d_attn = w_kv.shape[1] // 2
    out_spec = pl.BlockSpec((ROW_TM, d_attn), lambda i: (i, 0))
    return pl.pallas_call(
        _kv_kernel,
        grid=(m // ROW_TM,),
        in_specs=[
            pl.BlockSpec((ROW_TM, d), lambda i: (i, 0)),
            _const_spec((1, d)),
            _const_spec((d, 2 * d_attn)),
            _const_spec((1, d_attn)),
            _const_spec((LANES, LANES)),
        ],
        out_specs=[out_spec, out_spec],
        out_shape=[jax.ShapeDtypeStruct((m, d_attn), BF16)] * 2,
        compiler_params=pltpu.CompilerParams(
            dimension_semantics=("parallel",), vmem_limit_bytes=VMEM_LIMIT),
        name="kv",
    )(x2d, gain.reshape(1, d), w_kv, k_gain_row, pool)


def _q_kernel(x_ref, g_ref, wq_ref, qg_ref, pool_ref, q_ref):
    h = _rms_norm_rows(x_ref[...], g_ref[...]).astype(BF16)
    q = jnp.dot(h, wq_ref[...], preferred_element_type=F32)
    scale = HEAD_DIM ** -0.5
    q_ref[...] = (_head_rms_norm(q, qg_ref[...], pool_ref) * scale).astype(BF16)


def _qproj(x2d, gain, w_q, q_gain_row, pool):
    m, d = x2d.shape
    d_attn = w_q.shape[1]
    return pl.pallas_call(
        _q_kernel,
        grid=(m // ROW_TM,),
        in_specs=[
            pl.BlockSpec((ROW_TM, d), lambda i: (i, 0)),
            _const_spec((1, d)),
            _const_spec((d, d_attn)),
            _const_spec((1, d_attn)),
            _const_spec((LANES, LANES)),
        ],
        out_specs=pl.BlockSpec((ROW_TM, d_attn), lambda i: (i, 0)),
        out_shape=jax.ShapeDtypeStruct((m, d_attn), BF16),
        compiler_params=pltpu.CompilerParams(
            dimension_semantics=("parallel",), vmem_limit_bytes=VMEM_LIMIT),
        name="qproj",
    )(x2d, gain.reshape(1, d), w_q, q_gain_row, pool)


def _attn_kernel(q_ref, k_ref, v_ref, tri_ref, o_ref, acc_ref, carry_ref):
    t = ATT_T
    qi = pl.program_id(2)
    lane = lax.broadcasted_iota(jnp.int32, (t, LANES), 1)
    q_pos = lax.broadcasted_iota(jnp.int32, (t, t), 0)
    k_pos = lax.broadcasted_iota(jnp.int32, (t, t), 1)
    causal = k_pos < q_pos
    q = q_ref[...]

    def block(qh, head, kj, masked):
        r0 = pl.multiple_of(kj * t, t)
        kb = k_ref[pl.ds(r0, t), :]
        vb = v_ref[pl.ds(r0, t), :]
        z = lax.dot_general(qh, kb, (((1,), (1,)), ((), ())), preferred_element_type=F32)
        sp = jnp.maximum(z, 0.0) + jnp.log(1.0 + jnp.exp(-jnp.abs(z)))
        log_stay = -sp
        if masked:
            log_stay = jnp.where(causal, log_stay, 0.0)
        hi, lo = _split_bf16(log_stay)
        stay_after = (jnp.dot(hi, tri_ref[...], preferred_element_type=F32)
                      + jnp.dot(lo, tri_ref[...], preferred_element_type=F32)
                      + carry_ref[head])
        w = jnp.exp(z - sp + stay_after)
        if masked:
            w = jnp.where(causal, w, 0.0)
        acc_ref[head] += jnp.dot(w.astype(BF16), vb, preferred_element_type=F32)
        carry = carry_ref[head] + jnp.sum(log_stay, axis=1, keepdims=True)
        carry_ref[head] = carry
        return jnp.max(carry)

    for head in range(2):
        in_head = (lane >= head * HEAD_DIM) & (lane < (head + 1) * HEAD_DIM)
        qh = jnp.where(in_head, q, jnp.zeros_like(q))
        acc_ref[head] = jnp.zeros((t, LANES), F32)
        carry_ref[head] = jnp.zeros((t, 1), F32)
        top = block(qh, head, qi, True)

        def cond(state):
            kj, top = state
            return jnp.logical_and(kj >= 0, top > LOG_W_ZERO)

        def body(state, qh=qh, head=head):
            kj, _ = state
            return kj - 1, block(qh, head, kj, False)

        lax.while_loop(cond, body, (qi - 1, top))

    o_ref[...] = jnp.where(lane < HEAD_DIM, acc_ref[0], acc_ref[1]).astype(o_ref.dtype)


def _attention(q, k, v, tri):
    b, s, d_attn = q.shape
    n_pairs = d_attn // LANES
    kv_spec = pl.BlockSpec((None, s, LANES), lambda bi, pi, qi: (bi, 0, pi))
    q_spec = pl.BlockSpec((None, ATT_T, LANES), lambda bi, pi, qi: (bi, qi, pi))
    return pl.pallas_call(
        _attn_kernel,
        grid=(b, n_pairs, s // ATT_T),
        in_specs=[q_spec, kv_spec, kv_spec, _const_spec((ATT_T, ATT_T))],
        out_specs=q_spec,
        out_shape=jax.ShapeDtypeStruct((b, s, d_attn), BF16),
        scratch_shapes=[
            pltpu.VMEM((2, ATT_T, LANES), F32),
            pltpu.VMEM((2, ATT_T, 1), F32),
        ],
        compiler_params=pltpu.CompilerParams(
            dimension_semantics=("parallel", "parallel", "arbitrary"),
            vmem_limit_bytes=VMEM_LIMIT),
        name="attn",
    )(q, k, v, tri)


def _oproj_kernel(o_ref, wo_ref, x_ref, y_ref):
    y_ref[...] = x_ref[...] + jnp.dot(o_ref[...], wo_ref[...], preferred_element_type=F32)


def _oproj(o2d, w_o, x2d):
    m, d = x2d.shape
    d_attn = o2d.shape[1]
    return pl.pallas_call(
        _oproj_kernel,
        grid=(m // ROW_TM,),
        in_specs=[
            pl.BlockSpec((ROW_TM, d_attn), lambda i: (i, 0)),
            _const_spec((d_attn, d)),
            pl.BlockSpec((ROW_TM, d), lambda i: (i, 0)),
        ],
        out_specs=pl.BlockSpec((ROW_TM, d), lambda i: (i, 0)),
        out_shape=jax.ShapeDtypeStruct((m, d), F32),
        compiler_params=pltpu.CompilerParams(
            dimension_semantics=("parallel",), vmem_limit_bytes=VMEM_LIMIT),
        name="oproj",
    )(o2d, w_o, x2d)


def kernel(x, ffn1_norm, ffn1_w13, ffn1_w2, mix_norm, a_w_in, a_conv_w, a_conv_b, a_w_r, a_b_r,
           a_w_i, a_b_i, a_lambda, a_w_out, kv_norm, w_kv, k_norm, b_w_q, q_norm, b_w_o,
           ffn2_norm, ffn2_w13, ffn2_w2):
    b, s, d = x.shape
    depth = ffn1_norm.shape[0]
    n_a = a_w_in.shape[0]
    m = b * s
    bf = lambda t: t.astype(BF16)
    pool = _head_pool_matrix()
    tri_idx = np.arange(ATT_T)
    tri = jnp.asarray((tri_idx[:, None] > tri_idx[None, :]).astype(np.float32), BF16)

    k_sh = v_sh = None
    for l in range(depth):
        if l == n_a:
            k2d, v2d = _kv(x.reshape(m, d), kv_norm, bf(w_kv),
                           jnp.tile(k_norm, N_HEADS).reshape(1, -1), pool)
            k_sh = k2d.reshape(b, s, -1)
            v_sh = v2d.reshape(b, s, -1)
        x = _ffn(x.reshape(m, d), ffn1_norm[l], bf(ffn1_w13[l]), bf(ffn1_w2[l])).reshape(b, s, d)
        if l < n_a:
            w_ri = bf(jnp.concatenate([a_w_r[l], a_w_i[l]], axis=-1))
            x = _rglru(x, mix_norm[l], bf(a_w_in[l]), a_conv_w[l], a_conv_b[l], w_ri, a_b_r[l],
                       a_b_i[l], a_lambda[l], bf(a_w_out[l]))
        else:
            j = l - n_a
            q = _qproj(x.reshape(m, d), mix_norm[l], bf(b_w_q[j]),
                       jnp.tile(q_norm[j], N_HEADS).reshape(1, -1), pool)
            o = _attention(q.reshape(b, s, -1), k_sh, v_sh, tri)
            x = _oproj(o.reshape(m, -1), bf(b_w_o[j]), x.reshape(m, d)).reshape(b, s, d)
        x = _ffn(x.reshape(m, d), ffn2_norm[l], bf(ffn2_w13[l]), bf(ffn2_w2[l])).reshape(b, s, d)
    return x
```

```python
import functools

import jax
import jax.numpy as jnp
import numpy as np
from jax import lax
from jax.experimental import pallas as pl
from jax.experimental.pallas import tpu as pltpu

EPS = 1e-6
N_HEADS = 16
HEAD_DIM = 64
CONV_W = 4
LRU_C = 8.0
LANES = 128
SUBLANES = 8
VMEM_LIMIT = 56 * 1024 * 1024

FFN_TM = 512
FF_CHUNK = 256
ROW_TM = 512
LRU_TS = 256
ATT_T = 256
ATT_PAIRS = 4
LOG_W_ZERO = -104.0

F32 = jnp.float32
BF16 = jnp.bfloat16


def _rms_norm_rows(x, gain_row):
    ms = jnp.mean(x * x, axis=-1, keepdims=True)
    return x * lax.rsqrt(ms + EPS) * gain_row


def _sigmoid(x):
    return 1.0 / (1.0 + jnp.exp(-x))


def _split_bf16(x):
    hi = x.astype(BF16)
    lo = (x - hi.astype(F32)).astype(BF16)
    return hi, lo


def _const_spec(shape):
    nd = len(shape)
    return pl.BlockSpec(shape, lambda *_: (0,) * nd, pipeline_mode=pl.Buffered(1))


def _ffn_kernel(x_ref, g_ref, w13_ref, w2_ref, o_ref, act_ref):
    d_ff = w2_ref.shape[0]
    x = x_ref[...]
    xn = _rms_norm_rows(x, g_ref[...]).astype(BF16)
    for c in range(d_ff // FF_CHUNK):
        lo = c * FF_CHUNK
        gate = jnp.dot(xn, w13_ref[:, lo:lo + FF_CHUNK], preferred_element_type=F32)
        up = jnp.dot(xn, w13_ref[:, d_ff + lo:d_ff + lo + FF_CHUNK], preferred_element_type=F32)
        act_ref[:, lo:lo + FF_CHUNK] = (gate * _sigmoid(gate) * up).astype(BF16)
    y = jnp.dot(act_ref[...], w2_ref[...], preferred_element_type=F32)
    o_ref[...] = x + 0.5 * y


def _ffn(x2d, gain, w13, w2):
    m, d = x2d.shape
    d_ff = w2.shape[0]
    return pl.pallas_call(
        _ffn_kernel,
        grid=(m // FFN_TM,),
        in_specs=[
            pl.BlockSpec((FFN_TM, d), lambda i: (i, 0)),
            _const_spec((1, d)),
            _const_spec((d, 2 * d_ff)),
            _const_spec((d_ff, d)),
        ],
        out_specs=pl.BlockSpec((FFN_TM, d), lambda i: (i, 0)),
        out_shape=jax.ShapeDtypeStruct((m, d), F32),
        scratch_shapes=[pltpu.VMEM((FFN_TM, d_ff), BF16)],
        compiler_params=pltpu.CompilerParams(
            dimension_semantics=("parallel",), vmem_limit_bytes=VMEM_LIMIT),
        name="ffn",
    )(x2d, gain.reshape(1, d), w13, w2)


def _gelu_tanh(x):
    c = np.float32(np.sqrt(2.0 / np.pi))
    return x * (0.5 * (1.0 + jnp.tanh(c * (x + 0.044715 * (x * x * x)))))


def _rglru_kernel(x_ref, g_ref, win_ref, cw_ref, cb_ref, wri_ref, br_ref, bi_ref, lam_ref,
                  wout_ref, o_ref, rec_ref, a_ref, u_ref, gate_ref, hcar_ref):
    ts, w = x_ref.shape
    n_blocks, bw, _ = wri_ref.shape

    @pl.when(pl.program_id(1) == 0)
    def _():
        rec_ref[0:SUBLANES, :] = jnp.zeros((SUBLANES, w), F32)
        hcar_ref[...] = jnp.zeros_like(hcar_ref)

    x = x_ref[...]
    h = _rms_norm_rows(x, g_ref[...]).astype(BF16)
    proj = jnp.dot(h, win_ref[...], preferred_element_type=F32)
    gate_ref[...] = _gelu_tanh(proj[:, :w])
    rec = proj[:, w:]
    rec_ref[SUBLANES:SUBLANES + ts, :] = rec
    xc = cb_ref[...] + cw_ref[CONV_W - 1:CONV_W, :] * rec
    for k in range(CONV_W - 1):
        back = CONV_W - 1 - k
        xc = xc + cw_ref[k:k + 1, :] * rec_ref[SUBLANES - back:SUBLANES - back + ts, :]
    rec_ref[0:SUBLANES, :] = rec_ref[ts:ts + SUBLANES, :]

    xcb = xc.astype(BF16)
    sp_lam = jnp.maximum(-lam_ref[...], 0.0) + jnp.log(1.0 + jnp.exp(-jnp.abs(lam_ref[...])))
    for n in range(n_blocks):
        sl = slice(n * bw, (n + 1) * bw)
        ri = jnp.dot(xcb[:, sl], wri_ref[n], preferred_element_type=F32)
        r = _sigmoid(ri[:, :bw] + br_ref[:, sl])
        i = _sigmoid(ri[:, bw:] + bi_ref[:, sl])
        log_a = (-LRU_C) * r * sp_lam[:, sl]
        a = jnp.exp(log_a)
        a_ref[:, sl] = a
        u_ref[:, sl] = jnp.sqrt(-jnp.tanh(log_a) * (1.0 + a * a)) * (i * xc[:, sl])

    row = lax.broadcasted_iota(jnp.int32, (SUBLANES, w), 0)

    def scan_group(gidx, hprev):
        r0 = pl.multiple_of(gidx * SUBLANES, SUBLANES)
        a = a_ref[pl.ds(r0, SUBLANES), :]
        u = u_ref[pl.ds(r0, SUBLANES), :]
        for sh in (1, 2, 4):
            keep = row >= sh
            u_new = jnp.where(keep, u + a * pltpu.roll(u, sh, 0), u)
            a = jnp.where(keep, a * pltpu.roll(a, sh, 0), a)
            u = u_new
        hs = u + a * hprev
        u_ref[pl.ds(r0, SUBLANES), :] = hs
        return hs[SUBLANES - 1:SUBLANES, :]

    hcar_ref[...] = lax.fori_loop(0, ts // SUBLANES, scan_group, hcar_ref[...])

    y = (u_ref[...] * gate_ref[...]).astype(BF16)
    o_ref[...] = x + jnp.dot(y, wout_ref[...], preferred_element_type=F32)


def _rglru(x, gain, w_in, conv_w, conv_b, w_ri, b_r, b_i, lam, w_out):
    b, s, d = x.shape
    w = w_out.shape[0]
    n_blocks, bw, _ = w_ri.shape
    row_spec = pl.BlockSpec((None, LRU_TS, d), lambda bi, si: (bi, si, 0))
    return pl.pallas_call(
        _rglru_kernel,
        grid=(b, s // LRU_TS),
        in_specs=[
            row_spec,
            _const_spec((1, d)),
            _const_spec((d, 2 * w)),
            _const_spec((CONV_W, w)),
            _const_spec((1, w)),
            _const_spec((n_blocks, bw, 2 * bw)),
            _const_spec((1, w)),
            _const_spec((1, w)),
            _const_spec((1, w)),
            _const_spec((w, d)),
        ],
        out_specs=row_spec,
        out_shape=jax.ShapeDtypeStruct((b, s, d), F32),
        scratch_shapes=[
            pltpu.VMEM((LRU_TS + SUBLANES, w), F32),
            pltpu.VMEM((LRU_TS, w), F32),
            pltpu.VMEM((LRU_TS, w), F32),
            pltpu.VMEM((LRU_TS, w), F32),
            pltpu.VMEM((1, w), F32),
        ],
        compiler_params=pltpu.CompilerParams(
            dimension_semantics=("parallel", "arbitrary"), vmem_limit_bytes=VMEM_LIMIT),
        name="rglru",
    )(x, gain.reshape(1, d), w_in, conv_w, conv_b.reshape(1, w), w_ri, b_r.reshape(1, w),
      b_i.reshape(1, w), lam.reshape(1, w), w_out)


def _head_rms_norm(t, gain_row, pool_ref):
    outs = []
    for g in range(t.shape[1] // LANES):
        tg = t[:, g * LANES:(g + 1) * LANES]
        hi, lo = _split_bf16(tg * tg)
        ms = (jnp.dot(hi, pool_ref[...], preferred_element_type=F32)
              + jnp.dot(lo, pool_ref[...], preferred_element_type=F32))
        outs.append(tg * lax.rsqrt(ms + EPS))
    return jnp.concatenate(outs, axis=1) * gain_row


def _head_pool_matrix():
    idx = np.arange(LANES) // HEAD_DIM
    return jnp.asarray((idx[:, None] == idx[None, :]).astype(np.float32) / HEAD_DIM, BF16)


def _kv_kernel(x_ref, g_ref, wkv_ref, kg_ref, pool_ref, k_ref, v_ref):
    d_attn = k_ref.shape[1]
    h = _rms_norm_rows(x_ref[...], g_ref[...]).astype(BF16)
    kv = jnp.dot(h, wkv_ref[...], preferred_element_type=F32)
    k_ref[...] = _head_rms_norm(kv[:, :d_attn], kg_ref[...], pool_ref).astype(BF16)
    v_ref[...] = kv[:, d_attn:].astype(BF16)


def _kv(x2d, gain, w_kv, k_gain_row, pool):
    m, d = x2d.shape
    d_attn = w_kv.shape[1] // 2
    out_spec = pl.BlockSpec((ROW_TM, d_attn), lambda i: (i, 0))
    return pl.pallas_call(
        _kv_kernel,
        grid=(m // ROW_TM,),
        in_specs=[
            pl.BlockSpec((ROW_TM, d), lambda i: (i, 0)),
            _const_spec((1, d)),
            _const_spec((d, 2 * d_attn)),
            _const_spec((1, d_attn)),
            _const_spec((LANES, LANES)),
        ],
        out_specs=[out_spec, out_spec],
        out_shape=[jax.ShapeDtypeStruct((m, d_attn), BF16)] * 2,
        compiler_params=pltpu.CompilerParams(
            dimension_semantics=("parallel",), vmem_limit_bytes=VMEM_LIMIT),
        name="kv",
    )(x2d, gain.reshape(1, d), w_kv, k_gain_row, pool)


def _q_kernel(x_ref, g_ref, wq_ref, qg_ref, pool_ref, q_ref):
    h = _rms_norm_rows(x_ref[...], g_ref[...]).astype(BF16)
    q = jnp.dot(h, wq_ref[...], preferred_element_type=F32)
    scale = HEAD_DIM ** -0.5
    q_ref[...] = (_head_rms_norm(q, qg_ref[...], pool_ref) * scale).astype(BF16)


def _qproj(x2d, gain, w_q, q_gain_row, pool):
    m, d = x2d.shape
    d_attn = w_q.shape[1]
    return pl.pallas_call(
        _q_kernel,
        grid=(m // ROW_TM,),
        in_specs=[
            pl.BlockSpec((ROW_TM, d), lambda i: (i, 0)),
            _const_spec((1, d)),
            _const_spec((d, d_attn)),
            _const_spec((1, d_attn)),
            _const_spec((LANES, LANES)),
        ],
        out_specs=pl.BlockSpec((ROW_TM, d_attn), lambda i: (i, 0)),
        out_shape=jax.ShapeDtypeStruct((m, d_attn), BF16),
        compiler_params=pltpu.CompilerParams(
            dimension_semantics=("parallel",), vmem_limit_bytes=VMEM_LIMIT),
        name="qproj",
    )(x2d, gain.reshape(1, d), w_q, q_gain_row, pool)


def _attn_kernel(q_ref, k_ref, vt_ref, tri_ref, o_ref, acc_ref, carry_ref):
    t = ATT_T
    qi = pl.program_id(2)
    lane = lax.broadcasted_iota(jnp.int32, (t, LANES), 1)
    k_pos = lax.broadcasted_iota(jnp.int32, (t, t), 0)
    q_pos = lax.broadcasted_iota(jnp.int32, (t, t), 1)
    causal = k_pos < q_pos
    n_heads = 2 * ATT_PAIRS
    q_heads = []
    for pair in range(ATT_PAIRS):
        q = q_ref[:, pair * LANES:(pair + 1) * LANES]
        q_heads += [jnp.where(lane < HEAD_DIM, q, jnp.zeros_like(q)),
                    jnp.where(lane >= HEAD_DIM, q, jnp.zeros_like(q))]

    h = t // 2

    def blocks(specs):
        zs = []
        for head, kj, _, _ in specs:
            pair = head // 2
            kb = k_ref[pl.ds(pl.multiple_of(kj * t, t), t), pair * LANES:(pair + 1) * LANES]
            zs.append(lax.dot_general(kb, q_heads[head], (((1,), (1,)), ((), ())),
                                      preferred_element_type=F32))
        sps = [jnp.maximum(z, 0.0) + jnp.log(1.0 + jnp.exp(-jnp.abs(z))) for z in zs]
        exts = []
        for sp, (_, _, masked, _) in zip(sps, specs):
            hi, lo = _split_bf16(jnp.where(causal, sp, 0.0) if masked else sp)
            exts.append((jnp.dot(tri_ref[...], jnp.concatenate([hi[:h], lo[:h]], axis=0),
                                 preferred_element_type=F32),
                         jnp.dot(tri_ref[...], jnp.concatenate([hi[h:], lo[h:]], axis=0),
                                 preferred_element_type=F32)))
        ws, carries = [], []
        for z, sp, (ext_early, ext_late), (_, _, masked, carry) in zip(zs, sps, exts, specs):
            if isinstance(carry, int):
                carry = carries[carry]
            carry_early = ext_late[h:] if carry is None else carry + ext_late[h:]
            sa_early = ext_early[:h] + pltpu.repeat(carry_early, h // SUBLANES, axis=0)
            sa_late = ext_late[:h]
            if carry is not None:
                sa_late = sa_late + pltpu.repeat(carry, h // SUBLANES, axis=0)
            w = jnp.exp(z - sp + jnp.concatenate([sa_early, sa_late], axis=0))
            if masked:
                w = jnp.where(causal, w, 0.0)
            ws.append(w.astype(BF16))
            carries.append(carry_early + ext_early[h:])
        pvs = [jnp.dot(vt_ref[head // 2, kj], w, preferred_element_type=F32)
               for w, (head, kj, _, _) in zip(ws, specs)]
        return list(zip(pvs, carries))

    @pl.when(qi == 0)
    def _():
        out = blocks([(head, qi, True, None) for head in range(n_heads)])
        for head, (pv, carry) in enumerate(out):
            acc_ref[head] = pv
            carry_ref[head] = carry

    @pl.when(qi > 0)
    def _():
        out = blocks([(head, qi, True, None) for head in range(n_heads)]
                     + [(head, qi - 1, False, head) for head in range(n_heads)])
        for head in range(n_heads):
            acc_ref[head] = out[head][0] + out[n_heads + head][0]
            carry_ref[head] = out[n_heads + head][1]

    def max_carry(carries):
        top = carries[0]
        for carry in carries[1:]:
            top = jnp.maximum(top, carry)
        return jnp.max(top)

    def cond(state):
        kj, top = state
        return jnp.logical_and(kj >= 0, top > LOG_W_ZERO)

    def body(state):
        kj, _ = state
        out = blocks([(head, kj, False, carry_ref[head]) for head in range(n_heads)])
        for head, (pv, carry) in enumerate(out):
            acc_ref[head] += pv
            carry_ref[head] = carry
        return kj - 1, max_carry([carry for _, carry in out])

    top = max_carry([carry_ref[head] for head in range(n_heads)])
    lax.while_loop(cond, body, (qi - 2, top))

    feat = lax.broadcasted_iota(jnp.int32, (LANES, t), 0)
    for pair in range(ATT_PAIRS):
        o_t = jnp.where(feat < HEAD_DIM, acc_ref[2 * pair], acc_ref[2 * pair + 1])
        o_ref[:, pair * LANES:(pair + 1) * LANES] = o_t.T.astype(o_ref.dtype)


def _suffix_sum_matrix():
    j = np.arange(ATT_T // 2)
    later = (j[None, :] > j[:, None]).astype(np.float32)
    one_part = np.concatenate([later, np.ones((SUBLANES, ATT_T // 2), np.float32)], axis=0)
    return jnp.asarray(-np.concatenate([one_part, one_part], axis=1), BF16)


def _attention(q, k, vt, tri):
    b, s, d_attn = q.shape
    width = ATT_PAIRS * LANES
    k_spec = pl.BlockSpec((None, s, width), lambda bi, pi, qi: (bi, 0, pi))
    vt_spec = pl.BlockSpec((None, ATT_PAIRS, s // ATT_T, LANES, ATT_T),
                           lambda bi, pi, qi: (bi, pi, 0, 0, 0))
    q_spec = pl.BlockSpec((None, ATT_T, width), lambda bi, pi, qi: (bi, qi, pi))
    return pl.pallas_call(
        _attn_kernel,
        grid=(b, d_attn // width, s // ATT_T),
        in_specs=[q_spec, k_spec, vt_spec, _const_spec(tri.shape)],
        out_specs=q_spec,
        out_shape=jax.ShapeDtypeStruct((b, s, d_attn), BF16),
        scratch_shapes=[
            pltpu.VMEM((2 * ATT_PAIRS, LANES, ATT_T), F32),
            pltpu.VMEM((2 * ATT_PAIRS, SUBLANES, ATT_T), F32),
        ],
        compiler_params=pltpu.CompilerParams(
            dimension_semantics=("parallel", "parallel", "arbitrary"),
            vmem_limit_bytes=VMEM_LIMIT),
        name="attn",
    )(q, k, vt, tri)


def _oproj_kernel(o_ref, wo_ref, x_ref, y_ref):
    y_ref[...] = x_ref[...] + jnp.dot(o_ref[...], wo_ref[...], preferred_element_type=F32)


def _oproj(o2d, w_o, x2d):
    m, d = x2d.shape
    d_attn = o2d.shape[1]
    return pl.pallas_call(
        _oproj_kernel,
        grid=(m // ROW_TM,),
        in_specs=[
            pl.BlockSpec((ROW_TM, d_attn), lambda i: (i, 0)),
            _const_spec((d_attn, d)),
            pl.BlockSpec((ROW_TM, d), lambda i: (i, 0)),
        ],
        out_specs=pl.BlockSpec((ROW_TM, d), lambda i: (i, 0)),
        out_shape=jax.ShapeDtypeStruct((m, d), F32),
        compiler_params=pltpu.CompilerParams(
            dimension_semantics=("parallel",), vmem_limit_bytes=VMEM_LIMIT),
        name="oproj",
    )(o2d, w_o, x2d)


def kernel(x, ffn1_norm, ffn1_w13, ffn1_w2, mix_norm, a_w_in, a_conv_w, a_conv_b, a_w_r, a_b_r,
           a_w_i, a_b_i, a_lambda, a_w_out, kv_norm, w_kv, k_norm, b_w_q, q_norm, b_w_o,
           ffn2_norm, ffn2_w13, ffn2_w2):
    b, s, d = x.shape
    depth = ffn1_norm.shape[0]
    n_a = a_w_in.shape[0]
    m = b * s
    bf = lambda t: t.astype(BF16)
    pool = _head_pool_matrix()
    tri = _suffix_sum_matrix()

    k_sh = v_sh = None
    for l in range(depth):
        if l == n_a:
            k2d, v2d = _kv(x.reshape(m, d), kv_norm, bf(w_kv),
                           jnp.tile(k_norm, N_HEADS).reshape(1, -1), pool)
            k_sh = k2d.reshape(b, s, -1)
            v_sh = v2d.reshape(b, s // ATT_T, ATT_T, -1, LANES).transpose(0, 3, 1, 4, 2)
        x = _ffn(x.reshape(m, d), ffn1_norm[l], bf(ffn1_w13[l]), bf(ffn1_w2[l])).reshape(b, s, d)
        if l < n_a:
            w_ri = bf(jnp.concatenate([a_w_r[l], a_w_i[l]], axis=-1))
            x = _rglru(x, mix_norm[l], bf(a_w_in[l]), a_conv_w[l], a_conv_b[l], w_ri, a_b_r[l],
                       a_b_i[l], a_lambda[l], bf(a_w_out[l]))
        else:
            j = l - n_a
            q = _qproj(x.reshape(m, d), mix_norm[l], bf(b_w_q[j]),
                       jnp.tile(q_norm[j], N_HEADS).reshape(1, -1), pool)
            o = _attention(q.reshape(b, s, -1), k_sh, v_sh, tri)
            x = _oproj(o.reshape(m, -1), bf(b_w_o[j]), x.reshape(m, d)).reshape(b, s, d)
        x = _ffn(x.reshape(m, d), ffn2_norm[l], bf(ffn2_w13[l]), bf(ffn2_w2[l])).reshape(b, s, d)
    return x
```

```python
import jax
import jax.numpy as jnp
import numpy as np
from jax import lax
from jax.experimental import pallas as pl
from jax.experimental.pallas import tpu as pltpu

EPS = 1e-6
N_HEADS = 16
HEAD_DIM = 64
CONV_W = 4
LRU_C = 8.0
LANES = 128
SUBLANES = 8
VMEM_LIMIT = 56 * 1024 * 1024

FFN_TM = 512
FF_CHUNK = 256
LRU_TS = 256
ATT_T = 256
ATT_PAIRS = 4
POOL_W = 256
LOG_W_ZERO = -104.0

F32 = jnp.float32
BF16 = jnp.bfloat16


def _rms_norm_rows(x, gain_row):
    ms = jnp.mean(x * x, axis=-1, keepdims=True)
    return x * lax.rsqrt(ms + EPS) * gain_row


def _sigmoid(x):
    return 1.0 / (1.0 + jnp.exp(-x))


def _split_bf16(x):
    hi = x.astype(BF16)
    lo = (x - hi.astype(F32)).astype(BF16)
    return hi, lo


def _const_spec(shape):
    nd = len(shape)
    return pl.BlockSpec(shape, lambda *_: (0,) * nd, pipeline_mode=pl.Buffered(1))


def _half_swiglu_step(x, g_ref, w13_ref, w2_ref, act_ref):
    d_ff = w2_ref.shape[0]
    xn = _rms_norm_rows(x, g_ref[...]).astype(BF16)
    for c in range(d_ff // FF_CHUNK):
        lo = c * FF_CHUNK
        gate = jnp.dot(xn, w13_ref[:, lo:lo + FF_CHUNK], preferred_element_type=F32)
        up = jnp.dot(xn, w13_ref[:, d_ff + lo:d_ff + lo + FF_CHUNK], preferred_element_type=F32)
        act_ref[:, lo:lo + FF_CHUNK] = (gate * _sigmoid(gate) * up).astype(BF16)
    y = jnp.dot(act_ref[...], w2_ref[...], preferred_element_type=F32)
    return x + 0.5 * y


def _ffn_kernel(x_ref, g_ref, w13_ref, w2_ref, y_ref, act_ref):
    y_ref[...] = _half_swiglu_step(x_ref[...], g_ref, w13_ref, w2_ref, act_ref)


def _ffn_kv_kernel(x_ref, g_ref, w13_ref, w2_ref, ng_ref, wkv_ref, kg_ref, pool_ref,
                   y_ref, k_ref, vt_ref, act_ref):
    d_attn = k_ref.shape[1]
    y = _half_swiglu_step(x_ref[...], g_ref, w13_ref, w2_ref, act_ref)
    y_ref[...] = y
    h = _rms_norm_rows(y, ng_ref[...]).astype(BF16)
    kv = jnp.dot(h, wkv_ref[...], preferred_element_type=F32)
    k_ref[...] = _head_rms_norm(kv[:, :d_attn], kg_ref[...], pool_ref).astype(BF16)
    for pair in range(vt_ref.shape[0]):
        for blk in range(vt_ref.shape[1]):
            v_blk = kv[blk * ATT_T:(blk + 1) * ATT_T,
                       d_attn + pair * LANES:d_attn + (pair + 1) * LANES]
            vt_ref[pair, blk] = v_blk.T.astype(BF16)


def _ffn_q_kernel(x_ref, g_ref, w13_ref, w2_ref, ng_ref, wq_ref, qg_ref, pool_ref,
                  y_ref, q_ref, act_ref):
    y = _half_swiglu_step(x_ref[...], g_ref, w13_ref, w2_ref, act_ref)
    y_ref[...] = y
    h = _rms_norm_rows(y, ng_ref[...]).astype(BF16)
    q = jnp.dot(h, wq_ref[...], preferred_element_type=F32)
    q_ref[...] = (_head_rms_norm(q, qg_ref[...], pool_ref) * (HEAD_DIM ** -0.5)).astype(BF16)


def _oproj_ffn_kernel(x_ref, o_ref, wo_ref, g_ref, w13_ref, w2_ref, y_ref, act_ref):
    x = x_ref[...] + jnp.dot(o_ref[...], wo_ref[...], preferred_element_type=F32)
    y_ref[...] = _half_swiglu_step(x, g_ref, w13_ref, w2_ref, act_ref)


def _ffn_call(body, name, x2d, gain, w13, w2, extra_in=(), extra_in_specs=(), lead_in=(),
              lead_in_specs=(), extra_out_shapes=(), extra_out_specs=()):
    m, d = x2d.shape
    d_ff = w2.shape[0]
    row_spec = pl.BlockSpec((FFN_TM, d), lambda i: (i, 0))
    out = pl.pallas_call(
        body,
        grid=(m // FFN_TM,),
        in_specs=[row_spec, *lead_in_specs, _const_spec((1, d)), _const_spec((d, 2 * d_ff)),
                  _const_spec((d_ff, d)), *extra_in_specs],
        out_specs=[row_spec, *extra_out_specs],
        out_shape=[jax.ShapeDtypeStruct((m, d), F32), *extra_out_shapes],
        scratch_shapes=[pltpu.VMEM((FFN_TM, d_ff), BF16)],
        compiler_params=pltpu.CompilerParams(
            dimension_semantics=("parallel",), vmem_limit_bytes=VMEM_LIMIT),
        name=name,
    )(x2d, *lead_in, gain.reshape(1, d), w13, w2, *extra_in)
    return out[0] if len(out) == 1 else out


def _ffn(x2d, gain, w13, w2):
    return _ffn_call(_ffn_kernel, "ffn", x2d, gain, w13, w2)


def _ffn_kv(x2d, gain, w13, w2, kv_gain, w_kv, k_gain_row, pool, seq_len):
    m, d = x2d.shape
    d_attn = w_kv.shape[1] // 2
    n_pairs = d_attn // LANES
    blk_per_tile = FFN_TM // ATT_T
    tiles_per_seq = seq_len // FFN_TM
    vt_spec = pl.BlockSpec((None, n_pairs, blk_per_tile, LANES, ATT_T),
                           lambda i: (i // tiles_per_seq, 0, i % tiles_per_seq, 0, 0))
    return _ffn_call(
        _ffn_kv_kernel, "ffn_kv", x2d, gain, w13, w2,
        extra_in=(kv_gain.reshape(1, d), w_kv, k_gain_row, pool),
        extra_in_specs=(_const_spec((1, d)), _const_spec((d, 2 * d_attn)),
                        _const_spec((1, d_attn)), _const_spec((POOL_W, POOL_W))),
        extra_out_shapes=(jax.ShapeDtypeStruct((m, d_attn), BF16),
                          jax.ShapeDtypeStruct((m // seq_len, n_pairs, seq_len // ATT_T, LANES, ATT_T),
                                               BF16)),
        extra_out_specs=(pl.BlockSpec((FFN_TM, d_attn), lambda i: (i, 0)), vt_spec))


def _ffn_q(x2d, gain, w13, w2, q_in_gain, w_q, q_gain_row, pool):
    m, d = x2d.shape
    d_attn = w_q.shape[1]
    return _ffn_call(
        _ffn_q_kernel, "ffn_q", x2d, gain, w13, w2,
        extra_in=(q_in_gain.reshape(1, d), w_q, q_gain_row, pool),
        extra_in_specs=(_const_spec((1, d)), _const_spec((d, d_attn)),
                        _const_spec((1, d_attn)), _const_spec((POOL_W, POOL_W))),
        extra_out_shapes=(jax.ShapeDtypeStruct((m, d_attn), BF16),),
        extra_out_specs=(pl.BlockSpec((FFN_TM, d_attn), lambda i: (i, 0)),))


def _oproj_ffn(x2d, o2d, w_o, gain, w13, w2):
    d_attn = o2d.shape[1]
    return _ffn_call(
        _oproj_ffn_kernel, "oproj_ffn", x2d, gain, w13, w2,
        lead_in=(o2d, w_o),
        lead_in_specs=(pl.BlockSpec((FFN_TM, d_attn), lambda i: (i, 0)),
                       _const_spec((d_attn, x2d.shape[1]))))


def _gelu_tanh(x):
    c = np.float32(np.sqrt(2.0 / np.pi))
    return x * (0.5 * (1.0 + jnp.tanh(c * (x + 0.044715 * (x * x * x)))))


def _rglru_kernel(x_ref, g_ref, win_ref, cw_ref, cb_ref, wri_ref, br_ref, bi_ref, lam_ref,
                  wout_ref, o_ref, rec_ref, a_ref, u_ref, gate_ref, hcar_ref):
    ts, w = x_ref.shape
    n_blocks, bw, _ = wri_ref.shape

    @pl.when(pl.program_id(1) == 0)
    def _():
        rec_ref[0:SUBLANES, :] = jnp.zeros((SUBLANES, w), F32)
        hcar_ref[...] = jnp.zeros_like(hcar_ref)

    x = x_ref[...]
    h = _rms_norm_rows(x, g_ref[...]).astype(BF16)
    proj = jnp.dot(h, win_ref[...], preferred_element_type=F32)
    gate_ref[...] = _gelu_tanh(proj[:, :w])
    rec = proj[:, w:]
    rec_ref[SUBLANES:SUBLANES + ts, :] = rec
    xc = cb_ref[...] + cw_ref[CONV_W - 1:CONV_W, :] * rec
    for k in range(CONV_W - 1):
        back = CONV_W - 1 - k
        xc = xc + cw_ref[k:k + 1, :] * rec_ref[SUBLANES - back:SUBLANES - back + ts, :]
    rec_ref[0:SUBLANES, :] = rec_ref[ts:ts + SUBLANES, :]

    xcb = xc.astype(BF16)
    sp_lam = jnp.maximum(-lam_ref[...], 0.0) + jnp.log(1.0 + jnp.exp(-jnp.abs(lam_ref[...])))
    for n in range(n_blocks):
        sl = slice(n * bw, (n + 1) * bw)
        ri = jnp.dot(xcb[:, sl], wri_ref[n], preferred_element_type=F32)
        r = _sigmoid(ri[:, :bw] + br_ref[:, sl])
        i = _sigmoid(ri[:, bw:] + bi_ref[:, sl])
        log_a = (-LRU_C) * r * sp_lam[:, sl]
        a = jnp.exp(log_a)
        a_ref[:, sl] = a
        u_ref[:, sl] = jnp.sqrt(-jnp.tanh(log_a) * (1.0 + a * a)) * (i * xc[:, sl])

    row = lax.broadcasted_iota(jnp.int32, (SUBLANES, w), 0)

    def scan_group(gidx, hprev):
        r0 = pl.multiple_of(gidx * SUBLANES, SUBLANES)
        a = a_ref[pl.ds(r0, SUBLANES), :]
        u = u_ref[pl.ds(r0, SUBLANES), :]
        for sh in (1, 2, 4):
            keep = row >= sh
            u_new = jnp.where(keep, u + a * pltpu.roll(u, sh, 0), u)
            a = jnp.where(keep, a * pltpu.roll(a, sh, 0), a)
            u = u_new
        hs = u + a * hprev
        u_ref[pl.ds(r0, SUBLANES), :] = hs
        return hs[SUBLANES - 1:SUBLANES, :]

    hcar_ref[...] = lax.fori_loop(0, ts // SUBLANES, scan_group, hcar_ref[...])

    y = (u_ref[...] * gate_ref[...]).astype(BF16)
    o_ref[...] = x + jnp.dot(y, wout_ref[...], preferred_element_type=F32)


def _rglru(x, gain, w_in, conv_w, conv_b, w_ri, b_r, b_i, lam, w_out):
    b, s, d = x.shape
    w = w_out.shape[0]
    n_blocks, bw, _ = w_ri.shape
    row_spec = pl.BlockSpec((None, LRU_TS, d), lambda bi, si: (bi, si, 0))
    return pl.pallas_call(
        _rglru_kernel,
        grid=(b, s // LRU_TS),
        in_specs=[
            row_spec,
            _const_spec((1, d)),
            _const_spec((d, 2 * w)),
            _const_spec((CONV_W, w)),
            _const_spec((1, w)),
            _const_spec((n_blocks, bw, 2 * bw)),
            _const_spec((1, w)),
            _const_spec((1, w)),
            _const_spec((1, w)),
            _const_spec((w, d)),
        ],
        out_specs=row_spec,
        out_shape=jax.ShapeDtypeStruct((b, s, d), F32),
        scratch_shapes=[
            pltpu.VMEM((LRU_TS + SUBLANES, w), F32),
            pltpu.VMEM((LRU_TS, w), F32),
            pltpu.VMEM((LRU_TS, w), F32),
            pltpu.VMEM((LRU_TS, w), F32),
            pltpu.VMEM((1, w), F32),
        ],
        compiler_params=pltpu.CompilerParams(
            dimension_semantics=("parallel", "arbitrary"), vmem_limit_bytes=VMEM_LIMIT),
        name="rglru",
    )(x, gain.reshape(1, d), w_in, conv_w, conv_b.reshape(1, w), w_ri, b_r.reshape(1, w),
      b_i.reshape(1, w), lam.reshape(1, w), w_out)


def _head_rms_norm(t, gain_row, pool_ref):
    outs = []
    for g in range(t.shape[1] // POOL_W):
        tg = t[:, g * POOL_W:(g + 1) * POOL_W]
        ms = jnp.dot((tg * tg).astype(BF16), pool_ref[...], preferred_element_type=F32)
        outs.append(tg * lax.rsqrt(ms + EPS))
    return jnp.concatenate(outs, axis=1) * gain_row


def _head_pool_matrix():
    idx = np.arange(POOL_W) // HEAD_DIM
    return jnp.asarray((idx[:, None] == idx[None, :]).astype(np.float32) / HEAD_DIM, BF16)


def _attn_kernel(q_ref, k_ref, vt_ref, tri_ref, o_ref, acc_ref, carry_ref):
    t = ATT_T
    qi = pl.program_id(2)
    lane = lax.broadcasted_iota(jnp.int32, (t, LANES), 1)
    k_pos = lax.broadcasted_iota(jnp.int32, (t, t), 0)
    q_pos = lax.broadcasted_iota(jnp.int32, (t, t), 1)
    causal = k_pos < q_pos
    n_heads = 2 * ATT_PAIRS
    q_heads = []
    for pair in range(ATT_PAIRS):
        q = q_ref[:, pair * LANES:(pair + 1) * LANES]
        q_heads += [jnp.where(lane < HEAD_DIM, q, jnp.zeros_like(q)),
                    jnp.where(lane >= HEAD_DIM, q, jnp.zeros_like(q))]

    h = t // 2

    def blocks(specs):
        zs = []
        for head, kj, _, _ in specs:
            pair = head // 2
            kb = k_ref[pl.ds(pl.multiple_of(kj * t, t), t), pair * LANES:(pair + 1) * LANES]
            zs.append(lax.dot_general(kb, q_heads[head], (((1,), (1,)), ((), ())),
                                      preferred_element_type=F32))
        def halves(z, masked):
            return [(z[:h], causal[:h], 0), (z[h:, h:], causal[h:, h:], h)] if masked else \
                   [(z[:h], None, 0), (z[h:], None, 0)]

        def widen(x, q0):
            return x if q0 == 0 else jnp.concatenate([jnp.zeros((h, q0), x.dtype), x], axis=1)

        exts = []
        for z, (_, _, masked, _) in zip(zs, specs):
            ext = []
            for zp, mask, q0 in halves(z, masked):
                sp = jnp.maximum(zp, 0.0) + jnp.log(1.0 + jnp.exp(-jnp.abs(zp)))
                hi, lo = _split_bf16(sp if mask is None else jnp.where(mask, sp, 0.0))
                ext.append(jnp.dot(tri_ref[...],
                                   jnp.concatenate([widen(hi, q0), widen(lo, q0)], axis=0),
                                   preferred_element_type=F32))
            exts.append(ext)
        ws, carries = [], []
        for z, (ext_early, ext_late), (_, _, masked, carry) in zip(zs, exts, specs):
            if isinstance(carry, int):
                carry = carries[carry]
            carry_early = ext_late[h:] if carry is None else carry + ext_late[h:]
            w_halves = []
            for (zp, mask, q0), ext, c in zip(halves(z, masked), (ext_early, ext_late),
                                              (carry_early, carry)):
                log_w = zp + ext[:h, q0:]
                if c is not None:
                    log_w = log_w + pltpu.repeat(c[:, q0:], h // SUBLANES, axis=0)
                w = jnp.exp(log_w)
                if mask is not None:
                    w = jnp.where(mask, w, 0.0)
                w_halves.append(widen(w.astype(BF16), q0))
            ws.append(jnp.concatenate(w_halves, axis=0))
            carries.append(carry_early + ext_early[h:])
        pvs = [jnp.dot(vt_ref[head // 2, kj], w, preferred_element_type=F32)
               for w, (head, kj, _, _) in zip(ws, specs)]
        return list(zip(pvs, carries))

    @pl.when(qi == 0)
    def _():
        out = blocks([(head, qi, True, None) for head in range(n_heads)])
        for head, (pv, carry) in enumerate(out):
            acc_ref[head] = pv
            carry_ref[head] = carry

    @pl.when(qi > 0)
    def _():
        out = blocks([(head, qi, True, None) for head in range(n_heads)]
                     + [(head, qi - 1, False, head) for head in range(n_heads)])
        for head in range(n_heads):
            acc_ref[head] = out[head][0] + out[n_heads + head][0]
            carry_ref[head] = out[n_heads + head][1]

    def max_carry(carries):
        top = carries[0]
        for carry in carries[1:]:
            top = jnp.maximum(top, carry)
        return jnp.max(top)

    def cond(state):
        kj, top = state
        return jnp.logical_and(kj >= 0, top > LOG_W_ZERO)

    def body(state):
        kj, _ = state
        out = blocks([(head, kj, False, carry_ref[head]) for head in range(n_heads)])
        for head, (pv, carry) in enumerate(out):
            acc_ref[head] += pv
            carry_ref[head] = carry
        return kj - 1, max_carry([carry for _, carry in out])

    top = max_carry([carry_ref[head] for head in range(n_heads)])
    lax.while_loop(cond, body, (qi - 2, top))

    feat = lax.broadcasted_iota(jnp.int32, (LANES, t), 0)
    for pair in range(ATT_PAIRS):
        o_t = jnp.where(feat < HEAD_DIM, acc_ref[2 * pair], acc_ref[2 * pair + 1])
        o_ref[:, pair * LANES:(pair + 1) * LANES] = o_t.T.astype(o_ref.dtype)


def _suffix_sum_matrix():
    j = np.arange(ATT_T // 2)
    later = (j[None, :] >= j[:, None]).astype(np.float32)
    one_part = np.concatenate([later, np.ones((SUBLANES, ATT_T // 2), np.float32)], axis=0)
    return jnp.asarray(-np.concatenate([one_part, one_part], axis=1), BF16)


def _attention(q, k, vt, tri):
    b, s, d_attn = q.shape
    width = ATT_PAIRS * LANES
    k_spec = pl.BlockSpec((None, s, width), lambda bi, pi, qi: (bi, 0, pi))
    vt_spec = pl.BlockSpec((None, ATT_PAIRS, s // ATT_T, LANES, ATT_T),
                           lambda bi, pi, qi: (bi, pi, 0, 0, 0))
    q_spec = pl.BlockSpec((None, ATT_T, width), lambda bi, pi, qi: (bi, qi, pi))
    return pl.pallas_call(
        _attn_kernel,
        grid=(b, d_attn // width, s // ATT_T),
        in_specs=[q_spec, k_spec, vt_spec, _const_spec(tri.shape)],
        out_specs=q_spec,
        out_shape=jax.ShapeDtypeStruct((b, s, d_attn), BF16),
        scratch_shapes=[
            pltpu.VMEM((2 * ATT_PAIRS, LANES, ATT_T), F32),
            pltpu.VMEM((2 * ATT_PAIRS, SUBLANES, ATT_T), F32),
        ],
        compiler_params=pltpu.CompilerParams(
            dimension_semantics=("parallel", "parallel", "arbitrary"),
            vmem_limit_bytes=VMEM_LIMIT),
        name="attn",
    )(q, k, vt, tri)


def kernel(x, ffn1_norm, ffn1_w13, ffn1_w2, mix_norm, a_w_in, a_conv_w, a_conv_b, a_w_r, a_b_r,
           a_w_i, a_b_i, a_lambda, a_w_out, kv_norm, w_kv, k_norm, b_w_q, q_norm, b_w_o,
           ffn2_norm, ffn2_w13, ffn2_w2):
    b, s, d = x.shape
    depth = ffn1_norm.shape[0]
    n_a = a_w_in.shape[0]
    m = b * s
    bf = lambda t: t.astype(BF16)
    pool = _head_pool_matrix()
    tri = _suffix_sum_matrix()

    assert 1 <= n_a < depth and s % FFN_TM == 0 and FFN_TM % ATT_T == 0
    x = x.reshape(m, d)
    k_sh = vt_sh = None
    for l in range(depth):
        ffn1 = (ffn1_norm[l], bf(ffn1_w13[l]), bf(ffn1_w2[l]))
        ffn2 = (ffn2_norm[l], bf(ffn2_w13[l]), bf(ffn2_w2[l]))
        if l < n_a:
            x = _ffn(x, *ffn1)
            w_ri = bf(jnp.concatenate([a_w_r[l], a_w_i[l]], axis=-1))
            x = _rglru(x.reshape(b, s, d), mix_norm[l], bf(a_w_in[l]), a_conv_w[l], a_conv_b[l],
                       w_ri, a_b_r[l], a_b_i[l], a_lambda[l], bf(a_w_out[l])).reshape(m, d)
            if l == n_a - 1:
                x, k2d, vt_sh = _ffn_kv(x, *ffn2, kv_norm, bf(w_kv),
                                        jnp.tile(k_norm, N_HEADS).reshape(1, -1), pool, s)
                k_sh = k2d.reshape(b, s, -1)
            else:
                x = _ffn(x, *ffn2)
        else:
            j = l - n_a
            x, q = _ffn_q(x, *ffn1, mix_norm[l], bf(b_w_q[j]),
                          jnp.tile(q_norm[j], N_HEADS).reshape(1, -1), pool)
            o = _attention(q.reshape(b, s, -1), k_sh, vt_sh, tri)
            x = _oproj_ffn(x, o.reshape(m, -1), bf(b_w_o[j]), *ffn2)
    return x.reshape(b, s, d)
```

```python
import functools

import jax
import jax.numpy as jnp
import numpy as np
from jax import lax
from jax.experimental import pallas as pl
from jax.experimental.pallas import tpu as pltpu

EPS = 1e-6
N_HEADS = 16
HEAD_DIM = 64
CONV_W = 4
LRU_C = 8.0
LANES = 128
SUBLANES = 8
BF16_ROWS = 16
VMEM_LIMIT = 56 * 1024 * 1024

FFN_TM = 512
FF_CHUNK = 256
LRU_TS = 256
ATT_T = 256
ATT_PAIRS = 4
POOL_W = 256
LOG_W_ZERO = -104.0

F32 = jnp.float32
BF16 = jnp.bfloat16


def _rms_norm_rows(x, gain_row):
    ms = jnp.mean(x * x, axis=-1, keepdims=True)
    return x * lax.rsqrt(ms + EPS) * gain_row


def _sigmoid(x):
    return 1.0 / (1.0 + jnp.exp(-x))


def _split_bf16(x):
    hi = x.astype(BF16)
    lo = (x - hi.astype(F32)).astype(BF16)
    return hi, lo


def _const_spec(shape):
    nd = len(shape)
    return pl.BlockSpec(shape, lambda *_: (0,) * nd, pipeline_mode=pl.Buffered(1))


def _half_swiglu_step(x, g_ref, w13_ref, w2_ref, act_ref):
    d_ff = w2_ref.shape[0]
    xn = _rms_norm_rows(x, g_ref[...]).astype(BF16)
    for c in range(d_ff // FF_CHUNK):
        lo = c * FF_CHUNK
        gate = jnp.dot(xn, w13_ref[:, lo:lo + FF_CHUNK], preferred_element_type=F32)
        up = jnp.dot(xn, w13_ref[:, d_ff + lo:d_ff + lo + FF_CHUNK], preferred_element_type=F32)
        act_ref[:, lo:lo + FF_CHUNK] = (gate * _sigmoid(gate) * up).astype(BF16)
    y = jnp.dot(act_ref[...], w2_ref[...], preferred_element_type=F32)
    return x + 0.5 * y


def _ffn_kernel(x_ref, g_ref, w13_ref, w2_ref, y_ref, act_ref):
    y_ref[...] = _half_swiglu_step(x_ref[...], g_ref, w13_ref, w2_ref, act_ref)


def _ffn_kv_kernel(x_ref, g_ref, w13_ref, w2_ref, ng_ref, wkv_ref, kg_ref, pool_ref,
                   y_ref, k_ref, vt_ref, act_ref):
    d_attn = k_ref.shape[1]
    y = _half_swiglu_step(x_ref[...], g_ref, w13_ref, w2_ref, act_ref)
    y_ref[...] = y
    h = _rms_norm_rows(y, ng_ref[...]).astype(BF16)
    kv = jnp.dot(h, wkv_ref[...], preferred_element_type=F32)
    k_ref[...] = _head_rms_norm(kv[:, :d_attn], kg_ref[...], pool_ref).astype(BF16)
    for pair in range(vt_ref.shape[0]):
        for blk in range(vt_ref.shape[1]):
            v_blk = kv[blk * ATT_T:(blk + 1) * ATT_T,
                       d_attn + pair * LANES:d_attn + (pair + 1) * LANES]
            vt_ref[pair, blk] = v_blk.T.astype(BF16)


def _ffn_q_kernel(x_ref, g_ref, w13_ref, w2_ref, ng_ref, wq_ref, qg_ref, pool_ref,
                  y_ref, q_ref, act_ref):
    y = _half_swiglu_step(x_ref[...], g_ref, w13_ref, w2_ref, act_ref)
    y_ref[...] = y
    h = _rms_norm_rows(y, ng_ref[...]).astype(BF16)
    q = jnp.dot(h, wq_ref[...], preferred_element_type=F32)
    q_ref[...] = (_head_rms_norm(q, qg_ref[...], pool_ref) * (HEAD_DIM ** -0.5)).astype(BF16)


def _oproj_ffn_kernel(x_ref, o_ref, wo_ref, g_ref, w13_ref, w2_ref, y_ref, act_ref):
    x = x_ref[...] + jnp.dot(o_ref[...], wo_ref[...], preferred_element_type=F32)
    y_ref[...] = _half_swiglu_step(x, g_ref, w13_ref, w2_ref, act_ref)


def _cast_job(src, lead, n_steps):
    rows, cols = src.shape[-2:]
    n_chunks = max(k for k in range(1, n_steps + 1)
                   if rows % k == 0 and (rows // k) % BF16_ROWS == 0)
    chunk = rows // n_chunks
    if lead is None:
        in_spec = pl.BlockSpec((chunk, cols), lambda i: (jnp.minimum(i, n_chunks - 1), 0))
    else:
        in_spec = pl.BlockSpec((None, chunk, cols),
                               lambda i: (lead, jnp.minimum(i, n_chunks - 1), 0))
    out_spec = pl.BlockSpec((chunk, cols), lambda i: (jnp.minimum(i, n_chunks - 1), 0))
    return in_spec, out_spec, jax.ShapeDtypeStruct((rows, cols), BF16)


def _ffn_body(*refs, core, n_in, n_out, n_cast):
    ins, rest = refs[:n_in], refs[n_in:]
    cast_in, rest = rest[:n_cast], rest[n_cast:]
    outs, rest = rest[:n_out], rest[n_out:]
    cast_out, scratch = rest[:n_cast], rest[n_cast:]
    for src_ref, dst_ref in zip(cast_in, cast_out):
        dst_ref[...] = src_ref[...].astype(BF16)
    core(*ins, *outs, *scratch)


def _ffn_call(body, name, x2d, gain, w13, w2, casts, extra_in=(), extra_in_specs=(), lead_in=(),
              lead_in_specs=(), extra_out_shapes=(), extra_out_specs=()):
    m, d = x2d.shape
    d_ff = w2.shape[0]
    n_steps = m // FFN_TM
    row_spec = pl.BlockSpec((FFN_TM, d), lambda i: (i, 0))
    jobs = [_cast_job(src, lead, n_steps) for src, lead in casts]
    in_specs = [row_spec, *lead_in_specs, _const_spec((1, d)), _const_spec((d, 2 * d_ff)),
                _const_spec((d_ff, d)), *extra_in_specs]
    n_out = 1 + len(extra_out_shapes)
    out = pl.pallas_call(
        functools.partial(_ffn_body, core=body, n_in=len(in_specs), n_out=n_out,
                          n_cast=len(jobs)),
        grid=(n_steps,),
        in_specs=in_specs + [job[0] for job in jobs],
        out_specs=[row_spec, *extra_out_specs] + [job[1] for job in jobs],
        out_shape=[jax.ShapeDtypeStruct((m, d), F32), *extra_out_shapes] + [job[2] for job in jobs],
        scratch_shapes=[pltpu.VMEM((FFN_TM, d_ff), BF16)],
        compiler_params=pltpu.CompilerParams(
            dimension_semantics=("arbitrary",), vmem_limit_bytes=VMEM_LIMIT),
        name=name,
    )(x2d, *lead_in, gain.reshape(1, d), w13, w2, *extra_in, *(src for src, _ in casts))
    return out[:n_out], out[n_out:]


def _ffn(x2d, gain, w13, w2, casts):
    return _ffn_call(_ffn_kernel, "ffn", x2d, gain, w13, w2, casts)


def _ffn_kv(x2d, gain, w13, w2, casts, kv_gain, w_kv, k_gain_row, pool, seq_len):
    m, d = x2d.shape
    d_attn = w_kv.shape[1] // 2
    n_pairs = d_attn // LANES
    blk_per_tile = FFN_TM // ATT_T
    tiles_per_seq = seq_len // FFN_TM
    vt_spec = pl.BlockSpec((None, n_pairs, blk_per_tile, LANES, ATT_T),
                           lambda i: (i // tiles_per_seq, 0, i % tiles_per_seq, 0, 0))
    return _ffn_call(
        _ffn_kv_kernel, "ffn_kv", x2d, gain, w13, w2, casts,
        extra_in=(kv_gain.reshape(1, d), w_kv, k_gain_row, pool),
        extra_in_specs=(_const_spec((1, d)), _const_spec((d, 2 * d_attn)),
                        _const_spec((1, d_attn)), _const_spec((POOL_W, POOL_W))),
        extra_out_shapes=(jax.ShapeDtypeStruct((m, d_attn), BF16),
                          jax.ShapeDtypeStruct((m // seq_len, n_pairs, seq_len // ATT_T, LANES, ATT_T),
                                               BF16)),
        extra_out_specs=(pl.BlockSpec((FFN_TM, d_attn), lambda i: (i, 0)), vt_spec))


def _ffn_q(x2d, gain, w13, w2, casts, q_in_gain, w_q, q_gain_row, pool):
    m, d = x2d.shape
    d_attn = w_q.shape[1]
    return _ffn_call(
        _ffn_q_kernel, "ffn_q", x2d, gain, w13, w2, casts,
        extra_in=(q_in_gain.reshape(1, d), w_q, q_gain_row, pool),
        extra_in_specs=(_const_spec((1, d)), _const_spec((d, d_attn)),
                        _const_spec((1, d_attn)), _const_spec((POOL_W, POOL_W))),
        extra_out_shapes=(jax.ShapeDtypeStruct((m, d_attn), BF16),),
        extra_out_specs=(pl.BlockSpec((FFN_TM, d_attn), lambda i: (i, 0)),))


def _oproj_ffn(x2d, gain, w13, w2, casts, o2d, w_o):
    d_attn = o2d.shape[1]
    return _ffn_call(
        _oproj_ffn_kernel, "oproj_ffn", x2d, gain, w13, w2, casts,
        lead_in=(o2d, w_o),
        lead_in_specs=(pl.BlockSpec((FFN_TM, d_attn), lambda i: (i, 0)),
                       _const_spec((d_attn, x2d.shape[1]))))


def _gelu_tanh(x):
    c = np.float32(np.sqrt(2.0 / np.pi))
    return x * (0.5 * (1.0 + jnp.tanh(c * (x + 0.044715 * (x * x * x)))))


def _rglru_kernel(x_ref, g_ref, win_ref, cw_ref, cb_ref, wr_ref, wi_ref, br_ref, bi_ref, lam_ref,
                  wout_ref, o_ref, rec_ref, a_ref, u_ref, gate_ref, hcar_ref):
    ts, w = x_ref.shape
    bw = wr_ref.shape[1]
    n_blocks = w // bw

    @pl.when(pl.program_id(1) == 0)
    def _():
        rec_ref[0:SUBLANES, :] = jnp.zeros((SUBLANES, w), F32)
        hcar_ref[...] = jnp.zeros_like(hcar_ref)

    x = x_ref[...]
    h = _rms_norm_rows(x, g_ref[...]).astype(BF16)
    proj = jnp.dot(h, win_ref[...], preferred_element_type=F32)
    gate_ref[...] = _gelu_tanh(proj[:, :w])
    rec = proj[:, w:]
    rec_ref[SUBLANES:SUBLANES + ts, :] = rec
    xc = cb_ref[...] + cw_ref[CONV_W - 1:CONV_W, :] * rec
    for k in range(CONV_W - 1):
        back = CONV_W - 1 - k
        xc = xc + cw_ref[k:k + 1, :] * rec_ref[SUBLANES - back:SUBLANES - back + ts, :]
    rec_ref[0:SUBLANES, :] = rec_ref[ts:ts + SUBLANES, :]

    xcb = xc.astype(BF16)
    sp_lam = jnp.maximum(-lam_ref[...], 0.0) + jnp.log(1.0 + jnp.exp(-jnp.abs(lam_ref[...])))
    for n in range(n_blocks):
        sl = slice(n * bw, (n + 1) * bw)
        r = _sigmoid(jnp.dot(xcb[:, sl], wr_ref[sl, :], preferred_element_type=F32) + br_ref[:, sl])
        i = _sigmoid(jnp.dot(xcb[:, sl], wi_ref[sl, :], preferred_element_type=F32) + bi_ref[:, sl])
        log_a = (-LRU_C) * r * sp_lam[:, sl]
        a = jnp.exp(log_a)
        a_ref[:, sl] = a
        u_ref[:, sl] = jnp.sqrt(-jnp.tanh(log_a) * (1.0 + a * a)) * (i * xc[:, sl])

    row = lax.broadcasted_iota(jnp.int32, (SUBLANES, w), 0)

    def scan_group(gidx, hprev):
        r0 = pl.multiple_of(gidx * SUBLANES, SUBLANES)
        a = a_ref[pl.ds(r0, SUBLANES), :]
        u = u_ref[pl.ds(r0, SUBLANES), :]
        for sh in (1, 2, 4):
            keep = row >= sh
            u_new = jnp.where(keep, u + a * pltpu.roll(u, sh, 0), u)
            a = jnp.where(keep, a * pltpu.roll(a, sh, 0), a)
            u = u_new
        hs = u + a * hprev
        u_ref[pl.ds(r0, SUBLANES), :] = hs
        return hs[SUBLANES - 1:SUBLANES, :]

    hcar_ref[...] = lax.fori_loop(0, ts // SUBLANES, scan_group, hcar_ref[...])

    y = (u_ref[...] * gate_ref[...]).astype(BF16)
    o_ref[...] = x + jnp.dot(y, wout_ref[...], preferred_element_type=F32)


def _rglru(x, gain, w_in, conv_w, conv_b, w_r, w_i, b_r, b_i, lam, w_out):
    b, s, d = x.shape
    w = w_out.shape[0]
    bw = w_r.shape[1]
    row_spec = pl.BlockSpec((None, LRU_TS, d), lambda bi, si: (bi, si, 0))
    return pl.pallas_call(
        _rglru_kernel,
        grid=(b, s // LRU_TS),
        in_specs=[
            row_spec,
            _const_spec((1, d)),
            _const_spec((d, 2 * w)),
            _const_spec((CONV_W, w)),
            _const_spec((1, w)),
            _const_spec((w, bw)),
            _const_spec((w, bw)),
            _const_spec((1, w)),
            _const_spec((1, w)),
            _const_spec((1, w)),
            _const_spec((w, d)),
        ],
        out_specs=row_spec,
        out_shape=jax.ShapeDtypeStruct((b, s, d), F32),
        scratch_shapes=[
            pltpu.VMEM((LRU_TS + SUBLANES, w), F32),
            pltpu.VMEM((LRU_TS, w), F32),
            pltpu.VMEM((LRU_TS, w), F32),
            pltpu.VMEM((LRU_TS, w), F32),
            pltpu.VMEM((1, w), F32),
        ],
        compiler_params=pltpu.CompilerParams(
            dimension_semantics=("parallel", "arbitrary"), vmem_limit_bytes=VMEM_LIMIT),
        name="rglru",
    )(x, gain.reshape(1, d), w_in, conv_w, conv_b.reshape(1, w), w_r, w_i, b_r.reshape(1, w),
      b_i.reshape(1, w), lam.reshape(1, w), w_out)


def _head_rms_norm(t, gain_row, pool_ref):
    outs = []
    for g in range(t.shape[1] // POOL_W):
        tg = t[:, g * POOL_W:(g + 1) * POOL_W]
        ms = jnp.dot((tg * tg).astype(BF16), pool_ref[...], preferred_element_type=F32)
        outs.append(tg * lax.rsqrt(ms + EPS))
    return jnp.concatenate(outs, axis=1) * gain_row


def _head_pool_matrix():
    idx = np.arange(POOL_W) // HEAD_DIM
    return jnp.asarray((idx[:, None] == idx[None, :]).astype(np.float32) / HEAD_DIM, BF16)


def _attn_kernel(q_ref, k_ref, vt_ref, tri_ref, o_ref, acc_ref, carry_ref):
    t = ATT_T
    qi = pl.program_id(2)
    lane = lax.broadcasted_iota(jnp.int32, (t, LANES), 1)
    k_pos = lax.broadcasted_iota(jnp.int32, (t, t), 0)
    q_pos = lax.broadcasted_iota(jnp.int32, (t, t), 1)
    causal = k_pos < q_pos
    n_heads = 2 * ATT_PAIRS
    q_heads = []
    for pair in range(ATT_PAIRS):
        q = q_ref[:, pair * LANES:(pair + 1) * LANES]
        q_heads += [jnp.where(lane < HEAD_DIM, q, jnp.zeros_like(q)),
                    jnp.where(lane >= HEAD_DIM, q, jnp.zeros_like(q))]

    h = t // 2

    def blocks(specs):
        zs = []
        for head, kj, _, _ in specs:
            pair = head // 2
            kb = k_ref[pl.ds(pl.multiple_of(kj * t, t), t), pair * LANES:(pair + 1) * LANES]
            zs.append(lax.dot_general(kb, q_heads[head], (((1,), (1,)), ((), ())),
                                      preferred_element_type=F32))
        def halves(z, masked):
            return [(z[:h], causal[:h], 0), (z[h:, h:], causal[h:, h:], h)] if masked else \
                   [(z[:h], None, 0), (z[h:], None, 0)]

        def widen(x, q0):
            return x if q0 == 0 else jnp.concatenate([jnp.zeros((h, q0), x.dtype), x], axis=1)

        exts = []
        for z, (_, _, masked, _) in zip(zs, specs):
            ext = []
            for zp, mask, q0 in halves(z, masked):
                sp = jnp.maximum(zp, 0.0) + jnp.log(1.0 + jnp.exp(-jnp.abs(zp)))
                sp16 = (sp if mask is None else jnp.where(mask, sp, 0.0)).astype(BF16)
                ext.append(jnp.dot(tri_ref[...], widen(sp16, q0), preferred_element_type=F32))
            exts.append(ext)
        ws, carries = [], []
        for z, (ext_early, ext_late), (_, _, masked, carry) in zip(zs, exts, specs):
            if isinstance(carry, int):
                carry = carries[carry]
            carry_early = ext_late[h:] if carry is None else carry + ext_late[h:]
            w_halves = []
            for (zp, mask, q0), ext, c in zip(halves(z, masked), (ext_early, ext_late),
                                              (carry_early, carry)):
                log_w = zp + ext[:h, q0:]
                if c is not None:
                    log_w = log_w + pltpu.repeat(c[:, q0:], h // SUBLANES, axis=0)
                w = jnp.exp(log_w)
                if mask is not None:
                    w = jnp.where(mask, w, 0.0)
                w_halves.append(widen(w.astype(BF16), q0))
            ws.append(jnp.concatenate(w_halves, axis=0))
            carries.append(carry_early + ext_early[h:])
        pvs = [jnp.dot(vt_ref[head // 2, kj], w, preferred_element_type=F32)
               for w, (head, kj, _, _) in zip(ws, specs)]
        return list(zip(pvs, carries))

    @pl.when(qi == 0)
    def _():
        out = blocks([(head, qi, True, None) for head in range(n_heads)])
        for head, (pv, carry) in enumerate(out):
            acc_ref[head] = pv
            carry_ref[head] = carry

    @pl.when(qi > 0)
    def _():
        out = blocks([(head, qi, True, None) for head in range(n_heads)]
                     + [(head, qi - 1, False, head) for head in range(n_heads)])
        for head in range(n_heads):
            acc_ref[head] = out[head][0] + out[n_heads + head][0]
            carry_ref[head] = out[n_heads + head][1]

    def max_carry(carries):
        top = carries[0]
        for carry in carries[1:]:
            top = jnp.maximum(top, carry)
        return jnp.max(top)

    def cond(state):
        kj, top = state
        return jnp.logical_and(kj >= 0, top > LOG_W_ZERO)

    def body(state):
        kj, _ = state
        out = blocks([(head, kj, False, carry_ref[head]) for head in range(n_heads)])
        for head, (pv, carry) in enumerate(out):
            acc_ref[head] += pv
            carry_ref[head] = carry
        return kj - 1, max_carry([carry for _, carry in out])

    top = max_carry([carry_ref[head] for head in range(n_heads)])
    lax.while_loop(cond, body, (qi - 2, top))

    feat = lax.broadcasted_iota(jnp.int32, (LANES, t), 0)
    for pair in range(ATT_PAIRS):
        o_t = jnp.where(feat < HEAD_DIM, acc_ref[2 * pair], acc_ref[2 * pair + 1])
        o_ref[:, pair * LANES:(pair + 1) * LANES] = o_t.T.astype(o_ref.dtype)


def _suffix_sum_matrix():
    j = np.arange(ATT_T // 2)
    later = (j[None, :] >= j[:, None]).astype(np.float32)
    one_part = np.concatenate([later, np.ones((SUBLANES, ATT_T // 2), np.float32)], axis=0)
    return jnp.asarray(-one_part, BF16)


def _attention(q, k, vt, tri):
    b, s, d_attn = q.shape
    width = ATT_PAIRS * LANES
    k_spec = pl.BlockSpec((None, s, width), lambda bi, pi, qi: (bi, 0, pi))
    vt_spec = pl.BlockSpec((None, ATT_PAIRS, s // ATT_T, LANES, ATT_T),
                           lambda bi, pi, qi: (bi, pi, 0, 0, 0))
    q_spec = pl.BlockSpec((None, ATT_T, width), lambda bi, pi, qi: (bi, qi, pi))
    return pl.pallas_call(
        _attn_kernel,
        grid=(b, d_attn // width, s // ATT_T),
        in_specs=[q_spec, k_spec, vt_spec, _const_spec(tri.shape)],
        out_specs=q_spec,
        out_shape=jax.ShapeDtypeStruct((b, s, d_attn), BF16),
        scratch_shapes=[
            pltpu.VMEM((2 * ATT_PAIRS, LANES, ATT_T), F32),
            pltpu.VMEM((2 * ATT_PAIRS, SUBLANES, ATT_T), F32),
        ],
        compiler_params=pltpu.CompilerParams(
            dimension_semantics=("parallel", "parallel", "arbitrary"),
            vmem_limit_bytes=VMEM_LIMIT),
        name="attn",
    )(q, k, vt, tri)


def kernel(x, ffn1_norm, ffn1_w13, ffn1_w2, mix_norm, a_w_in, a_conv_w, a_conv_b, a_w_r, a_b_r,
           a_w_i, a_b_i, a_lambda, a_w_out, kv_norm, w_kv, k_norm, b_w_q, q_norm, b_w_o,
           ffn2_norm, ffn2_w13, ffn2_w2):
    b, s, d = x.shape
    depth = ffn1_norm.shape[0]
    n_a = a_w_in.shape[0]
    m = b * s
    pool = _head_pool_matrix()
    tri = _suffix_sum_matrix()

    assert 1 <= n_a < depth and s % FFN_TM == 0 and FFN_TM % ATT_T == 0

    def ffn_sources(l, which):
        w13, w2 = (ffn1_w13, ffn1_w2) if which == 1 else (ffn2_w13, ffn2_w2)
        srcs = {"w13": (w13, l), "w2": (w2, l)}
        if which == 2 and l == n_a - 1:
            srcs["w_kv"] = (w_kv, None)
        if l >= n_a:
            srcs["w_q" if which == 1 else "w_o"] = ((b_w_q if which == 1 else b_w_o), l - n_a)
        return srcs

    def mixer_sources(l):
        if l >= n_a:
            return {}
        lru_w = a_w_out.shape[1]
        return {"w_in": (a_w_in, l), "w_r": (a_w_r.reshape(n_a, lru_w, -1), l),
                "w_i": (a_w_i.reshape(n_a, lru_w, -1), l), "w_out": (a_w_out, l)}

    order = [(l, which) for l in range(depth) for which in (1, 2)]
    ready = {(0, 1, name): (src[lead] if lead is not None else src).astype(BF16)
             for name, (src, lead) in ffn_sources(0, 1).items()}

    x = x.reshape(m, d)
    k_sh = vt_sh = q = o = None
    for idx, (l, which) in enumerate(order):
        jobs = {}
        if which == 1:
            jobs.update({("mix", l, name): v for name, v in mixer_sources(l).items()})
        if idx + 1 < len(order):
            nxt = order[idx + 1]
            jobs.update({(*nxt, name): v for name, v in ffn_sources(*nxt).items()})
        own = {name: ready.pop((l, which, name)) for name in ffn_sources(l, which)}
        gain = (ffn1_norm if which == 1 else ffn2_norm)[l]
        args = (x, gain, own["w13"], own["w2"], list(jobs.values()))
        if which == 1 and l >= n_a:
            (x, q), cast = _ffn_q(*args, mix_norm[l], own["w_q"],
                                  jnp.tile(q_norm[l - n_a], N_HEADS).reshape(1, -1), pool)
        elif which == 2 and l >= n_a:
            (x,), cast = _oproj_ffn(*args, o.reshape(m, -1), own["w_o"])
        elif which == 2 and l == n_a - 1:
            (x, k2d, vt_sh), cast = _ffn_kv(*args, kv_norm, own["w_kv"],
                                            jnp.tile(k_norm, N_HEADS).reshape(1, -1), pool, s)
            k_sh = k2d.reshape(b, s, -1)
        else:
            (x,), cast = _ffn(*args)
        ready.update(zip(jobs.keys(), cast))
        if which == 1 and l < n_a:
            mix = {name: ready.pop(("mix", l, name)) for name in mixer_sources(l)}
            x = _rglru(x.reshape(b, s, d), mix_norm[l], mix["w_in"], a_conv_w[l], a_conv_b[l],
                       mix["w_r"], mix["w_i"], a_b_r[l], a_b_i[l], a_lambda[l],
                       mix["w_out"]).reshape(m, d)
        elif which == 1:
            o = _attention(q.reshape(b, s, -1), k_sh, vt_sh, tri)
    return x.reshape(b, s, d)
```

```python
import functools

import jax
import jax.numpy as jnp
import numpy as np
from jax import lax
from jax.experimental import pallas as pl
from jax.experimental.pallas import tpu as pltpu

EPS = 1e-6
N_HEADS = 16
HEAD_DIM = 64
CONV_W = 4
LRU_C = 8.0
LANES = 128
SUBLANES = 8
BF16_ROWS = 16
VMEM_LIMIT = 56 * 1024 * 1024

FFN_TM = 512
FF_CHUNK = 256
LRU_TS = 256
ATT_T = 256
ATT_PAIRS = 4
POOL_W = 256
LOG_W_ZERO = -104.0

F32 = jnp.float32
BF16 = jnp.bfloat16


def _rms_norm_rows(x, gain_row):
    ms = jnp.mean(x * x, axis=-1, keepdims=True)
    return x * lax.rsqrt(ms + EPS) * gain_row


def _sigmoid(x):
    return 1.0 / (1.0 + jnp.exp(-x))


def _split_bf16(x):
    hi = x.astype(BF16)
    lo = (x - hi.astype(F32)).astype(BF16)
    return hi, lo


def _const_spec(shape):
    nd = len(shape)
    return pl.BlockSpec(shape, lambda *_: (0,) * nd, pipeline_mode=pl.Buffered(1))


def _swiglu_up(xn, w13_ref, act_ref, chunks):
    d_ff = act_ref.shape[1]
    for c in chunks:
        lo = c * FF_CHUNK
        gate = jnp.dot(xn, w13_ref[:, lo:lo + FF_CHUNK], preferred_element_type=F32)
        up = jnp.dot(xn, w13_ref[:, d_ff + lo:d_ff + lo + FF_CHUNK], preferred_element_type=F32)
        act_ref[:, lo:lo + FF_CHUNK] = (gate * _sigmoid(gate) * up).astype(BF16)


def _half_swiglu_step(x, g_ref, w13_ref, w2_ref, act_ref):
    xn = _rms_norm_rows(x, g_ref[...]).astype(BF16)
    _swiglu_up(xn, w13_ref, act_ref, range(w2_ref.shape[0] // FF_CHUNK))
    return x + 0.5 * jnp.dot(act_ref[...], w2_ref[...], preferred_element_type=F32)


def _ffn_kernel(x_ref, g_ref, w13_ref, w2_ref, y_ref, act_ref):
    y_ref[...] = _half_swiglu_step(x_ref[...], g_ref, w13_ref, w2_ref, act_ref)


def _kv_tail(y, ng_ref, wkv_ref, kg_ref, pool_ref, k_ref, vt_ref):
    d_attn = k_ref.shape[1]
    h = _rms_norm_rows(y, ng_ref[...]).astype(BF16)
    kv = jnp.dot(h, wkv_ref[...], preferred_element_type=F32)
    k_ref[...] = _head_rms_norm(kv[:, :d_attn], kg_ref[...], pool_ref).astype(BF16)
    for pair in range(vt_ref.shape[0]):
        for blk in range(vt_ref.shape[1]):
            v_blk = kv[blk * ATT_T:(blk + 1) * ATT_T,
                       d_attn + pair * LANES:d_attn + (pair + 1) * LANES]
            vt_ref[pair, blk] = v_blk.T.astype(BF16)


def _ffn_q_kernel(x_ref, g_ref, w13_ref, w2_ref, ng_ref, wq_ref, qg_ref, pool_ref,
                  y_ref, q_ref, act_ref):
    y = _half_swiglu_step(x_ref[...], g_ref, w13_ref, w2_ref, act_ref)
    y_ref[...] = y
    h = _rms_norm_rows(y, ng_ref[...]).astype(BF16)
    q = jnp.dot(h, wq_ref[...], preferred_element_type=F32)
    q_ref[...] = (_head_rms_norm(q, qg_ref[...], pool_ref) * (HEAD_DIM ** -0.5)).astype(BF16)


def _oproj_ffn_kernel(x_ref, o_ref, wo_ref, g_ref, w13_ref, w2_ref, y_ref, act_ref):
    x = x_ref[...] + jnp.dot(o_ref[...], wo_ref[...], preferred_element_type=F32)
    y_ref[...] = _half_swiglu_step(x, g_ref, w13_ref, w2_ref, act_ref)


def _cast_job(src, lead, n_steps):
    rows, cols = src.shape[-2:]
    n_chunks = max(k for k in range(1, n_steps + 1)
                   if rows % k == 0 and (rows // k) % BF16_ROWS == 0)
    chunk = rows // n_chunks
    if lead is None:
        in_spec = pl.BlockSpec((chunk, cols), lambda i: (jnp.minimum(i, n_chunks - 1), 0))
    else:
        in_spec = pl.BlockSpec((None, chunk, cols),
                               lambda i: (lead, jnp.minimum(i, n_chunks - 1), 0))
    out_spec = pl.BlockSpec((chunk, cols), lambda i: (jnp.minimum(i, n_chunks - 1), 0))
    return in_spec, out_spec, jax.ShapeDtypeStruct((rows, cols), BF16)


def _ffn_body(*refs, core, n_in, n_out, n_cast):
    ins, rest = refs[:n_in], refs[n_in:]
    cast_in, rest = rest[:n_cast], rest[n_cast:]
    outs, rest = rest[:n_out], rest[n_out:]
    cast_out, scratch = rest[:n_cast], rest[n_cast:]
    for src_ref, dst_ref in zip(cast_in, cast_out):
        dst_ref[...] = src_ref[...].astype(BF16)
    core(*ins, *outs, *scratch)


def _ffn_call(body, name, x2d, gain, w13, w2, casts, extra_in=(), extra_in_specs=(), lead_in=(),
              lead_in_specs=(), extra_out_shapes=(), extra_out_specs=()):
    m, d = x2d.shape
    d_ff = w2.shape[0]
    n_steps = m // FFN_TM
    row_spec = pl.BlockSpec((FFN_TM, d), lambda i: (i, 0))
    jobs = [_cast_job(src, lead, n_steps) for src, lead in casts]
    in_specs = [row_spec, *lead_in_specs, _const_spec((1, d)), _const_spec((d, 2 * d_ff)),
                _const_spec((d_ff, d)), *extra_in_specs]
    n_out = 1 + len(extra_out_shapes)
    out = pl.pallas_call(
        functools.partial(_ffn_body, core=body, n_in=len(in_specs), n_out=n_out,
                          n_cast=len(jobs)),
        grid=(n_steps,),
        in_specs=in_specs + [job[0] for job in jobs],
        out_specs=[row_spec, *extra_out_specs] + [job[1] for job in jobs],
        out_shape=[jax.ShapeDtypeStruct((m, d), F32), *extra_out_shapes] + [job[2] for job in jobs],
        scratch_shapes=[pltpu.VMEM((FFN_TM, d_ff), BF16)],
        compiler_params=pltpu.CompilerParams(
            dimension_semantics=("arbitrary",), vmem_limit_bytes=VMEM_LIMIT),
        name=name,
    )(x2d, *lead_in, gain.reshape(1, d), w13, w2, *extra_in, *(src for src, _ in casts))
    return out[:n_out], out[n_out:]


def _ffn(x2d, gain, w13, w2, casts):
    return _ffn_call(_ffn_kernel, "ffn", x2d, gain, w13, w2, casts)


def _ffn_q(x2d, gain, w13, w2, casts, q_in_gain, w_q, q_gain_row, pool):
    m, d = x2d.shape
    d_attn = w_q.shape[1]
    return _ffn_call(
        _ffn_q_kernel, "ffn_q", x2d, gain, w13, w2, casts,
        extra_in=(q_in_gain.reshape(1, d), w_q, q_gain_row, pool),
        extra_in_specs=(_const_spec((1, d)), _const_spec((d, d_attn)),
                        _const_spec((1, d_attn)), _const_spec((POOL_W, POOL_W))),
        extra_out_shapes=(jax.ShapeDtypeStruct((m, d_attn), BF16),),
        extra_out_specs=(pl.BlockSpec((FFN_TM, d_attn), lambda i: (i, 0)),))


def _oproj_ffn(x2d, gain, w13, w2, casts, o2d, w_o):
    d_attn = o2d.shape[1]
    return _ffn_call(
        _oproj_ffn_kernel, "oproj_ffn", x2d, gain, w13, w2, casts,
        lead_in=(o2d, w_o),
        lead_in_specs=(pl.BlockSpec((FFN_TM, d_attn), lambda i: (i, 0)),
                       _const_spec((d_attn, x2d.shape[1]))))


def _gelu_tanh(x):
    c = np.float32(np.sqrt(2.0 / np.pi))
    return x * (0.5 * (1.0 + jnp.tanh(c * (x + 0.044715 * (x * x * x)))))


def _lru_in(x, g_ref, win_ref):
    h = _rms_norm_rows(x, g_ref[...]).astype(BF16)
    return jnp.dot(h, win_ref[...], preferred_element_type=F32)


def _lru_gates(proj, cw_ref, cb_ref, wr_ref, wi_ref, br_ref, bi_ref, lam_ref,
               rec_ref, a_ref, u_ref, gate_ref):
    ts, w = gate_ref.shape
    bw = wr_ref.shape[1]
    n_blocks = w // bw
    gate_ref[...] = _gelu_tanh(proj[:, :w])
    rec = proj[:, w:]
    rec_ref[SUBLANES:SUBLANES + ts, :] = rec
    xc = cb_ref[...] + cw_ref[CONV_W - 1:CONV_W, :] * rec
    for k in range(CONV_W - 1):
        back = CONV_W - 1 - k
        xc = xc + cw_ref[k:k + 1, :] * rec_ref[SUBLANES - back:SUBLANES - back + ts, :]
    rec_ref[0:SUBLANES, :] = rec_ref[ts:ts + SUBLANES, :]

    xcb = xc.astype(BF16)
    sp_lam = jnp.maximum(-lam_ref[...], 0.0) + jnp.log(1.0 + jnp.exp(-jnp.abs(lam_ref[...])))
    for n in range(n_blocks):
        sl = slice(n * bw, (n + 1) * bw)
        r = _sigmoid(jnp.dot(xcb[:, sl], wr_ref[sl, :], preferred_element_type=F32) + br_ref[:, sl])
        i = _sigmoid(jnp.dot(xcb[:, sl], wi_ref[sl, :], preferred_element_type=F32) + bi_ref[:, sl])
        log_a = (-LRU_C) * r * sp_lam[:, sl]
        a = jnp.exp(log_a)
        a_ref[:, sl] = a
        u_ref[:, sl] = jnp.sqrt(-jnp.tanh(log_a) * (1.0 + a * a)) * (i * xc[:, sl])


def _lru_scan(a_ref, u_ref, hcar_ref):
    ts, w = u_ref.shape
    row = lax.broadcasted_iota(jnp.int32, (SUBLANES, w), 0)
    hprev = hcar_ref[...]
    for gidx in range(ts // SUBLANES):
        rows = slice(gidx * SUBLANES, (gidx + 1) * SUBLANES)
        a = a_ref[rows, :]
        u = u_ref[rows, :]
        for sh in (1, 2, 4):
            keep = row >= sh
            u_new = jnp.where(keep, u + a * pltpu.roll(u, sh, 0), u)
            a = jnp.where(keep, a * pltpu.roll(a, sh, 0), a)
            u = u_new
        hs = u + a * hprev
        u_ref[rows, :] = hs
        hprev = hs[SUBLANES - 1:SUBLANES, :]
    hcar_ref[...] = hprev


def _lru_out(x, u_ref, gate_ref, wout_ref):
    y = (u_ref[...] * gate_ref[...]).astype(BF16)
    return x + jnp.dot(y, wout_ref[...], preferred_element_type=F32)


N_LRU_IN = 10
N_FFN_IN = 3


def _lru_ffn_kernel(*refs, tiles_per_seq, with_kv):
    x_ref, refs = refs[0], refs[1:]
    lru_in, refs = refs[:N_LRU_IN], refs[N_LRU_IN:]
    ffn_in, refs = refs[:N_FFN_IN], refs[N_FFN_IN:]
    if with_kv:
        kv_in, refs = refs[:4], refs[4:]
        (y_ref, k_ref, vt_ref), refs = refs[:3], refs[3:]
    else:
        y_ref, refs = refs[0], refs[1:]
    act_ref, mid_ref, rec_ref, a_ref, u_ref, gate_ref, hcar_ref = refs
    step = pl.program_id(0)

    @pl.when(step == 0)
    def _():
        mid_ref[...] = jnp.zeros_like(mid_ref)

    @pl.when(step % tiles_per_seq == 0)
    def _():
        rec_ref[0:SUBLANES, :] = jnp.zeros((SUBLANES, rec_ref.shape[1]), F32)
        hcar_ref[...] = jnp.zeros_like(hcar_ref)

    mg_ref, win_ref, cw_ref, cb_ref, wr_ref, wi_ref, br_ref, bi_ref, lam_ref, wout_ref = lru_in
    g_ref, w13_ref, w2_ref = ffn_in
    n_chunks = w2_ref.shape[0] // FF_CHUNK
    x = x_ref[...]
    x_prev = mid_ref[...]
    proj = _lru_in(x, mg_ref, win_ref)
    xn = _rms_norm_rows(x_prev, g_ref[...]).astype(BF16)
    _swiglu_up(xn, w13_ref, act_ref, range(0, n_chunks // 2))
    _lru_gates(proj, cw_ref, cb_ref, wr_ref, wi_ref, br_ref, bi_ref, lam_ref,
               rec_ref, a_ref, u_ref, gate_ref)
    _swiglu_up(xn, w13_ref, act_ref, range(n_chunks // 2, n_chunks))
    y = x_prev + 0.5 * jnp.dot(act_ref[...], w2_ref[...], preferred_element_type=F32)
    y_ref[...] = y
    _lru_scan(a_ref, u_ref, hcar_ref)
    if with_kv:
        _kv_tail(y, *kv_in, k_ref, vt_ref)
    mid_ref[...] = _lru_out(x, u_ref, gate_ref, wout_ref)


def _lru_ffn(x2d, seq_len, lru, ffn, casts, kv=None):
    m, d = x2d.shape
    gain, w_in, conv_w, conv_b, w_r, w_i, b_r, b_i, lam, w_out = lru
    f_gain, w13, w2 = ffn
    w = w_out.shape[0]
    d_ff = w2.shape[0]
    n_tiles = m // LRU_TS
    tiles_per_seq = seq_len // LRU_TS
    last = n_tiles - 1
    prev = lambda i: jnp.maximum(i - 1, 0)
    jobs = [_cast_job(src, lead, n_tiles + 1) for src, lead in casts]
    row_vec = lambda v: v.reshape(1, -1)
    operands = [x2d, row_vec(gain), w_in, conv_w, row_vec(conv_b), w_r, w_i, row_vec(b_r),
                row_vec(b_i), row_vec(lam), w_out, row_vec(f_gain), w13, w2]
    in_specs = [pl.BlockSpec((LRU_TS, d), lambda i: (jnp.minimum(i, last), 0))]
    in_specs += [_const_spec(op.shape) for op in operands[1:]]
    out_specs = [pl.BlockSpec((LRU_TS, d), lambda i: (prev(i), 0))]
    out_shape = [jax.ShapeDtypeStruct((m, d), F32)]
    if kv is not None:
        kv_gain, w_kv, k_gain_row, pool = kv
        d_attn = w_kv.shape[1] // 2
        n_pairs = d_attn // LANES
        kv_ops = [row_vec(kv_gain), w_kv, k_gain_row, pool]
        operands += kv_ops
        in_specs += [_const_spec(op.shape) for op in kv_ops]
        out_specs += [
            pl.BlockSpec((LRU_TS, d_attn), lambda i: (prev(i), 0)),
            pl.BlockSpec((None, n_pairs, LRU_TS // ATT_T, LANES, ATT_T),
                         lambda i: (prev(i) // tiles_per_seq, 0, prev(i) % tiles_per_seq, 0, 0))]
        out_shape += [
            jax.ShapeDtypeStruct((m, d_attn), BF16),
            jax.ShapeDtypeStruct((m // seq_len, n_pairs, seq_len // ATT_T, LANES, ATT_T), BF16)]
    n_out = len(out_shape)
    out = pl.pallas_call(
        functools.partial(
            _ffn_body, n_in=len(in_specs), n_out=n_out, n_cast=len(jobs),
            core=functools.partial(_lru_ffn_kernel, tiles_per_seq=tiles_per_seq,
                                   with_kv=kv is not None)),
        grid=(n_tiles + 1,),
        in_specs=in_specs + [job[0] for job in jobs],
        out_specs=out_specs + [job[1] for job in jobs],
        out_shape=out_shape + [job[2] for job in jobs],
        scratch_shapes=[
            pltpu.VMEM((LRU_TS, d_ff), BF16),
            pltpu.VMEM((LRU_TS, d), F32),
            pltpu.VMEM((LRU_TS + SUBLANES, w), F32),
            pltpu.VMEM((LRU_TS, w), F32),
            pltpu.VMEM((LRU_TS, w), F32),
            pltpu.VMEM((LRU_TS, w), F32),
            pltpu.VMEM((1, w), F32),
        ],
        compiler_params=pltpu.CompilerParams(
            dimension_semantics=("arbitrary",), vmem_limit_bytes=VMEM_LIMIT),
        name="lru_ffn",
    )(*operands, *(src for src, _ in casts))
    return out[:n_out], out[n_out:]


def _head_rms_norm(t, gain_row, pool_ref):
    outs = []
    for g in range(t.shape[1] // POOL_W):
        tg = t[:, g * POOL_W:(g + 1) * POOL_W]
        ms = jnp.dot((tg * tg).astype(BF16), pool_ref[...], preferred_element_type=F32)
        outs.append(tg * lax.rsqrt(ms + EPS))
    return jnp.concatenate(outs, axis=1) * gain_row


def _head_pool_matrix():
    idx = np.arange(POOL_W) // HEAD_DIM
    return jnp.asarray((idx[:, None] == idx[None, :]).astype(np.float32) / HEAD_DIM, BF16)


def _attn_kernel(q_ref, k_ref, vt_ref, tri_ref, o_ref, acc_ref, carry_ref):
    t = ATT_T
    qi = pl.program_id(2)
    lane = lax.broadcasted_iota(jnp.int32, (t, LANES), 1)
    k_pos = lax.broadcasted_iota(jnp.int32, (t, t), 0)
    q_pos = lax.broadcasted_iota(jnp.int32, (t, t), 1)
    causal = k_pos < q_pos
    n_heads = 2 * ATT_PAIRS
    q_heads = []
    for pair in range(ATT_PAIRS):
        q = q_ref[:, pair * LANES:(pair + 1) * LANES]
        q_heads += [jnp.where(lane < HEAD_DIM, q, jnp.zeros_like(q)),
                    jnp.where(lane >= HEAD_DIM, q, jnp.zeros_like(q))]

    h = t // 2

    def blocks(specs):
        zs = []
        for head, kj, _, _ in specs:
            pair = head // 2
            kb = k_ref[pl.ds(pl.multiple_of(kj * t, t), t), pair * LANES:(pair + 1) * LANES]
            zs.append(lax.dot_general(kb, q_heads[head], (((1,), (1,)), ((), ())),
                                      preferred_element_type=F32))
        def halves(z, masked):
            return [(z[:h], causal[:h], 0), (z[h:, h:], causal[h:, h:], h)] if masked else \
                   [(z[:h], None, 0), (z[h:], None, 0)]

        def widen(x, q0):
            return x if q0 == 0 else jnp.concatenate([jnp.zeros((h, q0), x.dtype), x], axis=1)

        exts = []
        for z, (_, _, masked, _) in zip(zs, specs):
            ext = []
            for zp, mask, q0 in halves(z, masked):
                sp = jnp.maximum(zp, 0.0) + jnp.log(1.0 + jnp.exp(-jnp.abs(zp)))
                sp16 = (sp if mask is None else jnp.where(mask, sp, 0.0)).astype(BF16)
                ext.append(jnp.dot(tri_ref[...], widen(sp16, q0), preferred_element_type=F32))
            exts.append(ext)
        ws, carries = [], []
        for z, (ext_early, ext_late), (_, _, masked, carry) in zip(zs, exts, specs):
            if isinstance(carry, int):
                carry = carries[carry]
            carry_early = ext_late[h:] if carry is None else carry + ext_late[h:]
            w_halves = []
            for (zp, mask, q0), ext, c in zip(halves(z, masked), (ext_early, ext_late),
                                              (carry_early, carry)):
                log_w = zp + ext[:h, q0:]
                if c is not None:
                    log_w = log_w + pltpu.repeat(c[:, q0:], h // SUBLANES, axis=0)
                w = jnp.exp(log_w)
                if mask is not None:
                    w = jnp.where(mask, w, 0.0)
                w_halves.append(widen(w.astype(BF16), q0))
            ws.append(jnp.concatenate(w_halves, axis=0))
            carries.append(carry_early + ext_early[h:])
        pvs = [jnp.dot(vt_ref[head // 2, kj], w, preferred_element_type=F32)
               for w, (head, kj, _, _) in zip(ws, specs)]
        return list(zip(pvs, carries))

    @pl.when(qi == 0)
    def _():
        out = blocks([(head, qi, True, None) for head in range(n_heads)])
        for head, (pv, carry) in enumerate(out):
            acc_ref[head] = pv
            carry_ref[head] = carry

    @pl.when(qi > 0)
    def _():
        out = blocks([(head, qi, True, None) for head in range(n_heads)]
                     + [(head, qi - 1, False, head) for head in range(n_heads)])
        for head in range(n_heads):
            acc_ref[head] = out[head][0] + out[n_heads + head][0]
            carry_ref[head] = out[n_heads + head][1]

    def max_carry(carries):
        top = carries[0]
        for carry in carries[1:]:
            top = jnp.maximum(top, carry)
        return jnp.max(top)

    def cond(state):
        kj, top = state
        return jnp.logical_and(kj >= 0, top > LOG_W_ZERO)

    def body(state):
        kj, _ = state
        out = blocks([(head, kj, False, carry_ref[head]) for head in range(n_heads)])
        for head, (pv, carry) in enumerate(out):
            acc_ref[head] += pv
            carry_ref[head] = carry
        return kj - 1, max_carry([carry for _, carry in out])

    top = max_carry([carry_ref[head] for head in range(n_heads)])
    lax.while_loop(cond, body, (qi - 2, top))

    feat = lax.broadcasted_iota(jnp.int32, (LANES, t), 0)
    for pair in range(ATT_PAIRS):
        o_t = jnp.where(feat < HEAD_DIM, acc_ref[2 * pair], acc_ref[2 * pair + 1])
        o_ref[:, pair * LANES:(pair + 1) * LANES] = o_t.T.astype(o_ref.dtype)


def _suffix_sum_matrix():
    j = np.arange(ATT_T // 2)
    later = (j[None, :] >= j[:, None]).astype(np.float32)
    one_part = np.concatenate([later, np.ones((SUBLANES, ATT_T // 2), np.float32)], axis=0)
    return jnp.asarray(-one_part, BF16)


def _attention(q, k, vt, tri):
    b, s, d_attn = q.shape
    width = ATT_PAIRS * LANES
    k_spec = pl.BlockSpec((None, s, width), lambda bi, pi, qi: (bi, 0, pi))
    vt_spec = pl.BlockSpec((None, ATT_PAIRS, s // ATT_T, LANES, ATT_T),
                           lambda bi, pi, qi: (bi, pi, 0, 0, 0))
    q_spec = pl.BlockSpec((None, ATT_T, width), lambda bi, pi, qi: (bi, qi, pi))
    return pl.pallas_call(
        _attn_kernel,
        grid=(b, d_attn // width, s // ATT_T),
        in_specs=[q_spec, k_spec, vt_spec, _const_spec(tri.shape)],
        out_specs=q_spec,
        out_shape=jax.ShapeDtypeStruct((b, s, d_attn), BF16),
        scratch_shapes=[
            pltpu.VMEM((2 * ATT_PAIRS, LANES, ATT_T), F32),
            pltpu.VMEM((2 * ATT_PAIRS, SUBLANES, ATT_T), F32),
        ],
        compiler_params=pltpu.CompilerParams(
            dimension_semantics=("parallel", "parallel", "arbitrary"),
            vmem_limit_bytes=VMEM_LIMIT),
        name="attn",
    )(q, k, vt, tri)


def kernel(x, ffn1_norm, ffn1_w13, ffn1_w2, mix_norm, a_w_in, a_conv_w, a_conv_b, a_w_r, a_b_r,
           a_w_i, a_b_i, a_lambda, a_w_out, kv_norm, w_kv, k_norm, b_w_q, q_norm, b_w_o,
           ffn2_norm, ffn2_w13, ffn2_w2):
    b, s, d = x.shape
    depth = ffn1_norm.shape[0]
    n_a = a_w_in.shape[0]
    m = b * s
    pool = _head_pool_matrix()
    tri = _suffix_sum_matrix()

    assert 1 <= n_a < depth and s % FFN_TM == 0 and FFN_TM % ATT_T == 0

    def ffn_sources(l, which):
        w13, w2 = (ffn1_w13, ffn1_w2) if which == 1 else (ffn2_w13, ffn2_w2)
        srcs = {"w13": (w13, l), "w2": (w2, l)}
        if which == 2 and l == n_a - 1:
            srcs["w_kv"] = (w_kv, None)
        if l >= n_a:
            srcs["w_q" if which == 1 else "w_o"] = ((b_w_q if which == 1 else b_w_o), l - n_a)
        return srcs

    def mixer_sources(l):
        if l >= n_a:
            return {}
        lru_w = a_w_out.shape[1]
        return {"w_in": (a_w_in, l), "w_r": (a_w_r.reshape(n_a, lru_w, -1), l),
                "w_i": (a_w_i.reshape(n_a, lru_w, -1), l), "w_out": (a_w_out, l)}

    order = [(l, which) for l in range(depth) for which in (1, 2)]
    ready = {(0, 1, name): (src[lead] if lead is not None else src).astype(BF16)
             for name, (src, lead) in ffn_sources(0, 1).items()}

    x = x.reshape(m, d)
    k_sh = vt_sh = q = o = None
    for idx, (l, which) in enumerate(order):
        jobs = {}
        if which == 1:
            jobs.update({("mix", l, name): v for name, v in mixer_sources(l).items()})
        if idx + 1 < len(order):
            nxt = order[idx + 1]
            jobs.update({(*nxt, name): v for name, v in ffn_sources(*nxt).items()})
        own = {name: ready.pop((l, which, name)) for name in ffn_sources(l, which)}
        gain = (ffn1_norm if which == 1 else ffn2_norm)[l]
        args = (x, gain, own["w13"], own["w2"], list(jobs.values()))
        if which == 1 and l >= n_a:
            (x, q), cast = _ffn_q(*args, mix_norm[l], own["w_q"],
                                  jnp.tile(q_norm[l - n_a], N_HEADS).reshape(1, -1), pool)
        elif which == 2 and l >= n_a:
            (x,), cast = _oproj_ffn(*args, o.reshape(m, -1), own["w_o"])
        elif which == 2:
            mix = {name: ready.pop(("mix", l, name)) for name in mixer_sources(l)}
            lru = (mix_norm[l], mix["w_in"], a_conv_w[l], a_conv_b[l], mix["w_r"], mix["w_i"],
                   a_b_r[l], a_b_i[l], a_lambda[l], mix["w_out"])
            kv = None
            if l == n_a - 1:
                kv = (kv_norm, own["w_kv"], jnp.tile(k_norm, N_HEADS).reshape(1, -1), pool)
            outs, cast = _lru_ffn(x, s, lru, (gain, own["w13"], own["w2"]), list(jobs.values()), kv)
            x = outs[0]
            if kv is not None:
                k_sh, vt_sh = outs[1].reshape(b, s, -1), outs[2]
        else:
            (x,), cast = _ffn(*args)
        ready.update(zip(jobs.keys(), cast))
        if which == 1 and l >= n_a:
            o = _attention(q.reshape(b, s, -1), k_sh, vt_sh, tri)
    return x.reshape(b, s, d)
```

```python
import functools

import jax
import jax.numpy as jnp
import numpy as np
from jax import lax
from jax.experimental import pallas as pl
from jax.experimental.pallas import tpu as pltpu

EPS = 1e-6
N_HEADS = 16
HEAD_DIM = 64
CONV_W = 4
LRU_C = 8.0
LANES = 128
SUBLANES = 8
BF16_ROWS = 16
VMEM_LIMIT = 56 * 1024 * 1024

FFN_TM = 512
FF_CHUNK = 256
LRU_TS = 256
ATT_T = 256
ATT_GROUP = 4
POOL_W = 256
LOG_W_ZERO = -104.0
LOG_W_NONE = -1e30

F32 = jnp.float32
BF16 = jnp.bfloat16


def _rms_norm_rows(x, gain_row):
    ms = jnp.mean(x * x, axis=-1, keepdims=True)
    return x * lax.rsqrt(ms + EPS) * gain_row


def _sigmoid(x):
    return 1.0 / (1.0 + jnp.exp(-x))


def _split_bf16(x):
    hi = x.astype(BF16)
    lo = (x - hi.astype(F32)).astype(BF16)
    return hi, lo


def _const_spec(shape):
    nd = len(shape)
    return pl.BlockSpec(shape, lambda *_: (0,) * nd, pipeline_mode=pl.Buffered(1))


def _swiglu_up(xn, w13_ref, act_ref, chunks):
    d_ff = act_ref.shape[1]
    for c in chunks:
        lo = c * FF_CHUNK
        gate = jnp.dot(xn, w13_ref[:, lo:lo + FF_CHUNK], preferred_element_type=F32)
        up = jnp.dot(xn, w13_ref[:, d_ff + lo:d_ff + lo + FF_CHUNK], preferred_element_type=F32)
        act_ref[:, lo:lo + FF_CHUNK] = (gate * _sigmoid(gate) * up).astype(BF16)


def _half_swiglu_step(x, g_ref, w13_ref, w2_ref, act_ref):
    xn = _rms_norm_rows(x, g_ref[...]).astype(BF16)
    _swiglu_up(xn, w13_ref, act_ref, range(w2_ref.shape[0] // FF_CHUNK))
    return x + 0.5 * jnp.dot(act_ref[...], w2_ref[...], preferred_element_type=F32)


def _ffn_kernel(x_ref, g_ref, w13_ref, w2_ref, y_ref, act_ref):
    y_ref[...] = _half_swiglu_step(x_ref[...], g_ref, w13_ref, w2_ref, act_ref)


def _kv_tail(y, ng_ref, wkv_ref, kg_ref, pool_ref, k_ref, vt_ref):
    d_attn = k_ref.shape[1]
    h = _rms_norm_rows(y, ng_ref[...]).astype(BF16)
    kv = jnp.dot(h, wkv_ref[...], preferred_element_type=F32)
    k_ref[...] = _head_rms_norm(kv[:, :d_attn], kg_ref[...], pool_ref).astype(BF16)
    for pair in range(vt_ref.shape[0]):
        for blk in range(vt_ref.shape[1]):
            v_blk = kv[blk * ATT_T:(blk + 1) * ATT_T,
                       d_attn + pair * LANES:d_attn + (pair + 1) * LANES]
            vt_ref[pair, blk] = v_blk.T.astype(BF16)


def _ffn_q_kernel(x_ref, g_ref, w13_ref, w2_ref, ng_ref, wq_ref, qg_ref, pool_ref,
                  y_ref, q_ref, act_ref):
    y = _half_swiglu_step(x_ref[...], g_ref, w13_ref, w2_ref, act_ref)
    y_ref[...] = y
    h = _rms_norm_rows(y, ng_ref[...]).astype(BF16)
    q = jnp.dot(h, wq_ref[...], preferred_element_type=F32)
    q_ref[...] = (_head_rms_norm(q, qg_ref[...], pool_ref) * (HEAD_DIM ** -0.5)).astype(BF16)


def _cast_job(src, lead, n_steps):
    rows, cols = src.shape[-2:]
    n_chunks = max(k for k in range(1, n_steps + 1)
                   if rows % k == 0 and (rows // k) % BF16_ROWS == 0)
    chunk = rows // n_chunks
    if lead is None:
        in_spec = pl.BlockSpec((chunk, cols), lambda i: (jnp.minimum(i, n_chunks - 1), 0))
    else:
        in_spec = pl.BlockSpec((None, chunk, cols),
                               lambda i: (lead, jnp.minimum(i, n_chunks - 1), 0))
    out_spec = pl.BlockSpec((chunk, cols), lambda i: (jnp.minimum(i, n_chunks - 1), 0))
    return in_spec, out_spec, jax.ShapeDtypeStruct((rows, cols), BF16)


def _ffn_body(*refs, core, n_in, n_out, n_cast):
    ins, rest = refs[:n_in], refs[n_in:]
    cast_in, rest = rest[:n_cast], rest[n_cast:]
    outs, rest = rest[:n_out], rest[n_out:]
    cast_out, scratch = rest[:n_cast], rest[n_cast:]
    for src_ref, dst_ref in zip(cast_in, cast_out):
        dst_ref[...] = src_ref[...].astype(BF16)
    core(*ins, *outs, *scratch)


def _ffn_call(body, name, x2d, gain, w13, w2, casts, extra_in=(), extra_in_specs=(),
              extra_out_shapes=(), extra_out_specs=()):
    m, d = x2d.shape
    d_ff = w2.shape[0]
    n_steps = m // FFN_TM
    row_spec = pl.BlockSpec((FFN_TM, d), lambda i: (i, 0))
    jobs = [_cast_job(src, lead, n_steps) for src, lead in casts]
    in_specs = [row_spec, _const_spec((1, d)), _const_spec((d, 2 * d_ff)),
                _const_spec((d_ff, d)), *extra_in_specs]
    n_out = 1 + len(extra_out_shapes)
    out = pl.pallas_call(
        functools.partial(_ffn_body, core=body, n_in=len(in_specs), n_out=n_out,
                          n_cast=len(jobs)),
        grid=(n_steps,),
        in_specs=in_specs + [job[0] for job in jobs],
        out_specs=[row_spec, *extra_out_specs] + [job[1] for job in jobs],
        out_shape=[jax.ShapeDtypeStruct((m, d), F32), *extra_out_shapes] + [job[2] for job in jobs],
        scratch_shapes=[pltpu.VMEM((FFN_TM, d_ff), BF16)],
        compiler_params=pltpu.CompilerParams(
            dimension_semantics=("arbitrary",), vmem_limit_bytes=VMEM_LIMIT),
        name=name,
    )(x2d, gain.reshape(1, d), w13, w2, *extra_in, *(src for src, _ in casts))
    return out[:n_out], out[n_out:]


def _ffn(x2d, gain, w13, w2, casts):
    return _ffn_call(_ffn_kernel, "ffn", x2d, gain, w13, w2, casts)


def _ffn_q(x2d, gain, w13, w2, casts, q_in_gain, w_q, q_gain_row, pool):
    m, d = x2d.shape
    d_attn = w_q.shape[1]
    return _ffn_call(
        _ffn_q_kernel, "ffn_q", x2d, gain, w13, w2, casts,
        extra_in=(q_in_gain.reshape(1, d), w_q, q_gain_row, pool),
        extra_in_specs=(_const_spec((1, d)), _const_spec((d, d_attn)),
                        _const_spec((1, d_attn)), _const_spec((POOL_W, POOL_W))),
        extra_out_shapes=(jax.ShapeDtypeStruct((m, d_attn), BF16),),
        extra_out_specs=(pl.BlockSpec((FFN_TM, d_attn), lambda i: (i, 0)),))


def _gelu_tanh(x):
    c = np.float32(np.sqrt(2.0 / np.pi))
    return x * (0.5 * (1.0 + jnp.tanh(c * (x + 0.044715 * (x * x * x)))))


def _lru_in(x, g_ref, win_ref):
    h = _rms_norm_rows(x, g_ref[...]).astype(BF16)
    return jnp.dot(h, win_ref[...], preferred_element_type=F32)


def _lru_gates(proj, cw_ref, cb_ref, wr_ref, wi_ref, br_ref, bi_ref, lam_ref,
               rec_ref, a_ref, u_ref, gate_ref):
    ts, w = gate_ref.shape
    bw = wr_ref.shape[1]
    n_blocks = w // bw
    gate_ref[...] = _gelu_tanh(proj[:, :w])
    rec = proj[:, w:]
    rec_ref[SUBLANES:SUBLANES + ts, :] = rec
    xc = cb_ref[...] + cw_ref[CONV_W - 1:CONV_W, :] * rec
    for k in range(CONV_W - 1):
        back = CONV_W - 1 - k
        xc = xc + cw_ref[k:k + 1, :] * rec_ref[SUBLANES - back:SUBLANES - back + ts, :]
    rec_ref[0:SUBLANES, :] = rec_ref[ts:ts + SUBLANES, :]

    xcb = xc.astype(BF16)
    sp_lam = jnp.maximum(-lam_ref[...], 0.0) + jnp.log(1.0 + jnp.exp(-jnp.abs(lam_ref[...])))
    for n in range(n_blocks):
        sl = slice(n * bw, (n + 1) * bw)
        r = _sigmoid(jnp.dot(xcb[:, sl], wr_ref[sl, :], preferred_element_type=F32) + br_ref[:, sl])
        i = _sigmoid(jnp.dot(xcb[:, sl], wi_ref[sl, :], preferred_element_type=F32) + bi_ref[:, sl])
        log_a = (-LRU_C) * r * sp_lam[:, sl]
        a = jnp.exp(log_a)
        a_ref[:, sl] = a
        u_ref[:, sl] = jnp.sqrt(-jnp.tanh(log_a) * (1.0 + a * a)) * (i * xc[:, sl])


def _lru_scan(a_ref, u_ref, hcar_ref):
    ts, w = u_ref.shape
    row = lax.broadcasted_iota(jnp.int32, (SUBLANES, w), 0)
    hprev = hcar_ref[...]
    for gidx in range(ts // SUBLANES):
        rows = slice(gidx * SUBLANES, (gidx + 1) * SUBLANES)
        a = a_ref[rows, :]
        u = u_ref[rows, :]
        for sh in (1, 2, 4):
            keep = row >= sh
            u_new = jnp.where(keep, u + a * pltpu.roll(u, sh, 0), u)
            a = jnp.where(keep, a * pltpu.roll(a, sh, 0), a)
            u = u_new
        hs = u + a * hprev
        u_ref[rows, :] = hs
        hprev = hs[SUBLANES - 1:SUBLANES, :]
    hcar_ref[...] = hprev


def _lru_out(x, u_ref, gate_ref, wout_ref):
    y = (u_ref[...] * gate_ref[...]).astype(BF16)
    return x + jnp.dot(y, wout_ref[...], preferred_element_type=F32)


N_LRU_IN = 10
N_FFN_IN = 3


def _lru_ffn_kernel(*refs, tiles_per_seq, with_kv):
    x_ref, refs = refs[0], refs[1:]
    lru_in, refs = refs[:N_LRU_IN], refs[N_LRU_IN:]
    ffn_in, refs = refs[:N_FFN_IN], refs[N_FFN_IN:]
    if with_kv:
        kv_in, refs = refs[:4], refs[4:]
        (y_ref, k_ref, vt_ref), refs = refs[:3], refs[3:]
    else:
        y_ref, refs = refs[0], refs[1:]
    act_ref, mid_ref, rec_ref, a_ref, u_ref, gate_ref, hcar_ref = refs
    step = pl.program_id(0)

    @pl.when(step == 0)
    def _():
        mid_ref[...] = jnp.zeros_like(mid_ref)

    @pl.when(step % tiles_per_seq == 0)
    def _():
        rec_ref[0:SUBLANES, :] = jnp.zeros((SUBLANES, rec_ref.shape[1]), F32)
        hcar_ref[...] = jnp.zeros_like(hcar_ref)

    mg_ref, win_ref, cw_ref, cb_ref, wr_ref, wi_ref, br_ref, bi_ref, lam_ref, wout_ref = lru_in
    g_ref, w13_ref, w2_ref = ffn_in
    n_chunks = w2_ref.shape[0] // FF_CHUNK
    x = x_ref[...]
    x_prev = mid_ref[...]
    proj = _lru_in(x, mg_ref, win_ref)
    xn = _rms_norm_rows(x_prev, g_ref[...]).astype(BF16)
    _swiglu_up(xn, w13_ref, act_ref, range(0, n_chunks // 2))
    _lru_gates(proj, cw_ref, cb_ref, wr_ref, wi_ref, br_ref, bi_ref, lam_ref,
               rec_ref, a_ref, u_ref, gate_ref)
    _swiglu_up(xn, w13_ref, act_ref, range(n_chunks // 2, n_chunks))
    y = x_prev + 0.5 * jnp.dot(act_ref[...], w2_ref[...], preferred_element_type=F32)
    y_ref[...] = y
    _lru_scan(a_ref, u_ref, hcar_ref)
    if with_kv:
        _kv_tail(y, *kv_in, k_ref, vt_ref)
    mid_ref[...] = _lru_out(x, u_ref, gate_ref, wout_ref)


def _lru_ffn(x2d, seq_len, lru, ffn, casts, kv=None):
    m, d = x2d.shape
    gain, w_in, conv_w, conv_b, w_r, w_i, b_r, b_i, lam, w_out = lru
    f_gain, w13, w2 = ffn
    w = w_out.shape[0]
    d_ff = w2.shape[0]
    n_tiles = m // LRU_TS
    tiles_per_seq = seq_len // LRU_TS
    last = n_tiles - 1
    prev = lambda i: jnp.maximum(i - 1, 0)
    jobs = [_cast_job(src, lead, n_tiles + 1) for src, lead in casts]
    row_vec = lambda v: v.reshape(1, -1)
    operands = [x2d, row_vec(gain), w_in, conv_w, row_vec(conv_b), w_r, w_i, row_vec(b_r),
                row_vec(b_i), row_vec(lam), w_out, row_vec(f_gain), w13, w2]
    in_specs = [pl.BlockSpec((LRU_TS, d), lambda i: (jnp.minimum(i, last), 0))]
    in_specs += [_const_spec(op.shape) for op in operands[1:]]
    out_specs = [pl.BlockSpec((LRU_TS, d), lambda i: (prev(i), 0))]
    out_shape = [jax.ShapeDtypeStruct((m, d), F32)]
    if kv is not None:
        kv_gain, w_kv, k_gain_row, pool = kv
        d_attn = w_kv.shape[1] // 2
        n_pairs = d_attn // LANES
        kv_ops = [row_vec(kv_gain), w_kv, k_gain_row, pool]
        operands += kv_ops
        in_specs += [_const_spec(op.shape) for op in kv_ops]
        out_specs += [
            pl.BlockSpec((LRU_TS, d_attn), lambda i: (prev(i), 0)),
            pl.BlockSpec((None, n_pairs, LRU_TS // ATT_T, LANES, ATT_T),
                         lambda i: (prev(i) // tiles_per_seq, 0, prev(i) % tiles_per_seq, 0, 0))]
        out_shape += [
            jax.ShapeDtypeStruct((m, d_attn), BF16),
            jax.ShapeDtypeStruct((m // seq_len, n_pairs, seq_len // ATT_T, LANES, ATT_T), BF16)]
    n_out = len(out_shape)
    out = pl.pallas_call(
        functools.partial(
            _ffn_body, n_in=len(in_specs), n_out=n_out, n_cast=len(jobs),
            core=functools.partial(_lru_ffn_kernel, tiles_per_seq=tiles_per_seq,
                                   with_kv=kv is not None)),
        grid=(n_tiles + 1,),
        in_specs=in_specs + [job[0] for job in jobs],
        out_specs=out_specs + [job[1] for job in jobs],
        out_shape=out_shape + [job[2] for job in jobs],
        scratch_shapes=[
            pltpu.VMEM((LRU_TS, d_ff), BF16),
            pltpu.VMEM((LRU_TS, d), F32),
            pltpu.VMEM((LRU_TS + SUBLANES, w), F32),
            pltpu.VMEM((LRU_TS, w), F32),
            pltpu.VMEM((LRU_TS, w), F32),
            pltpu.VMEM((LRU_TS, w), F32),
            pltpu.VMEM((1, w), F32),
        ],
        compiler_params=pltpu.CompilerParams(
            dimension_semantics=("arbitrary",), vmem_limit_bytes=VMEM_LIMIT),
        name="lru_ffn",
    )(*operands, *(src for src, _ in casts))
    return out[:n_out], out[n_out:]


def _head_rms_norm(t, gain_row, pool_ref):
    outs = []
    for g in range(t.shape[1] // POOL_W):
        tg = t[:, g * POOL_W:(g + 1) * POOL_W]
        ms = jnp.dot((tg * tg).astype(BF16), pool_ref[...], preferred_element_type=F32)
        outs.append(tg * lax.rsqrt(ms + EPS))
    return jnp.concatenate(outs, axis=1) * gain_row


def _head_pool_matrix():
    idx = np.arange(POOL_W) // HEAD_DIM
    return jnp.asarray((idx[:, None] == idx[None, :]).astype(np.float32) / HEAD_DIM, BF16)


def _attn_causal():
    k_pos = lax.broadcasted_iota(jnp.int32, (ATT_T, ATT_T), 0)
    q_pos = lax.broadcasted_iota(jnp.int32, (ATT_T, ATT_T), 1)
    return k_pos < q_pos


def _attn_halves(z, masked, causal):
    h = ATT_T // 2
    if masked:
        return [(z[:h], causal[:h], 0), (z[h:, h:], causal[h:, h:], h)]
    return [(z[:h], None, 0), (z[h:], None, 0)]


def _attn_widen(x, q0):
    return x if q0 == 0 else jnp.concatenate([jnp.zeros((x.shape[0], q0), x.dtype), x], axis=1)


def _attn_scores(specs, q_heads, k_ref):
    zs = []
    for head, kj, _, _ in specs:
        pair = head // 2
        kb = k_ref[pl.ds(pl.multiple_of(kj * ATT_T, ATT_T), ATT_T), pair * LANES:(pair + 1) * LANES]
        zs.append(lax.dot_general(kb, q_heads[head], (((1,), (1,)), ((), ())),
                                  preferred_element_type=F32))
    return zs


def _attn_sums(zs, specs, tri_ref, causal):
    exts = []
    for z, (_, _, masked, _) in zip(zs, specs):
        ext = []
        for zp, mask, q0 in _attn_halves(z, masked, causal):
            sp = jnp.maximum(zp, 0.0) + jnp.log(1.0 + jnp.exp(-jnp.abs(zp)))
            sp16 = (sp if mask is None else jnp.where(mask, sp, 0.0)).astype(BF16)
            ext.append(jnp.dot(tri_ref[...], _attn_widen(sp16, q0), preferred_element_type=F32))
        exts.append(ext)
    return exts


def _attn_weights(zs, exts, specs, causal, chain_valid=None):
    h = ATT_T // 2
    ws, carries = [], []
    for z, (ext_early, ext_late), (_, _, masked, carry) in zip(zs, exts, specs):
        if isinstance(carry, int):
            carry = carries[carry]
            if chain_valid is not None:
                carry = jnp.where(chain_valid, carry, LOG_W_NONE)
        total_late = jnp.broadcast_to(ext_late[0:1], (SUBLANES, ATT_T))
        carry_early = total_late if carry is None else carry + total_late
        w_halves = []
        for (zp, mask, q0), ext, c in zip(_attn_halves(z, masked, causal), (ext_early, ext_late),
                                          (carry_early, carry)):
            log_w = zp + ext[:, q0:]
            if c is not None:
                log_w = log_w + c[0:1, q0:]
            w = jnp.exp(log_w)
            if mask is not None:
                w = jnp.where(mask, w, 0.0)
            w_halves.append(_attn_widen(w.astype(BF16), q0))
        ws.append(jnp.concatenate(w_halves, axis=0))
        carries.append(carry_early + jnp.broadcast_to(ext_early[0:1], (SUBLANES, ATT_T)))
    return ws, carries


def _attn_values(ws, specs, vt_ref):
    return [jnp.dot(vt_ref[head // 2, kj], w, preferred_element_type=F32)
            for w, (head, kj, _, _) in zip(ws, specs)]


def _attn_ffn_kernel(x_ref, q_ref, k_ref, vt_ref, tri_ref, wo_ref, g_ref, w13_ref, w2_ref, y_ref,
                     act_ref, acc_ref, carry_ref, *, tiles_per_seq, n_tiles):
    t = ATT_T
    n_heads = acc_ref.shape[0]
    n_chunks = w2_ref.shape[0] // FF_CHUNK
    step = pl.program_id(0)
    qi = jnp.minimum(step, n_tiles - 1) % tiles_per_seq

    @pl.when(step == 0)
    def _():
        acc_ref[...] = jnp.zeros_like(acc_ref)

    feat = lax.broadcasted_iota(jnp.int32, (LANES, t), 0)
    o_prev = jnp.concatenate(
        [jnp.where(feat < HEAD_DIM, acc_ref[2 * pair], acc_ref[2 * pair + 1]).T.astype(BF16)
         for pair in range(n_heads // 2)], axis=1)

    lane = lax.broadcasted_iota(jnp.int32, (t, LANES), 1)
    causal = _attn_causal()
    q_heads = []
    for pair in range(n_heads // 2):
        q = q_ref[:, pair * LANES:(pair + 1) * LANES]
        q_heads += [jnp.where(lane < HEAD_DIM, q, jnp.zeros_like(q)),
                    jnp.where(lane >= HEAD_DIM, q, jnp.zeros_like(q))]

    has_prev = qi > 0
    kj_prev = jnp.maximum(qi - 1, 0)
    groups = [range(g, g + ATT_GROUP) for g in range(0, n_heads, ATT_GROUP)]
    n_up = 2 * len(groups) - 1
    up_slices = [range(n_chunks * s // n_up, n_chunks * (s + 1) // n_up) for s in range(n_up)]
    ffn_stage = iter(up_slices)
    x_in = xn = None
    results = {}
    for gidx, heads in enumerate(groups):
        specs = ([(head, qi, True, None) for head in heads]
                 + [(head, kj_prev, False, idx) for idx, head in enumerate(heads)])
        zs = _attn_scores(specs, q_heads, k_ref)
        if gidx == 0:
            x_in = x_ref[...] + jnp.dot(o_prev, wo_ref[...], preferred_element_type=F32)
            xn = _rms_norm_rows(x_in, g_ref[...]).astype(BF16)
        else:
            _swiglu_up(xn, w13_ref, act_ref, next(ffn_stage))
        exts = _attn_sums(zs, specs, tri_ref, causal)
        _swiglu_up(xn, w13_ref, act_ref, next(ffn_stage))
        ws, carries = _attn_weights(zs, exts, specs, causal, chain_valid=has_prev)
        pvs = _attn_values(ws, specs, vt_ref)
        for idx, head in enumerate(heads):
            results[head] = (pvs[idx] + pvs[len(heads) + idx], carries[len(heads) + idx])
    y_ref[...] = x_in + 0.5 * jnp.dot(act_ref[...], w2_ref[...], preferred_element_type=F32)
    for head, (pv, carry) in results.items():
        acc_ref[head] = pv
        carry_ref[head] = carry

    def max_carry(carries):
        top = carries[0]
        for carry in carries[1:]:
            top = jnp.maximum(top, carry)
        return jnp.max(top)

    def cond(state):
        kj, top = state
        return jnp.logical_and(kj >= 0, top > LOG_W_ZERO)

    def body(state):
        kj, _ = state
        specs = [(head, kj, False, carry_ref[head]) for head in range(n_heads)]
        zs = _attn_scores(specs, q_heads, k_ref)
        ws, carries = _attn_weights(zs, _attn_sums(zs, specs, tri_ref, causal), specs, causal)
        for head, pv in enumerate(_attn_values(ws, specs, vt_ref)):
            acc_ref[head] += pv
            carry_ref[head] = carries[head]
        return kj - 1, max_carry(carries)

    top = max_carry([carry_ref[head] for head in range(n_heads)])
    lax.while_loop(cond, body, (qi - 2, top))


def _suffix_sum_matrix():
    j = np.arange(ATT_T // 2)
    return jnp.asarray(-(j[None, :] >= j[:, None]).astype(np.float32), BF16)


def _attn_ffn(x2d, q2d, k, vt, tri, w_o, gain, w13, w2, casts):
    m, d = x2d.shape
    b, s, d_attn = k.shape
    d_ff = w2.shape[0]
    n_heads = d_attn // HEAD_DIM
    n_tiles = m // ATT_T
    tiles_per_seq = s // ATT_T
    cur = lambda i: jnp.minimum(i, n_tiles - 1)
    prev = lambda i: jnp.maximum(i - 1, 0)
    jobs = [_cast_job(src, lead, n_tiles + 1) for src, lead in casts]
    operands = [x2d, q2d, k, vt, tri, w_o, gain.reshape(1, d), w13, w2]
    in_specs = [
        pl.BlockSpec((ATT_T, d), lambda i: (prev(i), 0)),
        pl.BlockSpec((ATT_T, d_attn), lambda i: (cur(i), 0)),
        pl.BlockSpec((None, s, d_attn), lambda i: (cur(i) // tiles_per_seq, 0, 0),
                     pipeline_mode=pl.Buffered(1)),
        pl.BlockSpec((None,) + vt.shape[1:], lambda i: (cur(i) // tiles_per_seq, 0, 0, 0, 0),
                     pipeline_mode=pl.Buffered(1)),
    ] + [_const_spec(op.shape) for op in operands[4:]]
    out = pl.pallas_call(
        functools.partial(
            _ffn_body, n_in=len(in_specs), n_out=1, n_cast=len(jobs),
            core=functools.partial(_attn_ffn_kernel, tiles_per_seq=tiles_per_seq, n_tiles=n_tiles)),
        grid=(n_tiles + 1,),
        in_specs=in_specs + [job[0] for job in jobs],
        out_specs=[pl.BlockSpec((ATT_T, d), lambda i: (prev(i), 0))] + [job[1] for job in jobs],
        out_shape=[jax.ShapeDtypeStruct((m, d), F32)] + [job[2] for job in jobs],
        scratch_shapes=[
            pltpu.VMEM((ATT_T, d_ff), BF16),
            pltpu.VMEM((n_heads, LANES, ATT_T), F32),
            pltpu.VMEM((n_heads, SUBLANES, ATT_T), F32),
        ],
        compiler_params=pltpu.CompilerParams(
            dimension_semantics=("arbitrary",), vmem_limit_bytes=VMEM_LIMIT),
        name="attn_ffn",
    )(*operands, *(src for src, _ in casts))
    return out[:1], out[1:]


def kernel(x, ffn1_norm, ffn1_w13, ffn1_w2, mix_norm, a_w_in, a_conv_w, a_conv_b, a_w_r, a_b_r,
           a_w_i, a_b_i, a_lambda, a_w_out, kv_norm, w_kv, k_norm, b_w_q, q_norm, b_w_o,
           ffn2_norm, ffn2_w13, ffn2_w2):
    b, s, d = x.shape
    depth = ffn1_norm.shape[0]
    n_a = a_w_in.shape[0]
    m = b * s
    pool = _head_pool_matrix()
    tri = _suffix_sum_matrix()

    assert 1 <= n_a < depth and s % FFN_TM == 0 and FFN_TM % ATT_T == 0

    def ffn_sources(l, which):
        w13, w2 = (ffn1_w13, ffn1_w2) if which == 1 else (ffn2_w13, ffn2_w2)
        srcs = {"w13": (w13, l), "w2": (w2, l)}
        if which == 2 and l == n_a - 1:
            srcs["w_kv"] = (w_kv, None)
        if l >= n_a:
            srcs["w_q" if which == 1 else "w_o"] = ((b_w_q if which == 1 else b_w_o), l - n_a)
        return srcs

    def mixer_sources(l):
        if l >= n_a:
            return {}
        lru_w = a_w_out.shape[1]
        return {"w_in": (a_w_in, l), "w_r": (a_w_r.reshape(n_a, lru_w, -1), l),
                "w_i": (a_w_i.reshape(n_a, lru_w, -1), l), "w_out": (a_w_out, l)}

    order = [(l, which) for l in range(depth) for which in (1, 2)]
    ready = {(0, 1, name): (src[lead] if lead is not None else src).astype(BF16)
             for name, (src, lead) in ffn_sources(0, 1).items()}

    x = x.reshape(m, d)
    k_sh = vt_sh = q = None
    for idx, (l, which) in enumerate(order):
        jobs = {}
        if which == 1:
            jobs.update({("mix", l, name): v for name, v in mixer_sources(l).items()})
        if idx + 1 < len(order):
            nxt = order[idx + 1]
            jobs.update({(*nxt, name): v for name, v in ffn_sources(*nxt).items()})
        own = {name: ready.pop((l, which, name)) for name in ffn_sources(l, which)}
        gain = (ffn1_norm if which == 1 else ffn2_norm)[l]
        args = (x, gain, own["w13"], own["w2"], list(jobs.values()))
        if which == 1 and l >= n_a:
            (x, q), cast = _ffn_q(*args, mix_norm[l], own["w_q"],
                                  jnp.tile(q_norm[l - n_a], N_HEADS).reshape(1, -1), pool)
        elif which == 2 and l >= n_a:
            (x,), cast = _attn_ffn(x, q, k_sh, vt_sh, tri, own["w_o"], gain, own["w13"], own["w2"],
                                   list(jobs.values()))
        elif which == 2:
            mix = {name: ready.pop(("mix", l, name)) for name in mixer_sources(l)}
            lru = (mix_norm[l], mix["w_in"], a_conv_w[l], a_conv_b[l], mix["w_r"], mix["w_i"],
                   a_b_r[l], a_b_i[l], a_lambda[l], mix["w_out"])
            kv = None
            if l == n_a - 1:
                kv = (kv_norm, own["w_kv"], jnp.tile(k_norm, N_HEADS).reshape(1, -1), pool)
            outs, cast = _lru_ffn(x, s, lru, (gain, own["w13"], own["w2"]), list(jobs.values()), kv)
            x = outs[0]
            if kv is not None:
                k_sh, vt_sh = outs[1].reshape(b, s, -1), outs[2]
        else:
            (x,), cast = _ffn(*args)
        ready.update(zip(jobs.keys(), cast))
    return x.reshape(b, s, d)
```

```python
import functools

import jax
import jax.numpy as jnp
import numpy as np
from jax import lax
from jax.experimental import pallas as pl
from jax.experimental.pallas import tpu as pltpu

EPS = 1e-6
N_HEADS = 16
HEAD_DIM = 64
CONV_W = 4
LRU_C = 8.0
LANES = 128
SUBLANES = 8
BF16_ROWS = 16
VMEM_LIMIT = 56 * 1024 * 1024

FFN_TM = 1024
FF_CHUNK = 256
LRU_TS = 256
ATT_T = 256
ATT_GROUP = 4
POOL_W = 256
LOG_W_ZERO = -104.0
LOG_W_NONE = -1e30

F32 = jnp.float32
BF16 = jnp.bfloat16


def _rms_norm_rows(x, gain_row):
    ms = jnp.mean(x * x, axis=-1, keepdims=True)
    return x * lax.rsqrt(ms + EPS) * gain_row


def _sigmoid(x):
    return 1.0 / (1.0 + jnp.exp(-x))


def _split_bf16(x):
    hi = x.astype(BF16)
    lo = (x - hi.astype(F32)).astype(BF16)
    return hi, lo


def _const_spec(shape):
    nd = len(shape)
    return pl.BlockSpec(shape, lambda *_: (0,) * nd, pipeline_mode=pl.Buffered(1))


def _swiglu_up(xn, w13_ref, act_ref, chunks):
    d_ff = act_ref.shape[1]
    for c in chunks:
        lo = c * FF_CHUNK
        gate = jnp.dot(xn, w13_ref[:, lo:lo + FF_CHUNK], preferred_element_type=F32)
        up = jnp.dot(xn, w13_ref[:, d_ff + lo:d_ff + lo + FF_CHUNK], preferred_element_type=F32)
        act_ref[:, lo:lo + FF_CHUNK] = (gate * _sigmoid(gate) * up).astype(BF16)


def _half_swiglu_step(x, g_ref, w13_ref, w2_ref, act_ref):
    xn = _rms_norm_rows(x, g_ref[...]).astype(BF16)
    _swiglu_up(xn, w13_ref, act_ref, range(w2_ref.shape[0] // FF_CHUNK))
    return x + 0.5 * jnp.dot(act_ref[...], w2_ref[...], preferred_element_type=F32)


def _ffn_kernel(x_ref, g_ref, w13_ref, w2_ref, y_ref, act_ref):
    y_ref[...] = _half_swiglu_step(x_ref[...], g_ref, w13_ref, w2_ref, act_ref)


def _kv_tail(y, ng_ref, wkv_ref, kg_ref, pool_ref, k_ref, vt_ref):
    d_attn = k_ref.shape[1]
    h = _rms_norm_rows(y, ng_ref[...]).astype(BF16)
    kv = jnp.dot(h, wkv_ref[...], preferred_element_type=F32)
    k_ref[...] = _head_rms_norm(kv[:, :d_attn], kg_ref[...], pool_ref).astype(BF16)
    for pair in range(vt_ref.shape[0]):
        for blk in range(vt_ref.shape[1]):
            v_blk = kv[blk * ATT_T:(blk + 1) * ATT_T,
                       d_attn + pair * LANES:d_attn + (pair + 1) * LANES]
            vt_ref[pair, blk] = v_blk.T.astype(BF16)


def _ffn_q_kernel(x_ref, g_ref, w13_ref, w2_ref, ng_ref, wq_ref, qg_ref, pool_ref,
                  y_ref, q_ref, act_ref):
    y = _half_swiglu_step(x_ref[...], g_ref, w13_ref, w2_ref, act_ref)
    y_ref[...] = y
    h = _rms_norm_rows(y, ng_ref[...]).astype(BF16)
    q = jnp.dot(h, wq_ref[...], preferred_element_type=F32)
    q_ref[...] = (_head_rms_norm(q, qg_ref[...], pool_ref) * (HEAD_DIM ** -0.5)).astype(BF16)


def _cast_job(src, lead, n_steps):
    rows, cols = src.shape[-2:]
    n_chunks = max(k for k in range(1, n_steps + 1)
                   if rows % k == 0 and (rows // k) % BF16_ROWS == 0)
    chunk = rows // n_chunks
    if lead is None:
        in_spec = pl.BlockSpec((chunk, cols), lambda i: (jnp.minimum(i, n_chunks - 1), 0))
    else:
        in_spec = pl.BlockSpec((None, chunk, cols),
                               lambda i: (lead, jnp.minimum(i, n_chunks - 1), 0))
    out_spec = pl.BlockSpec((chunk, cols), lambda i: (jnp.minimum(i, n_chunks - 1), 0))
    return in_spec, out_spec, jax.ShapeDtypeStruct((rows, cols), BF16)


def _ffn_body(*refs, core, n_in, n_out, n_cast):
    ins, rest = refs[:n_in], refs[n_in:]
    cast_in, rest = rest[:n_cast], rest[n_cast:]
    outs, rest = rest[:n_out], rest[n_out:]
    cast_out, scratch = rest[:n_cast], rest[n_cast:]
    for src_ref, dst_ref in zip(cast_in, cast_out):
        dst_ref[...] = src_ref[...].astype(BF16)
    core(*ins, *outs, *scratch)


def _ffn_call(body, name, x2d, gain, w13, w2, casts, extra_in=(), extra_in_specs=(),
              extra_out_shapes=(), extra_out_specs=()):
    m, d = x2d.shape
    d_ff = w2.shape[0]
    n_steps = m // FFN_TM
    row_spec = pl.BlockSpec((FFN_TM, d), lambda i: (i, 0))
    jobs = [_cast_job(src, lead, n_steps) for src, lead in casts]
    in_specs = [row_spec, _const_spec((1, d)), _const_spec((d, 2 * d_ff)),
                _const_spec((d_ff, d)), *extra_in_specs]
    n_out = 1 + len(extra_out_shapes)
    out = pl.pallas_call(
        functools.partial(_ffn_body, core=body, n_in=len(in_specs), n_out=n_out,
                          n_cast=len(jobs)),
        grid=(n_steps,),
        in_specs=in_specs + [job[0] for job in jobs],
        out_specs=[row_spec, *extra_out_specs] + [job[1] for job in jobs],
        out_shape=[jax.ShapeDtypeStruct((m, d), F32), *extra_out_shapes] + [job[2] for job in jobs],
        scratch_shapes=[pltpu.VMEM((FFN_TM, d_ff), BF16)],
        compiler_params=pltpu.CompilerParams(
            dimension_semantics=("arbitrary",), vmem_limit_bytes=VMEM_LIMIT),
        name=name,
    )(x2d, gain.reshape(1, d), w13, w2, *extra_in, *(src for src, _ in casts))
    return out[:n_out], out[n_out:]


def _ffn(x2d, gain, w13, w2, casts):
    return _ffn_call(_ffn_kernel, "ffn", x2d, gain, w13, w2, casts)


def _ffn_q(x2d, gain, w13, w2, casts, q_in_gain, w_q, q_gain_row, pool):
    m, d = x2d.shape
    d_attn = w_q.shape[1]
    return _ffn_call(
        _ffn_q_kernel, "ffn_q", x2d, gain, w13, w2, casts,
        extra_in=(q_in_gain.reshape(1, d), w_q, q_gain_row, pool),
        extra_in_specs=(_const_spec((1, d)), _const_spec((d, d_attn)),
                        _const_spec((1, d_attn)), _const_spec((POOL_W, POOL_W))),
        extra_out_shapes=(jax.ShapeDtypeStruct((m, d_attn), BF16),),
        extra_out_specs=(pl.BlockSpec((FFN_TM, d_attn), lambda i: (i, 0)),))


def _gelu_tanh(x):
    c = np.float32(np.sqrt(2.0 / np.pi))
    return x * (0.5 * (1.0 + jnp.tanh(c * (x + 0.044715 * (x * x * x)))))


def _lru_in(x, g_ref, win_ref):
    h = _rms_norm_rows(x, g_ref[...]).astype(BF16)
    return jnp.dot(h, win_ref[...], preferred_element_type=F32)


def _lru_gates(proj, cw_ref, cb_ref, wr_ref, wi_ref, br_ref, bi_ref, lam_ref,
               rec_ref, a_ref, u_ref, gate_ref):
    ts, w = gate_ref.shape
    bw = wr_ref.shape[1]
    n_blocks = w // bw
    gate_ref[...] = _gelu_tanh(proj[:, :w])
    rec = proj[:, w:]
    rec_ref[SUBLANES:SUBLANES + ts, :] = rec
    xc = cb_ref[...] + cw_ref[CONV_W - 1:CONV_W, :] * rec
    for k in range(CONV_W - 1):
        back = CONV_W - 1 - k
        xc = xc + cw_ref[k:k + 1, :] * rec_ref[SUBLANES - back:SUBLANES - back + ts, :]
    rec_ref[0:SUBLANES, :] = rec_ref[ts:ts + SUBLANES, :]

    xcb = xc.astype(BF16)
    sp_lam = jnp.maximum(-lam_ref[...], 0.0) + jnp.log(1.0 + jnp.exp(-jnp.abs(lam_ref[...])))
    for n in range(n_blocks):
        sl = slice(n * bw, (n + 1) * bw)
        r = _sigmoid(jnp.dot(xcb[:, sl], wr_ref[sl, :], preferred_element_type=F32) + br_ref[:, sl])
        i = _sigmoid(jnp.dot(xcb[:, sl], wi_ref[sl, :], preferred_element_type=F32) + bi_ref[:, sl])
        log_a = (-LRU_C) * r * sp_lam[:, sl]
        a = jnp.exp(log_a)
        a_ref[:, sl] = a
        u_ref[:, sl] = jnp.sqrt(-jnp.tanh(log_a) * (1.0 + a * a)) * (i * xc[:, sl])


def _lru_scan(a_ref, u_ref, hcar_ref):
    ts, w = u_ref.shape
    row = lax.broadcasted_iota(jnp.int32, (SUBLANES, w), 0)
    hprev = hcar_ref[...]
    for gidx in range(ts // SUBLANES):
        rows = slice(gidx * SUBLANES, (gidx + 1) * SUBLANES)
        a = a_ref[rows, :]
        u = u_ref[rows, :]
        for sh in (1, 2, 4):
            keep = row >= sh
            u_new = jnp.where(keep, u + a * pltpu.roll(u, sh, 0), u)
            a = jnp.where(keep, a * pltpu.roll(a, sh, 0), a)
            u = u_new
        hs = u + a * hprev
        u_ref[rows, :] = hs
        hprev = hs[SUBLANES - 1:SUBLANES, :]
    hcar_ref[...] = hprev


def _lru_out(x, u_ref, gate_ref, wout_ref):
    y = (u_ref[...] * gate_ref[...]).astype(BF16)
    return x + jnp.dot(y, wout_ref[...], preferred_element_type=F32)


N_LRU_IN = 10
N_FFN_IN = 3


def _lru_ffn_kernel(*refs, tiles_per_seq, with_kv):
    x_ref, refs = refs[0], refs[1:]
    lru_in, refs = refs[:N_LRU_IN], refs[N_LRU_IN:]
    ffn_in, refs = refs[:N_FFN_IN], refs[N_FFN_IN:]
    if with_kv:
        kv_in, refs = refs[:4], refs[4:]
        (y_ref, k_ref, vt_ref), refs = refs[:3], refs[3:]
    else:
        y_ref, refs = refs[0], refs[1:]
    act_ref, mid_ref, rec_ref, a_ref, u_ref, gate_ref, hcar_ref = refs
    step = pl.program_id(0)

    @pl.when(step == 0)
    def _():
        mid_ref[...] = jnp.zeros_like(mid_ref)

    @pl.when(step % tiles_per_seq == 0)
    def _():
        rec_ref[0:SUBLANES, :] = jnp.zeros((SUBLANES, rec_ref.shape[1]), F32)
        hcar_ref[...] = jnp.zeros_like(hcar_ref)

    mg_ref, win_ref, cw_ref, cb_ref, wr_ref, wi_ref, br_ref, bi_ref, lam_ref, wout_ref = lru_in
    g_ref, w13_ref, w2_ref = ffn_in
    n_chunks = w2_ref.shape[0] // FF_CHUNK
    x = x_ref[...]
    x_prev = mid_ref[...]
    proj = _lru_in(x, mg_ref, win_ref)
    xn = _rms_norm_rows(x_prev, g_ref[...]).astype(BF16)
    _swiglu_up(xn, w13_ref, act_ref, range(0, n_chunks // 2))
    _lru_gates(proj, cw_ref, cb_ref, wr_ref, wi_ref, br_ref, bi_ref, lam_ref,
               rec_ref, a_ref, u_ref, gate_ref)
    _swiglu_up(xn, w13_ref, act_ref, range(n_chunks // 2, n_chunks))
    y = x_prev + 0.5 * jnp.dot(act_ref[...], w2_ref[...], preferred_element_type=F32)
    y_ref[...] = y
    _lru_scan(a_ref, u_ref, hcar_ref)
    if with_kv:
        _kv_tail(y, *kv_in, k_ref, vt_ref)
    mid_ref[...] = _lru_out(x, u_ref, gate_ref, wout_ref)


def _lru_ffn(x2d, seq_len, lru, ffn, casts, kv=None):
    m, d = x2d.shape
    gain, w_in, conv_w, conv_b, w_r, w_i, b_r, b_i, lam, w_out = lru
    f_gain, w13, w2 = ffn
    w = w_out.shape[0]
    d_ff = w2.shape[0]
    n_tiles = m // LRU_TS
    tiles_per_seq = seq_len // LRU_TS
    last = n_tiles - 1
    prev = lambda i: jnp.maximum(i - 1, 0)
    jobs = [_cast_job(src, lead, n_tiles + 1) for src, lead in casts]
    row_vec = lambda v: v.reshape(1, -1)
    operands = [x2d, row_vec(gain), w_in, conv_w, row_vec(conv_b), w_r, w_i, row_vec(b_r),
                row_vec(b_i), row_vec(lam), w_out, row_vec(f_gain), w13, w2]
    in_specs = [pl.BlockSpec((LRU_TS, d), lambda i: (jnp.minimum(i, last), 0))]
    in_specs += [_const_spec(op.shape) for op in operands[1:]]
    out_specs = [pl.BlockSpec((LRU_TS, d), lambda i: (prev(i), 0))]
    out_shape = [jax.ShapeDtypeStruct((m, d), F32)]
    if kv is not None:
        kv_gain, w_kv, k_gain_row, pool = kv
        d_attn = w_kv.shape[1] // 2
        n_pairs = d_attn // LANES
        kv_ops = [row_vec(kv_gain), w_kv, k_gain_row, pool]
        operands += kv_ops
        in_specs += [_const_spec(op.shape) for op in kv_ops]
        out_specs += [
            pl.BlockSpec((LRU_TS, d_attn), lambda i: (prev(i), 0)),
            pl.BlockSpec((None, n_pairs, LRU_TS // ATT_T, LANES, ATT_T),
                         lambda i: (prev(i) // tiles_per_seq, 0, prev(i) % tiles_per_seq, 0, 0))]
        out_shape += [
            jax.ShapeDtypeStruct((m, d_attn), BF16),
            jax.ShapeDtypeStruct((m // seq_len, n_pairs, seq_len // ATT_T, LANES, ATT_T), BF16)]
    n_out = len(out_shape)
    out = pl.pallas_call(
        functools.partial(
            _ffn_body, n_in=len(in_specs), n_out=n_out, n_cast=len(jobs),
            core=functools.partial(_lru_ffn_kernel, tiles_per_seq=tiles_per_seq,
                                   with_kv=kv is not None)),
        grid=(n_tiles + 1,),
        in_specs=in_specs + [job[0] for job in jobs],
        out_specs=out_specs + [job[1] for job in jobs],
        out_shape=out_shape + [job[2] for job in jobs],
        scratch_shapes=[
            pltpu.VMEM((LRU_TS, d_ff), BF16),
            pltpu.VMEM((LRU_TS, d), F32),
            pltpu.VMEM((LRU_TS + SUBLANES, w), F32),
            pltpu.VMEM((LRU_TS, w), F32),
            pltpu.VMEM((LRU_TS, w), F32),
            pltpu.VMEM((LRU_TS, w), F32),
            pltpu.VMEM((1, w), F32),
        ],
        compiler_params=pltpu.CompilerParams(
            dimension_semantics=("arbitrary",), vmem_limit_bytes=VMEM_LIMIT),
        name="lru_ffn",
    )(*operands, *(src for src, _ in casts))
    return out[:n_out], out[n_out:]


def _head_rms_norm(t, gain_row, pool_ref):
    outs = []
    for g in range(t.shape[1] // POOL_W):
        tg = t[:, g * POOL_W:(g + 1) * POOL_W]
        ms = jnp.dot((tg * tg).astype(BF16), pool_ref[...], preferred_element_type=F32)
        outs.append(tg * lax.rsqrt(ms + EPS))
    return jnp.concatenate(outs, axis=1) * gain_row


def _head_pool_matrix():
    idx = np.arange(POOL_W) // HEAD_DIM
    return jnp.asarray((idx[:, None] == idx[None, :]).astype(np.float32) / HEAD_DIM, BF16)


def _attn_causal():
    k_pos = lax.broadcasted_iota(jnp.int32, (ATT_T, ATT_T), 0)
    q_pos = lax.broadcasted_iota(jnp.int32, (ATT_T, ATT_T), 1)
    return k_pos < q_pos


def _attn_halves(z, masked, causal):
    h = ATT_T // 2
    if masked:
        return [(z[:h], causal[:h], 0), (z[h:, h:], causal[h:, h:], h)]
    return [(z[:h], None, 0), (z[h:], None, 0)]


def _attn_widen(x, q0):
    return x if q0 == 0 else jnp.concatenate([jnp.zeros((x.shape[0], q0), x.dtype), x], axis=1)


def _attn_scores(specs, q_heads, k_ref):
    zs = []
    for head, kj, _, _ in specs:
        pair = head // 2
        kb = k_ref[pl.ds(pl.multiple_of(kj * ATT_T, ATT_T), ATT_T), pair * LANES:(pair + 1) * LANES]
        zs.append(lax.dot_general(kb, q_heads[head], (((1,), (1,)), ((), ())),
                                  preferred_element_type=F32))
    return zs


def _attn_sums(zs, specs, tri_ref, causal):
    exts = []
    for z, (_, _, masked, _) in zip(zs, specs):
        ext = []
        for zp, mask, q0 in _attn_halves(z, masked, causal):
            sp = jnp.maximum(zp, 0.0) + jnp.log(1.0 + jnp.exp(-jnp.abs(zp)))
            sp16 = (sp if mask is None else jnp.where(mask, sp, 0.0)).astype(BF16)
            ext.append(jnp.dot(tri_ref[...], _attn_widen(sp16, q0), preferred_element_type=F32))
        exts.append(ext)
    return exts


def _attn_weights(zs, exts, specs, causal, chain_valid=None):
    h = ATT_T // 2
    ws, carries = [], []
    for z, (ext_early, ext_late), (_, _, masked, carry) in zip(zs, exts, specs):
        if isinstance(carry, int):
            carry = carries[carry]
            if chain_valid is not None:
                carry = jnp.where(chain_valid, carry, LOG_W_NONE)
        total_late = jnp.broadcast_to(ext_late[0:1], (SUBLANES, ATT_T))
        carry_early = total_late if carry is None else carry + total_late
        w_halves = []
        for (zp, mask, q0), ext, c in zip(_attn_halves(z, masked, causal), (ext_early, ext_late),
                                          (carry_early, carry)):
            log_w = zp + ext[:, q0:]
            if c is not None:
                log_w = log_w + c[0:1, q0:]
            w = jnp.exp(log_w)
            if mask is not None:
                w = jnp.where(mask, w, 0.0)
            w_halves.append(_attn_widen(w.astype(BF16), q0))
        ws.append(jnp.concatenate(w_halves, axis=0))
        carries.append(carry_early + jnp.broadcast_to(ext_early[0:1], (SUBLANES, ATT_T)))
    return ws, carries


def _attn_values(ws, specs, vt_ref):
    return [jnp.dot(vt_ref[head // 2, kj], w, preferred_element_type=F32)
            for w, (head, kj, _, _) in zip(ws, specs)]


def _attn_ffn_kernel(x_ref, q_ref, k_ref, vt_ref, tri_ref, wo_ref, g_ref, w13_ref, w2_ref, y_ref,
                     act_ref, acc_ref, carry_ref, *, tiles_per_seq, n_tiles):
    t = ATT_T
    n_heads = acc_ref.shape[0]
    n_chunks = w2_ref.shape[0] // FF_CHUNK
    step = pl.program_id(0)
    qi = jnp.minimum(step, n_tiles - 1) % tiles_per_seq

    @pl.when(step == 0)
    def _():
        acc_ref[...] = jnp.zeros_like(acc_ref)

    feat = lax.broadcasted_iota(jnp.int32, (LANES, t), 0)
    o_prev = jnp.concatenate(
        [jnp.where(feat < HEAD_DIM, acc_ref[2 * pair], acc_ref[2 * pair + 1]).T.astype(BF16)
         for pair in range(n_heads // 2)], axis=1)

    lane = lax.broadcasted_iota(jnp.int32, (t, LANES), 1)
    causal = _attn_causal()
    q_heads = []
    for pair in range(n_heads // 2):
        q = q_ref[:, pair * LANES:(pair + 1) * LANES]
        q_heads += [jnp.where(lane < HEAD_DIM, q, jnp.zeros_like(q)),
                    jnp.where(lane >= HEAD_DIM, q, jnp.zeros_like(q))]

    has_prev = qi > 0
    kj_prev = jnp.maximum(qi - 1, 0)
    groups = [range(g, g + ATT_GROUP) for g in range(0, n_heads, ATT_GROUP)]
    n_up = 2 * len(groups) - 1
    up_slices = [range(n_chunks * s // n_up, n_chunks * (s + 1) // n_up) for s in range(n_up)]
    ffn_stage = iter(up_slices)
    x_in = xn = None
    results = {}
    for gidx, heads in enumerate(groups):
        specs = ([(head, qi, True, None) for head in heads]
                 + [(head, kj_prev, False, idx) for idx, head in enumerate(heads)])
        zs = _attn_scores(specs, q_heads, k_ref)
        if gidx == 0:
            x_in = x_ref[...] + jnp.dot(o_prev, wo_ref[...], preferred_element_type=F32)
            xn = _rms_norm_rows(x_in, g_ref[...]).astype(BF16)
        else:
            _swiglu_up(xn, w13_ref, act_ref, next(ffn_stage))
        exts = _attn_sums(zs, specs, tri_ref, causal)
        _swiglu_up(xn, w13_ref, act_ref, next(ffn_stage))
        ws, carries = _attn_weights(zs, exts, specs, causal, chain_valid=has_prev)
        pvs = _attn_values(ws, specs, vt_ref)
        for idx, head in enumerate(heads):
            results[head] = (pvs[idx] + pvs[len(heads) + idx], carries[len(heads) + idx])
    y_ref[...] = x_in + 0.5 * jnp.dot(act_ref[...], w2_ref[...], preferred_element_type=F32)
    for head, (pv, carry) in results.items():
        acc_ref[head] = pv
        carry_ref[head] = carry

    def max_carry(carries):
        top = carries[0]
        for carry in carries[1:]:
            top = jnp.maximum(top, carry)
        return jnp.max(top)

    def cond(state):
        kj, top = state
        return jnp.logical_and(kj >= 0, top > LOG_W_ZERO)

    def body(state):
        kj, _ = state
        specs = [(head, kj, False, carry_ref[head]) for head in range(n_heads)]
        zs = _attn_scores(specs, q_heads, k_ref)
        ws, carries = _attn_weights(zs, _attn_sums(zs, specs, tri_ref, causal), specs, causal)
        for head, pv in enumerate(_attn_values(ws, specs, vt_ref)):
            acc_ref[head] += pv
            carry_ref[head] = carries[head]
        return kj - 1, max_carry(carries)

    top = max_carry([carry_ref[head] for head in range(n_heads)])
    lax.while_loop(cond, body, (qi - 2, top))


def _suffix_sum_matrix():
    j = np.arange(ATT_T // 2)
    return jnp.asarray(-(j[None, :] >= j[:, None]).astype(np.float32), BF16)


def _attn_ffn(x2d, q2d, k, vt, tri, w_o, gain, w13, w2, casts):
    m, d = x2d.shape
    b, s, d_attn = k.shape
    d_ff = w2.shape[0]
    n_heads = d_attn // HEAD_DIM
    n_tiles = m // ATT_T
    tiles_per_seq = s // ATT_T
    cur = lambda i: jnp.minimum(i, n_tiles - 1)
    prev = lambda i: jnp.maximum(i - 1, 0)
    jobs = [_cast_job(src, lead, n_tiles + 1) for src, lead in casts]
    operands = [x2d, q2d, k, vt, tri, w_o, gain.reshape(1, d), w13, w2]
    in_specs = [
        pl.BlockSpec((ATT_T, d), lambda i: (prev(i), 0)),
        pl.BlockSpec((ATT_T, d_attn), lambda i: (cur(i), 0)),
        pl.BlockSpec((None, s, d_attn), lambda i: (cur(i) // tiles_per_seq, 0, 0),
                     pipeline_mode=pl.Buffered(1)),
        pl.BlockSpec((None,) + vt.shape[1:], lambda i: (cur(i) // tiles_per_seq, 0, 0, 0, 0),
                     pipeline_mode=pl.Buffered(1)),
    ] + [_const_spec(op.shape) for op in operands[4:]]
    out = pl.pallas_call(
        functools.partial(
            _ffn_body, n_in=len(in_specs), n_out=1, n_cast=len(jobs),
            core=functools.partial(_attn_ffn_kernel, tiles_per_seq=tiles_per_seq, n_tiles=n_tiles)),
        grid=(n_tiles + 1,),
        in_specs=in_specs + [job[0] for job in jobs],
        out_specs=[pl.BlockSpec((ATT_T, d), lambda i: (prev(i), 0))] + [job[1] for job in jobs],
        out_shape=[jax.ShapeDtypeStruct((m, d), F32)] + [job[2] for job in jobs],
        scratch_shapes=[
            pltpu.VMEM((ATT_T, d_ff), BF16),
            pltpu.VMEM((n_heads, LANES, ATT_T), F32),
            pltpu.VMEM((n_heads, SUBLANES, ATT_T), F32),
        ],
        compiler_params=pltpu.CompilerParams(
            dimension_semantics=("arbitrary",), vmem_limit_bytes=VMEM_LIMIT),
        name="attn_ffn",
    )(*operands, *(src for src, _ in casts))
    return out[:1], out[1:]


def kernel(x, ffn1_norm, ffn1_w13, ffn1_w2, mix_norm, a_w_in, a_conv_w, a_conv_b, a_w_r, a_b_r,
           a_w_i, a_b_i, a_lambda, a_w_out, kv_norm, w_kv, k_norm, b_w_q, q_norm, b_w_o,
           ffn2_norm, ffn2_w13, ffn2_w2):
    b, s, d = x.shape
    depth = ffn1_norm.shape[0]
    n_a = a_w_in.shape[0]
    m = b * s
    pool = _head_pool_matrix()
    tri = _suffix_sum_matrix()

    assert 1 <= n_a < depth and s % FFN_TM == 0 and FFN_TM % ATT_T == 0

    def ffn_sources(l, which):
        w13, w2 = (ffn1_w13, ffn1_w2) if which == 1 else (ffn2_w13, ffn2_w2)
        srcs = {"w13": (w13, l), "w2": (w2, l)}
        if which == 2 and l == n_a - 1:
            srcs["w_kv"] = (w_kv, None)
        if l >= n_a:
            srcs["w_q" if which == 1 else "w_o"] = ((b_w_q if which == 1 else b_w_o), l - n_a)
        return srcs

    def mixer_sources(l):
        if l >= n_a:
            return {}
        lru_w = a_w_out.shape[1]
        return {"w_in": (a_w_in, l), "w_r": (a_w_r.reshape(n_a, lru_w, -1), l),
                "w_i": (a_w_i.reshape(n_a, lru_w, -1), l), "w_out": (a_w_out, l)}

    order = [(l, which) for l in range(depth) for which in (1, 2)]
    ready = {(0, 1, name): (src[lead] if lead is not None else src).astype(BF16)
             for name, (src, lead) in ffn_sources(0, 1).items()}

    x = x.reshape(m, d)
    k_sh = vt_sh = q = None
    for idx, (l, which) in enumerate(order):
        jobs = {}
        if which == 1:
            jobs.update({("mix", l, name): v for name, v in mixer_sources(l).items()})
        if idx + 1 < len(order):
            nxt = order[idx + 1]
            jobs.update({(*nxt, name): v for name, v in ffn_sources(*nxt).items()})
        own = {name: ready.pop((l, which, name)) for name in ffn_sources(l, which)}
        gain = (ffn1_norm if which == 1 else ffn2_norm)[l]
        args = (x, gain, own["w13"], own["w2"], list(jobs.values()))
        if which == 1 and l >= n_a:
            (x, q), cast = _ffn_q(*args, mix_norm[l], own["w_q"],
                                  jnp.tile(q_norm[l - n_a], N_HEADS).reshape(1, -1), pool)
        elif which == 2 and l >= n_a:
            (x,), cast = _attn_ffn(x, q, k_sh, vt_sh, tri, own["w_o"], gain, own["w13"], own["w2"],
                                   list(jobs.values()))
        elif which == 2:
            mix = {name: ready.pop(("mix", l, name)) for name in mixer_sources(l)}
            lru = (mix_norm[l], mix["w_in"], a_conv_w[l], a_conv_b[l], mix["w_r"], mix["w_i"],
                   a_b_r[l], a_b_i[l], a_lambda[l], mix["w_out"])
            kv = None
            if l == n_a - 1:
                kv = (kv_norm, own["w_kv"], jnp.tile(k_norm, N_HEADS).reshape(1, -1), pool)
            outs, cast = _lru_ffn(x, s, lru, (gain, own["w13"], own["w2"]), list(jobs.values()), kv)
            x = outs[0]
            if kv is not None:
                k_sh, vt_sh = outs[1].reshape(b, s, -1), outs[2]
        else:
            (x,), cast = _ffn(*args)
        ready.update(zip(jobs.keys(), cast))
    return x.reshape(b, s, d)
```

```python
import functools

import jax
import jax.numpy as jnp
import numpy as np
from jax import lax
from jax.experimental import pallas as pl
from jax.experimental.pallas import tpu as pltpu

EPS = 1e-6
N_HEADS = 16
HEAD_DIM = 64
CONV_W = 4
LRU_C = 8.0
LANES = 128
SUBLANES = 8
BF16_ROWS = 16
VMEM_LIMIT = 56 * 1024 * 1024

FFN_TM = 1024
FF_CHUNK = 256
LRU_TS = 256
ATT_T = 256
ATT_GROUP = 4
POOL_W = 256
LOG_W_ZERO = -104.0
LOG_W_NONE = -1e30

F32 = jnp.float32
BF16 = jnp.bfloat16


def _rms_norm_rows(x, gain_row):
    ms = jnp.mean(x * x, axis=-1, keepdims=True)
    return x * lax.rsqrt(ms + EPS) * gain_row


def _sigmoid(x):
    return 1.0 / (1.0 + jnp.exp(-x))


def _const_spec(shape):
    nd = len(shape)
    return pl.BlockSpec(shape, lambda *_: (0,) * nd, pipeline_mode=pl.Buffered(1))


def _swiglu_up(xn, w13_ref, act_ref, chunks):
    d_ff = act_ref.shape[1]
    for c in chunks:
        lo = c * FF_CHUNK
        gate = jnp.dot(xn, w13_ref[:, lo:lo + FF_CHUNK], preferred_element_type=F32)
        up = jnp.dot(xn, w13_ref[:, d_ff + lo:d_ff + lo + FF_CHUNK], preferred_element_type=F32)
        act_ref[:, lo:lo + FF_CHUNK] = (gate * _sigmoid(gate) * up).astype(BF16)


def _half_swiglu_step(x, g_ref, w13_ref, w2_ref, act_ref):
    xn = _rms_norm_rows(x, g_ref[...]).astype(BF16)
    _swiglu_up(xn, w13_ref, act_ref, range(w2_ref.shape[0] // FF_CHUNK))
    return x + 0.5 * jnp.dot(act_ref[...], w2_ref[...], preferred_element_type=F32)


def _ffn_kernel(x_ref, g_ref, w13_ref, w2_ref, y_ref, act_ref):
    y_ref[...] = _half_swiglu_step(x_ref[...], g_ref, w13_ref, w2_ref, act_ref)


def _kv_tail(y, ng_ref, wkv_ref, kg_ref, pool_ref, k_ref, vt_ref):
    d_attn = k_ref.shape[1]
    h = _rms_norm_rows(y, ng_ref[...]).astype(BF16)
    kv = jnp.dot(h, wkv_ref[...], preferred_element_type=F32)
    k_ref[...] = _head_rms_norm(kv[:, :d_attn], kg_ref[...], pool_ref).astype(BF16)
    for pair in range(vt_ref.shape[0]):
        for blk in range(vt_ref.shape[1]):
            v_blk = kv[blk * ATT_T:(blk + 1) * ATT_T,
                       d_attn + pair * LANES:d_attn + (pair + 1) * LANES]
            vt_ref[pair, blk] = v_blk.T.astype(BF16)


def _ffn_q_kernel(x_ref, g_ref, w13_ref, w2_ref, ng_ref, wq_ref, qg_ref, pool_ref,
                  y_ref, q_ref, act_ref):
    y = _half_swiglu_step(x_ref[...], g_ref, w13_ref, w2_ref, act_ref)
    y_ref[...] = y
    h = _rms_norm_rows(y, ng_ref[...]).astype(BF16)
    q = jnp.dot(h, wq_ref[...], preferred_element_type=F32)
    q_ref[...] = (_head_rms_norm(q, qg_ref[...], pool_ref) * (HEAD_DIM ** -0.5)).astype(BF16)


def _cast_job(src, lead, n_steps):
    rows, cols = src.shape[-2:]
    n_chunks = max(k for k in range(1, n_steps + 1)
                   if rows % k == 0 and (rows // k) % BF16_ROWS == 0)
    chunk = rows // n_chunks
    if lead is None:
        in_spec = pl.BlockSpec((chunk, cols), lambda i: (jnp.minimum(i, n_chunks - 1), 0))
    else:
        in_spec = pl.BlockSpec((None, chunk, cols),
                               lambda i: (lead, jnp.minimum(i, n_chunks - 1), 0))
    out_spec = pl.BlockSpec((chunk, cols), lambda i: (jnp.minimum(i, n_chunks - 1), 0))
    return in_spec, out_spec, jax.ShapeDtypeStruct((rows, cols), BF16)


def _ffn_body(*refs, core, n_in, n_out, n_cast):
    ins, rest = refs[:n_in], refs[n_in:]
    cast_in, rest = rest[:n_cast], rest[n_cast:]
    outs, rest = rest[:n_out], rest[n_out:]
    cast_out, scratch = rest[:n_cast], rest[n_cast:]
    for src_ref, dst_ref in zip(cast_in, cast_out):
        dst_ref[...] = src_ref[...].astype(BF16)
    core(*ins, *outs, *scratch)


def _ffn_call(body, name, x2d, gain, w13, w2, casts, extra_in=(), extra_in_specs=(),
              extra_out_shapes=(), extra_out_specs=()):
    m, d = x2d.shape
    d_ff = w2.shape[0]
    n_steps = m // FFN_TM
    row_spec = pl.BlockSpec((FFN_TM, d), lambda i: (i, 0))
    jobs = [_cast_job(src, lead, n_steps) for src, lead in casts]
    in_specs = [row_spec, _const_spec((1, d)), _const_spec((d, 2 * d_ff)),
                _const_spec((d_ff, d)), *extra_in_specs]
    n_out = 1 + len(extra_out_shapes)
    out = pl.pallas_call(
        functools.partial(_ffn_body, core=body, n_in=len(in_specs), n_out=n_out,
                          n_cast=len(jobs)),
        grid=(n_steps,),
        in_specs=in_specs + [job[0] for job in jobs],
        out_specs=[row_spec, *extra_out_specs] + [job[1] for job in jobs],
        out_shape=[jax.ShapeDtypeStruct((m, d), F32), *extra_out_shapes] + [job[2] for job in jobs],
        scratch_shapes=[pltpu.VMEM((FFN_TM, d_ff), BF16)],
        compiler_params=pltpu.CompilerParams(
            dimension_semantics=("arbitrary",), vmem_limit_bytes=VMEM_LIMIT),
        name=name,
    )(x2d, gain.reshape(1, d), w13, w2, *extra_in, *(src for src, _ in casts))
    return out[:n_out], out[n_out:]


def _ffn(x2d, gain, w13, w2, casts):
    return _ffn_call(_ffn_kernel, "ffn", x2d, gain, w13, w2, casts)


def _ffn_q(x2d, gain, w13, w2, casts, q_in_gain, w_q, q_gain_row, pool):
    m, d = x2d.shape
    d_attn = w_q.shape[1]
    return _ffn_call(
        _ffn_q_kernel, "ffn_q", x2d, gain, w13, w2, casts,
        extra_in=(q_in_gain.reshape(1, d), w_q, q_gain_row, pool),
        extra_in_specs=(_const_spec((1, d)), _const_spec((d, d_attn)),
                        _const_spec((1, d_attn)), _const_spec((POOL_W, POOL_W))),
        extra_out_shapes=(jax.ShapeDtypeStruct((m, d_attn), BF16),),
        extra_out_specs=(pl.BlockSpec((FFN_TM, d_attn), lambda i: (i, 0)),))


def _gelu_tanh(x):
    c = np.float32(np.sqrt(2.0 / np.pi))
    return x * (0.5 * (1.0 + jnp.tanh(c * (x + 0.044715 * (x * x * x)))))


def _lru_in(x, g_ref, win_ref):
    h = _rms_norm_rows(x, g_ref[...]).astype(BF16)
    return jnp.dot(h, win_ref[...], preferred_element_type=F32)


def _lru_gates(proj, cw_ref, cb_ref, wr_ref, wi_ref, br_ref, bi_ref, lam_ref,
               rec_ref, a_ref, u_ref, gate_ref):
    ts, w = gate_ref.shape
    bw = wr_ref.shape[1]
    n_blocks = w // bw
    gate_ref[...] = _gelu_tanh(proj[:, :w])
    rec = proj[:, w:]
    rec_ref[SUBLANES:SUBLANES + ts, :] = rec
    xc = cb_ref[...] + cw_ref[CONV_W - 1:CONV_W, :] * rec
    for k in range(CONV_W - 1):
        back = CONV_W - 1 - k
        xc = xc + cw_ref[k:k + 1, :] * rec_ref[SUBLANES - back:SUBLANES - back + ts, :]
    rec_ref[0:SUBLANES, :] = rec_ref[ts:ts + SUBLANES, :]

    xcb = xc.astype(BF16)
    sp_lam = jnp.maximum(-lam_ref[...], 0.0) + jnp.log(1.0 + jnp.exp(-jnp.abs(lam_ref[...])))
    for n in range(n_blocks):
        sl = slice(n * bw, (n + 1) * bw)
        r = _sigmoid(jnp.dot(xcb[:, sl], wr_ref[sl, :], preferred_element_type=F32) + br_ref[:, sl])
        i = _sigmoid(jnp.dot(xcb[:, sl], wi_ref[sl, :], preferred_element_type=F32) + bi_ref[:, sl])
        log_a = (-LRU_C) * r * sp_lam[:, sl]
        a = jnp.exp(log_a)
        a_ref[:, sl] = a
        u_ref[:, sl] = jnp.sqrt(-jnp.tanh(log_a) * (1.0 + a * a)) * (i * xc[:, sl])


def _lru_scan(a_ref, u_ref, hcar_ref):
    ts, w = u_ref.shape
    row = lax.broadcasted_iota(jnp.int32, (SUBLANES, w), 0)
    hprev = hcar_ref[...]
    for gidx in range(ts // SUBLANES):
        rows = slice(gidx * SUBLANES, (gidx + 1) * SUBLANES)
        a = a_ref[rows, :]
        u = u_ref[rows, :]
        for sh in (1, 2, 4):
            keep = row >= sh
            u_new = jnp.where(keep, u + a * pltpu.roll(u, sh, 0), u)
            a = jnp.where(keep, a * pltpu.roll(a, sh, 0), a)
            u = u_new
        hs = u + a * hprev
        u_ref[rows, :] = hs
        hprev = hs[SUBLANES - 1:SUBLANES, :]
    hcar_ref[...] = hprev


def _lru_out(x, u_ref, gate_ref, wout_ref):
    y = (u_ref[...] * gate_ref[...]).astype(BF16)
    return x + jnp.dot(y, wout_ref[...], preferred_element_type=F32)


N_LRU_IN = 10
N_FFN_IN = 3


def _lru_ffn_kernel(*refs, tiles_per_seq, with_kv):
    x_ref, refs = refs[0], refs[1:]
    lru_in, refs = refs[:N_LRU_IN], refs[N_LRU_IN:]
    ffn_in, refs = refs[:N_FFN_IN], refs[N_FFN_IN:]
    if with_kv:
        kv_in, refs = refs[:4], refs[4:]
        (y_ref, k_ref, vt_ref), refs = refs[:3], refs[3:]
    else:
        y_ref, refs = refs[0], refs[1:]
    act_ref, mid_ref, rec_ref, a_ref, u_ref, gate_ref, hcar_ref = refs
    step = pl.program_id(0)

    @pl.when(step == 0)
    def _():
        mid_ref[...] = jnp.zeros_like(mid_ref)

    @pl.when(step % tiles_per_seq == 0)
    def _():
        rec_ref[0:SUBLANES, :] = jnp.zeros((SUBLANES, rec_ref.shape[1]), F32)
        hcar_ref[...] = jnp.zeros_like(hcar_ref)

    mg_ref, win_ref, cw_ref, cb_ref, wr_ref, wi_ref, br_ref, bi_ref, lam_ref, wout_ref = lru_in
    g_ref, w13_ref, w2_ref = ffn_in
    n_chunks = w2_ref.shape[0] // FF_CHUNK
    x = x_ref[...]
    x_prev = mid_ref[...]
    cut = [-(-n_chunks * s // 3) for s in range(4)]
    xn = _rms_norm_rows(x_prev, g_ref[...]).astype(BF16)
    _swiglu_up(xn, w13_ref, act_ref, range(cut[0], cut[1]))
    proj = _lru_in(x, mg_ref, win_ref)
    _swiglu_up(xn, w13_ref, act_ref, range(cut[1], cut[2]))
    _lru_gates(proj, cw_ref, cb_ref, wr_ref, wi_ref, br_ref, bi_ref, lam_ref,
               rec_ref, a_ref, u_ref, gate_ref)
    _swiglu_up(xn, w13_ref, act_ref, range(cut[2], cut[3]))
    y = x_prev + 0.5 * jnp.dot(act_ref[...], w2_ref[...], preferred_element_type=F32)
    y_ref[...] = y
    if with_kv:
        _kv_tail(y, *kv_in, k_ref, vt_ref)
    _lru_scan(a_ref, u_ref, hcar_ref)
    mid_ref[...] = _lru_out(x, u_ref, gate_ref, wout_ref)


def _lru_ffn(x2d, seq_len, lru, ffn, casts, kv=None):
    m, d = x2d.shape
    gain, w_in, conv_w, conv_b, w_r, w_i, b_r, b_i, lam, w_out = lru
    f_gain, w13, w2 = ffn
    w = w_out.shape[0]
    d_ff = w2.shape[0]
    n_tiles = m // LRU_TS
    tiles_per_seq = seq_len // LRU_TS
    last = n_tiles - 1
    prev = lambda i: jnp.maximum(i - 1, 0)
    jobs = [_cast_job(src, lead, n_tiles + 1) for src, lead in casts]
    row_vec = lambda v: v.reshape(1, -1)
    operands = [x2d, row_vec(gain), w_in, conv_w, row_vec(conv_b), w_r, w_i, row_vec(b_r),
                row_vec(b_i), row_vec(lam), w_out, row_vec(f_gain), w13, w2]
    in_specs = [pl.BlockSpec((LRU_TS, d), lambda i: (jnp.minimum(i, last), 0))]
    in_specs += [_const_spec(op.shape) for op in operands[1:]]
    out_specs = [pl.BlockSpec((LRU_TS, d), lambda i: (prev(i), 0))]
    out_shape = [jax.ShapeDtypeStruct((m, d), F32)]
    if kv is not None:
        kv_gain, w_kv, k_gain_row, pool = kv
        d_attn = w_kv.shape[1] // 2
        n_pairs = d_attn // LANES
        kv_ops = [row_vec(kv_gain), w_kv, k_gain_row, pool]
        operands += kv_ops
        in_specs += [_const_spec(op.shape) for op in kv_ops]
        out_specs += [
            pl.BlockSpec((LRU_TS, d_attn), lambda i: (prev(i), 0)),
            pl.BlockSpec((None, n_pairs, LRU_TS // ATT_T, LANES, ATT_T),
                         lambda i: (prev(i) // tiles_per_seq, 0, prev(i) % tiles_per_seq, 0, 0))]
        out_shape += [
            jax.ShapeDtypeStruct((m, d_attn), BF16),
            jax.ShapeDtypeStruct((m // seq_len, n_pairs, seq_len // ATT_T, LANES, ATT_T), BF16)]
    n_out = len(out_shape)
    out = pl.pallas_call(
        functools.partial(
            _ffn_body, n_in=len(in_specs), n_out=n_out, n_cast=len(jobs),
            core=functools.partial(_lru_ffn_kernel, tiles_per_seq=tiles_per_seq,
                                   with_kv=kv is not None)),
        grid=(n_tiles + 1,),
        in_specs=in_specs + [job[0] for job in jobs],
        out_specs=out_specs + [job[1] for job in jobs],
        out_shape=out_shape + [job[2] for job in jobs],
        scratch_shapes=[
            pltpu.VMEM((LRU_TS, d_ff), BF16),
            pltpu.VMEM((LRU_TS, d), F32),
            pltpu.VMEM((LRU_TS + SUBLANES, w), F32),
            pltpu.VMEM((LRU_TS, w), F32),
            pltpu.VMEM((LRU_TS, w), F32),
            pltpu.VMEM((LRU_TS, w), F32),
            pltpu.VMEM((1, w), F32),
        ],
        compiler_params=pltpu.CompilerParams(
            dimension_semantics=("arbitrary",), vmem_limit_bytes=VMEM_LIMIT),
        name="lru_ffn",
    )(*operands, *(src for src, _ in casts))
    return out[:n_out], out[n_out:]


def _head_rms_norm(t, gain_row, pool_ref):
    outs = []
    for g in range(t.shape[1] // POOL_W):
        tg = t[:, g * POOL_W:(g + 1) * POOL_W]
        ms = jnp.dot((tg * tg).astype(BF16), pool_ref[...], preferred_element_type=F32)
        outs.append(tg * lax.rsqrt(ms + EPS))
    return jnp.concatenate(outs, axis=1) * gain_row


def _head_pool_matrix():
    idx = np.arange(POOL_W) // HEAD_DIM
    return jnp.asarray((idx[:, None] == idx[None, :]).astype(np.float32) / HEAD_DIM, BF16)


def _attn_causal():
    k_pos = lax.broadcasted_iota(jnp.int32, (ATT_T, ATT_T), 0)
    q_pos = lax.broadcasted_iota(jnp.int32, (ATT_T, ATT_T), 1)
    return k_pos < q_pos


def _attn_halves(z, masked, causal):
    h = ATT_T // 2
    if masked:
        return [(z[:h], causal[:h], 0), (z[h:, h:], causal[h:, h:], h)]
    return [(z[:h], None, 0), (z[h:], None, 0)]


def _attn_widen(x, q0):
    return x if q0 == 0 else jnp.concatenate([jnp.zeros((x.shape[0], q0), x.dtype), x], axis=1)


def _attn_scores(specs, q_heads, k_ref):
    zs = []
    for head, kj, _, _ in specs:
        pair = head // 2
        kb = k_ref[pl.ds(pl.multiple_of(kj * ATT_T, ATT_T), ATT_T), pair * LANES:(pair + 1) * LANES]
        zs.append(lax.dot_general(kb, q_heads[head], (((1,), (1,)), ((), ())),
                                  preferred_element_type=F32))
    return zs


def _attn_sums(zs, specs, tri_ref, causal):
    exts = []
    for z, (_, _, masked, _) in zip(zs, specs):
        ext = []
        for zp, mask, q0 in _attn_halves(z, masked, causal):
            sp = jnp.maximum(zp, 0.0) + jnp.log(1.0 + jnp.exp(-jnp.abs(zp)))
            sp16 = (sp if mask is None else jnp.where(mask, sp, 0.0)).astype(BF16)
            ext.append(jnp.dot(tri_ref[...], _attn_widen(sp16, q0), preferred_element_type=F32))
        exts.append(ext)
    return exts


def _attn_weights(zs, exts, specs, causal, chain_valid=None):
    h = ATT_T // 2
    ws, carries = [], []
    for z, (ext_early, ext_late), (_, _, masked, carry) in zip(zs, exts, specs):
        if isinstance(carry, int):
            carry = carries[carry]
            if chain_valid is not None:
                carry = jnp.where(chain_valid, carry, LOG_W_NONE)
        total_late = jnp.broadcast_to(ext_late[0:1], (SUBLANES, ATT_T))
        carry_early = total_late if carry is None else carry + total_late
        w_halves = []
        for (zp, mask, q0), ext, c in zip(_attn_halves(z, masked, causal), (ext_early, ext_late),
                                          (carry_early, carry)):
            log_w = zp + ext[:, q0:]
            if c is not None:
                log_w = log_w + c[0:1, q0:]
            w = jnp.exp(log_w)
            if mask is not None:
                w = jnp.where(mask, w, 0.0)
            w_halves.append(_attn_widen(w.astype(BF16), q0))
        ws.append(jnp.concatenate(w_halves, axis=0))
        carries.append(carry_early + jnp.broadcast_to(ext_early[0:1], (SUBLANES, ATT_T)))
    return ws, carries


def _attn_values(ws, specs, vt_ref):
    return [jnp.dot(vt_ref[head // 2, kj], w, preferred_element_type=F32)
            for w, (head, kj, _, _) in zip(ws, specs)]


def _attn_ffn_kernel(x_ref, q_ref, k_ref, vt_ref, tri_ref, wo_ref, g_ref, w13_ref, w2_ref, y_ref,
                     act_ref, acc_ref, carry_ref, *, tiles_per_seq, n_tiles):
    t = ATT_T
    n_heads = acc_ref.shape[0]
    n_chunks = w2_ref.shape[0] // FF_CHUNK
    step = pl.program_id(0)
    qi = jnp.minimum(step, n_tiles - 1) % tiles_per_seq

    @pl.when(step == 0)
    def _():
        acc_ref[...] = jnp.zeros_like(acc_ref)

    feat = lax.broadcasted_iota(jnp.int32, (LANES, t), 0)
    o_prev = jnp.concatenate(
        [jnp.where(feat < HEAD_DIM, acc_ref[2 * pair], acc_ref[2 * pair + 1]).T.astype(BF16)
         for pair in range(n_heads // 2)], axis=1)

    lane = lax.broadcasted_iota(jnp.int32, (t, LANES), 1)
    causal = _attn_causal()
    q_heads = []
    for pair in range(n_heads // 2):
        q = q_ref[:, pair * LANES:(pair + 1) * LANES]
        q_heads += [jnp.where(lane < HEAD_DIM, q, jnp.zeros_like(q)),
                    jnp.where(lane >= HEAD_DIM, q, jnp.zeros_like(q))]

    has_prev = qi > 0
    kj_prev = jnp.maximum(qi - 1, 0)
    groups = [range(g, g + ATT_GROUP) for g in range(0, n_heads, ATT_GROUP)]
    n_up = 2 * len(groups) - 1
    up_slices = [range(n_chunks * s // n_up, n_chunks * (s + 1) // n_up) for s in range(n_up)]
    ffn_stage = iter(up_slices)
    x_in = xn = None
    results = {}
    for gidx, heads in enumerate(groups):
        specs = ([(head, qi, True, None) for head in heads]
                 + [(head, kj_prev, False, idx) for idx, head in enumerate(heads)])
        zs = _attn_scores(specs, q_heads, k_ref)
        if gidx == 0:
            x_in = x_ref[...] + jnp.dot(o_prev, wo_ref[...], preferred_element_type=F32)
            xn = _rms_norm_rows(x_in, g_ref[...]).astype(BF16)
        else:
            _swiglu_up(xn, w13_ref, act_ref, next(ffn_stage))
        exts = _attn_sums(zs, specs, tri_ref, causal)
        _swiglu_up(xn, w13_ref, act_ref, next(ffn_stage))
        if gidx == len(groups) - 1:
            y_ref[...] = x_in + 0.5 * jnp.dot(act_ref[...], w2_ref[...], preferred_element_type=F32)
        ws, carries = _attn_weights(zs, exts, specs, causal, chain_valid=has_prev)
        pvs = _attn_values(ws, specs, vt_ref)
        for idx, head in enumerate(heads):
            results[head] = (pvs[idx] + pvs[len(heads) + idx], carries[len(heads) + idx])
    for head, (pv, carry) in results.items():
        acc_ref[head] = pv
        carry_ref[head] = carry

    def max_carry(carries):
        top = carries[0]
        for carry in carries[1:]:
            top = jnp.maximum(top, carry)
        return jnp.max(top)

    def cond(state):
        kj, top = state
        return jnp.logical_and(kj >= 0, top > LOG_W_ZERO)

    def body(state):
        kj, _ = state
        specs = [(head, kj, False, carry_ref[head]) for head in range(n_heads)]
        zs = _attn_scores(specs, q_heads, k_ref)
        ws, carries = _attn_weights(zs, _attn_sums(zs, specs, tri_ref, causal), specs, causal)
        for head, pv in enumerate(_attn_values(ws, specs, vt_ref)):
            acc_ref[head] += pv
            carry_ref[head] = carries[head]
        return kj - 1, max_carry(carries)

    top = max_carry([carry for _, carry in results.values()])
    lax.while_loop(cond, body, (qi - 2, top))


def _suffix_sum_matrix():
    j = np.arange(ATT_T // 2)
    return jnp.asarray(-(j[None, :] >= j[:, None]).astype(np.float32), BF16)


def _attn_ffn(x2d, q2d, k, vt, tri, w_o, gain, w13, w2, casts):
    m, d = x2d.shape
    b, s, d_attn = k.shape
    d_ff = w2.shape[0]
    n_heads = d_attn // HEAD_DIM
    n_tiles = m // ATT_T
    tiles_per_seq = s // ATT_T
    cur = lambda i: jnp.minimum(i, n_tiles - 1)
    prev = lambda i: jnp.maximum(i - 1, 0)
    jobs = [_cast_job(src, lead, n_tiles + 1) for src, lead in casts]
    operands = [x2d, q2d, k, vt, tri, w_o, gain.reshape(1, d), w13, w2]
    in_specs = [
        pl.BlockSpec((ATT_T, d), lambda i: (prev(i), 0)),
        pl.BlockSpec((ATT_T, d_attn), lambda i: (cur(i), 0)),
        pl.BlockSpec((None, s, d_attn), lambda i: (cur(i) // tiles_per_seq, 0, 0),
                     pipeline_mode=pl.Buffered(1)),
        pl.BlockSpec((None,) + vt.shape[1:], lambda i: (cur(i) // tiles_per_seq, 0, 0, 0, 0),
                     pipeline_mode=pl.Buffered(1)),
    ] + [_const_spec(op.shape) for op in operands[4:]]
    out = pl.pallas_call(
        functools.partial(
            _ffn_body, n_in=len(in_specs), n_out=1, n_cast=len(jobs),
            core=functools.partial(_attn_ffn_kernel, tiles_per_seq=tiles_per_seq, n_tiles=n_tiles)),
        grid=(n_tiles + 1,),
        in_specs=in_specs + [job[0] for job in jobs],
        out_specs=[pl.BlockSpec((ATT_T, d), lambda i: (prev(i), 0))] + [job[1] for job in jobs],
        out_shape=[jax.ShapeDtypeStruct((m, d), F32)] + [job[2] for job in jobs],
        scratch_shapes=[
            pltpu.VMEM((ATT_T, d_ff), BF16),
            pltpu.VMEM((n_heads, LANES, ATT_T), F32),
            pltpu.VMEM((n_heads, SUBLANES, ATT_T), F32),
        ],
        compiler_params=pltpu.CompilerParams(
            dimension_semantics=("arbitrary",), vmem_limit_bytes=VMEM_LIMIT),
        name="attn_ffn",
    )(*operands, *(src for src, _ in casts))
    return out[:1], out[1:]


def kernel(x, ffn1_norm, ffn1_w13, ffn1_w2, mix_norm, a_w_in, a_conv_w, a_conv_b, a_w_r, a_b_r,
           a_w_i, a_b_i, a_lambda, a_w_out, kv_norm, w_kv, k_norm, b_w_q, q_norm, b_w_o,
           ffn2_norm, ffn2_w13, ffn2_w2):
    b, s, d = x.shape
    depth = ffn1_norm.shape[0]
    n_a = a_w_in.shape[0]
    m = b * s
    pool = _head_pool_matrix()
    tri = _suffix_sum_matrix()

    assert 1 <= n_a < depth and s % FFN_TM == 0 and FFN_TM % ATT_T == 0

    def ffn_sources(l, which):
        w13, w2 = (ffn1_w13, ffn1_w2) if which == 1 else (ffn2_w13, ffn2_w2)
        srcs = {"w13": (w13, l), "w2": (w2, l)}
        if which == 2 and l == n_a - 1:
            srcs["w_kv"] = (w_kv, None)
        if l >= n_a:
            srcs["w_q" if which == 1 else "w_o"] = ((b_w_q if which == 1 else b_w_o), l - n_a)
        return srcs

    def mixer_sources(l):
        if l >= n_a:
            return {}
        lru_w = a_w_out.shape[1]
        return {"w_in": (a_w_in, l), "w_r": (a_w_r.reshape(n_a, lru_w, -1), l),
                "w_i": (a_w_i.reshape(n_a, lru_w, -1), l), "w_out": (a_w_out, l)}

    order = [(l, which) for l in range(depth) for which in (1, 2)]
    ready = {(0, 1, name): (src[lead] if lead is not None else src).astype(BF16)
             for name, (src, lead) in ffn_sources(0, 1).items()}

    x = x.reshape(m, d)
    k_sh = vt_sh = q = None
    for idx, (l, which) in enumerate(order):
        jobs = {}
        if which == 1:
            jobs.update({("mix", l, name): v for name, v in mixer_sources(l).items()})
        if idx + 1 < len(order):
            nxt = order[idx + 1]
            jobs.update({(*nxt, name): v for name, v in ffn_sources(*nxt).items()})
        own = {name: ready.pop((l, which, name)) for name in ffn_sources(l, which)}
        gain = (ffn1_norm if which == 1 else ffn2_norm)[l]
        args = (x, gain, own["w13"], own["w2"], list(jobs.values()))
        if which == 1 and l >= n_a:
            (x, q), cast = _ffn_q(*args, mix_norm[l], own["w_q"],
                                  jnp.tile(q_norm[l - n_a], N_HEADS).reshape(1, -1), pool)
        elif which == 2 and l >= n_a:
            (x,), cast = _attn_ffn(x, q, k_sh, vt_sh, tri, own["w_o"], gain, own["w13"], own["w2"],
                                   list(jobs.values()))
        elif which == 2:
            mix = {name: ready.pop(("mix", l, name)) for name in mixer_sources(l)}
            lru = (mix_norm[l], mix["w_in"], a_conv_w[l], a_conv_b[l], mix["w_r"], mix["w_i"],
                   a_b_r[l], a_b_i[l], a_lambda[l], mix["w_out"])
            kv = None
            if l == n_a - 1:
                kv = (kv_norm, own["w_kv"], jnp.tile(k_norm, N_HEADS).reshape(1, -1), pool)
            outs, cast = _lru_ffn(x, s, lru, (gain, own["w13"], own["w2"]), list(jobs.values()), kv)
            x = outs[0]
            if kv is not None:
                k_sh, vt_sh = outs[1].reshape(b, s, -1), outs[2]
        else:
            (x,), cast = _ffn(*args)
        ready.update(zip(jobs.keys(), cast))
    return x.reshape(b, s, d)
```

```python
import functools

import jax
import jax.numpy as jnp
import numpy as np
from jax import lax
from jax.experimental import pallas as pl
from jax.experimental.pallas import tpu as pltpu

EPS = 1e-6
N_HEADS = 16
HEAD_DIM = 64
CONV_W = 4
LRU_C = 8.0
LANES = 128
SUBLANES = 8
BF16_ROWS = 16
VMEM_LIMIT = 56 * 1024 * 1024

FFN_TM = 1024
FF_CHUNK = 256
LRU_TS = 256
ATT_T = 256
ATT_GROUP = 4
POOL_W = 256
LOG_W_ZERO = -104.0
LOG_W_NONE = -1e30

F32 = jnp.float32
BF16 = jnp.bfloat16


def _rms_norm_rows(x, gain_row):
    ms = jnp.mean(x * x, axis=-1, keepdims=True)
    return x * lax.rsqrt(ms + EPS) * gain_row


def _sigmoid(x):
    return 1.0 / (1.0 + jnp.exp(-x))


def _const_spec(shape):
    nd = len(shape)
    return pl.BlockSpec(shape, lambda *_: (0,) * nd, pipeline_mode=pl.Buffered(1))


def _swiglu_up(xn, w13_ref, act_ref, chunks):
    d_ff = act_ref.shape[1]
    for c in chunks:
        lo = c * FF_CHUNK
        gate = jnp.dot(xn, w13_ref[:, lo:lo + FF_CHUNK], preferred_element_type=F32)
        up = jnp.dot(xn, w13_ref[:, d_ff + lo:d_ff + lo + FF_CHUNK], preferred_element_type=F32)
        act_ref[:, lo:lo + FF_CHUNK] = (gate * _sigmoid(gate) * up).astype(BF16)


def _half_swiglu_step(x, g_ref, w13_ref, w2_ref, act_ref):
    xn = _rms_norm_rows(x, g_ref[...]).astype(BF16)
    _swiglu_up(xn, w13_ref, act_ref, range(w2_ref.shape[0] // FF_CHUNK))
    return x + 0.5 * jnp.dot(act_ref[...], w2_ref[...], preferred_element_type=F32)


def _ffn_kernel(x_ref, g_ref, w13_ref, w2_ref, y_ref, act_ref):
    y_ref[...] = _half_swiglu_step(x_ref[...], g_ref, w13_ref, w2_ref, act_ref)


def _kv_tail(y, ng_ref, wkv_ref, kg_ref, pool_ref, k_ref, vt_ref):
    d_attn = k_ref.shape[1]
    h = _rms_norm_rows(y, ng_ref[...]).astype(BF16)
    kv = jnp.dot(h, wkv_ref[...], preferred_element_type=F32)
    k_ref[...] = _head_rms_norm(kv[:, :d_attn], kg_ref[...], pool_ref).astype(BF16)
    for pair in range(vt_ref.shape[0]):
        for blk in range(vt_ref.shape[1]):
            v_blk = kv[blk * ATT_T:(blk + 1) * ATT_T,
                       d_attn + pair * LANES:d_attn + (pair + 1) * LANES]
            vt_ref[pair, blk] = v_blk.T.astype(BF16)


def _ffn_q_kernel(x_ref, g_ref, w13_ref, w2_ref, ng_ref, wq_ref, qg_ref, pool_ref,
                  y_ref, q_ref, act_ref):
    y = _half_swiglu_step(x_ref[...], g_ref, w13_ref, w2_ref, act_ref)
    y_ref[...] = y
    h = _rms_norm_rows(y, ng_ref[...]).astype(BF16)
    q = jnp.dot(h, wq_ref[...], preferred_element_type=F32)
    q_ref[...] = (_head_rms_norm(q, qg_ref[...], pool_ref) * (HEAD_DIM ** -0.5)).astype(BF16)


def _cast_job(src, lead, n_steps):
    rows, cols = src.shape[-2:]
    n_chunks = max(k for k in range(1, n_steps + 1)
                   if rows % k == 0 and (rows // k) % BF16_ROWS == 0)
    chunk = rows // n_chunks
    if lead is None:
        in_spec = pl.BlockSpec((chunk, cols), lambda i: (jnp.minimum(i, n_chunks - 1), 0))
    else:
        in_spec = pl.BlockSpec((None, chunk, cols),
                               lambda i: (lead, jnp.minimum(i, n_chunks - 1), 0))
    out_spec = pl.BlockSpec((chunk, cols), lambda i: (jnp.minimum(i, n_chunks - 1), 0))
    return in_spec, out_spec, jax.ShapeDtypeStruct((rows, cols), BF16)


def _ffn_body(*refs, core, n_in, n_out, n_cast):
    ins, rest = refs[:n_in], refs[n_in:]
    cast_in, rest = rest[:n_cast], rest[n_cast:]
    outs, rest = rest[:n_out], rest[n_out:]
    cast_out, scratch = rest[:n_cast], rest[n_cast:]
    for src_ref, dst_ref in zip(cast_in, cast_out):
        dst_ref[...] = src_ref[...].astype(BF16)
    core(*ins, *outs, *scratch)


def _ffn_call(body, name, x2d, gain, w13, w2, casts, extra_in=(), extra_in_specs=(),
              extra_out_shapes=(), extra_out_specs=()):
    m, d = x2d.shape
    d_ff = w2.shape[0]
    n_steps = m // FFN_TM
    row_spec = pl.BlockSpec((FFN_TM, d), lambda i: (i, 0))
    jobs = [_cast_job(src, lead, n_steps) for src, lead in casts]
    in_specs = [row_spec, _const_spec((1, d)), _const_spec((d, 2 * d_ff)),
                _const_spec((d_ff, d)), *extra_in_specs]
    n_out = 1 + len(extra_out_shapes)
    out = pl.pallas_call(
        functools.partial(_ffn_body, core=body, n_in=len(in_specs), n_out=n_out,
                          n_cast=len(jobs)),
        grid=(n_steps,),
        in_specs=in_specs + [job[0] for job in jobs],
        out_specs=[row_spec, *extra_out_specs] + [job[1] for job in jobs],
        out_shape=[jax.ShapeDtypeStruct((m, d), F32), *extra_out_shapes] + [job[2] for job in jobs],
        scratch_shapes=[pltpu.VMEM((FFN_TM, d_ff), BF16)],
        compiler_params=pltpu.CompilerParams(
            dimension_semantics=("arbitrary",), vmem_limit_bytes=VMEM_LIMIT),
        name=name,
    )(x2d, gain.reshape(1, d), w13, w2, *extra_in, *(src for src, _ in casts))
    return out[:n_out], out[n_out:]


def _ffn(x2d, gain, w13, w2, casts):
    return _ffn_call(_ffn_kernel, "ffn", x2d, gain, w13, w2, casts)


def _ffn_q(x2d, gain, w13, w2, casts, q_in_gain, w_q, q_gain_row, pool):
    m, d = x2d.shape
    d_attn = w_q.shape[1]
    return _ffn_call(
        _ffn_q_kernel, "ffn_q", x2d, gain, w13, w2, casts,
        extra_in=(q_in_gain.reshape(1, d), w_q, q_gain_row, pool),
        extra_in_specs=(_const_spec((1, d)), _const_spec((d, d_attn)),
                        _const_spec((1, d_attn)), _const_spec((POOL_W, POOL_W))),
        extra_out_shapes=(jax.ShapeDtypeStruct((m, d_attn), BF16),),
        extra_out_specs=(pl.BlockSpec((FFN_TM, d_attn), lambda i: (i, 0)),))


def _gelu_tanh(x):
    c = np.float32(np.sqrt(2.0 / np.pi))
    return x * (0.5 * (1.0 + jnp.tanh(c * (x + 0.044715 * (x * x * x)))))


def _lru_in(x, g_ref, win_ref):
    h = _rms_norm_rows(x, g_ref[...]).astype(BF16)
    return jnp.dot(h, win_ref[...], preferred_element_type=F32)


def _lru_gates(proj, cw_ref, cb_ref, wr_ref, wi_ref, br_ref, bi_ref, lam_ref,
               rec_ref, a_ref, u_ref, gate_ref):
    ts, w = gate_ref.shape
    bw = wr_ref.shape[1]
    n_blocks = w // bw
    gate_ref[...] = _gelu_tanh(proj[:, :w])
    rec = proj[:, w:]
    rec_ref[SUBLANES:SUBLANES + ts, :] = rec
    xc = cb_ref[...] + cw_ref[CONV_W - 1:CONV_W, :] * rec
    for k in range(CONV_W - 1):
        back = CONV_W - 1 - k
        xc = xc + cw_ref[k:k + 1, :] * rec_ref[SUBLANES - back:SUBLANES - back + ts, :]
    rec_ref[0:SUBLANES, :] = rec_ref[ts:ts + SUBLANES, :]

    xcb = xc.astype(BF16)
    sp_lam = jnp.maximum(-lam_ref[...], 0.0) + jnp.log(1.0 + jnp.exp(-jnp.abs(lam_ref[...])))
    for n in range(n_blocks):
        sl = slice(n * bw, (n + 1) * bw)
        r = _sigmoid(jnp.dot(xcb[:, sl], wr_ref[sl, :], preferred_element_type=F32) + br_ref[:, sl])
        i = _sigmoid(jnp.dot(xcb[:, sl], wi_ref[sl, :], preferred_element_type=F32) + bi_ref[:, sl])
        log_a = (-LRU_C) * r * sp_lam[:, sl]
        a = jnp.exp(log_a)
        a_ref[:, sl] = a
        u_ref[:, sl] = jnp.sqrt(-jnp.tanh(log_a) * (1.0 + a * a)) * (i * xc[:, sl])


def _lru_scan(a_ref, u_ref, hcar_ref):
    ts, w = u_ref.shape
    row = lax.broadcasted_iota(jnp.int32, (SUBLANES, w), 0)
    hprev = hcar_ref[...]
    for gidx in range(ts // SUBLANES):
        rows = slice(gidx * SUBLANES, (gidx + 1) * SUBLANES)
        a = a_ref[rows, :]
        u = u_ref[rows, :]
        for sh in (1, 2, 4):
            keep = row >= sh
            u_new = jnp.where(keep, u + a * pltpu.roll(u, sh, 0), u)
            a = jnp.where(keep, a * pltpu.roll(a, sh, 0), a)
            u = u_new
        hs = u + a * hprev
        u_ref[rows, :] = hs
        hprev = hs[SUBLANES - 1:SUBLANES, :]
    hcar_ref[...] = hprev


def _lru_out(x, u_ref, gate_ref, wout_ref):
    y = (u_ref[...] * gate_ref[...]).astype(BF16)
    return x + jnp.dot(y, wout_ref[...], preferred_element_type=F32)


N_LRU_IN = 10
N_FFN_IN = 3


def _lru_ffn_kernel(*refs, tiles_per_seq, with_kv):
    x_ref, refs = refs[0], refs[1:]
    lru_in, refs = refs[:N_LRU_IN], refs[N_LRU_IN:]
    ffn_in, refs = refs[:N_FFN_IN], refs[N_FFN_IN:]
    if with_kv:
        kv_in, refs = refs[:4], refs[4:]
        (y_ref, k_ref, vt_ref), refs = refs[:3], refs[3:]
    else:
        y_ref, refs = refs[0], refs[1:]
    act_ref, mid_ref, rec_ref, a_ref, u_ref, gate_ref, hcar_ref = refs
    step = pl.program_id(0)

    @pl.when(step == 0)
    def _():
        mid_ref[...] = jnp.zeros_like(mid_ref)

    @pl.when(step % tiles_per_seq == 0)
    def _():
        rec_ref[0:SUBLANES, :] = jnp.zeros((SUBLANES, rec_ref.shape[1]), F32)
        hcar_ref[...] = jnp.zeros_like(hcar_ref)

    mg_ref, win_ref, cw_ref, cb_ref, wr_ref, wi_ref, br_ref, bi_ref, lam_ref, wout_ref = lru_in
    g_ref, w13_ref, w2_ref = ffn_in
    n_chunks = w2_ref.shape[0] // FF_CHUNK
    x = x_ref[...]
    x_prev = mid_ref[...]
    cut = [-(-n_chunks * s // 3) for s in range(4)]
    xn = _rms_norm_rows(x_prev, g_ref[...]).astype(BF16)
    _swiglu_up(xn, w13_ref, act_ref, range(cut[0], cut[1]))
    proj = _lru_in(x, mg_ref, win_ref)
    _swiglu_up(xn, w13_ref, act_ref, range(cut[1], cut[2]))
    _lru_gates(proj, cw_ref, cb_ref, wr_ref, wi_ref, br_ref, bi_ref, lam_ref,
               rec_ref, a_ref, u_ref, gate_ref)
    _swiglu_up(xn, w13_ref, act_ref, range(cut[2], cut[3]))
    y = x_prev + 0.5 * jnp.dot(act_ref[...], w2_ref[...], preferred_element_type=F32)
    y_ref[...] = y
    _lru_scan(a_ref, u_ref, hcar_ref)
    mid_ref[...] = _lru_out(x, u_ref, gate_ref, wout_ref)
    if with_kv:
        _kv_tail(y, *kv_in, k_ref, vt_ref)


def _lru_ffn(x2d, seq_len, lru, ffn, casts, kv=None):
    m, d = x2d.shape
    gain, w_in, conv_w, conv_b, w_r, w_i, b_r, b_i, lam, w_out = lru
    f_gain, w13, w2 = ffn
    w = w_out.shape[0]
    d_ff = w2.shape[0]
    n_tiles = m // LRU_TS
    tiles_per_seq = seq_len // LRU_TS
    last = n_tiles - 1
    prev = lambda i: jnp.maximum(i - 1, 0)
    jobs = [_cast_job(src, lead, n_tiles + 1) for src, lead in casts]
    row_vec = lambda v: v.reshape(1, -1)
    operands = [x2d, row_vec(gain), w_in, conv_w, row_vec(conv_b), w_r, w_i, row_vec(b_r),
                row_vec(b_i), row_vec(lam), w_out, row_vec(f_gain), w13, w2]
    in_specs = [pl.BlockSpec((LRU_TS, d), lambda i: (jnp.minimum(i, last), 0))]
    in_specs += [_const_spec(op.shape) for op in operands[1:]]
    out_specs = [pl.BlockSpec((LRU_TS, d), lambda i: (prev(i), 0))]
    out_shape = [jax.ShapeDtypeStruct((m, d), F32)]
    if kv is not None:
        kv_gain, w_kv, k_gain_row, pool = kv
        d_attn = w_kv.shape[1] // 2
        n_pairs = d_attn // LANES
        kv_ops = [row_vec(kv_gain), w_kv, k_gain_row, pool]
        operands += kv_ops
        in_specs += [_const_spec(op.shape) for op in kv_ops]
        out_specs += [
            pl.BlockSpec((LRU_TS, d_attn), lambda i: (prev(i), 0)),
            pl.BlockSpec((None, n_pairs, LRU_TS // ATT_T, LANES, ATT_T),
                         lambda i: (prev(i) // tiles_per_seq, 0, prev(i) % tiles_per_seq, 0, 0))]
        out_shape += [
            jax.ShapeDtypeStruct((m, d_attn), BF16),
            jax.ShapeDtypeStruct((m // seq_len, n_pairs, seq_len // ATT_T, LANES, ATT_T), BF16)]
    n_out = len(out_shape)
    out = pl.pallas_call(
        functools.partial(
            _ffn_body, n_in=len(in_specs), n_out=n_out, n_cast=len(jobs),
            core=functools.partial(_lru_ffn_kernel, tiles_per_seq=tiles_per_seq,
                                   with_kv=kv is not None)),
        grid=(n_tiles + 1,),
        in_specs=in_specs + [job[0] for job in jobs],
        out_specs=out_specs + [job[1] for job in jobs],
        out_shape=out_shape + [job[2] for job in jobs],
        scratch_shapes=[
            pltpu.VMEM((LRU_TS, d_ff), BF16),
            pltpu.VMEM((LRU_TS, d), F32),
            pltpu.VMEM((LRU_TS + SUBLANES, w), F32),
            pltpu.VMEM((LRU_TS, w), F32),
            pltpu.VMEM((LRU_TS, w), F32),
            pltpu.VMEM((LRU_TS, w), F32),
            pltpu.VMEM((1, w), F32),
        ],
        compiler_params=pltpu.CompilerParams(
            dimension_semantics=("arbitrary",), vmem_limit_bytes=VMEM_LIMIT),
        name="lru_ffn",
    )(*operands, *(src for src, _ in casts))
    return out[:n_out], out[n_out:]


def _head_rms_norm(t, gain_row, pool_ref):
    outs = []
    for g in range(t.shape[1] // POOL_W):
        tg = t[:, g * POOL_W:(g + 1) * POOL_W]
        ms = jnp.dot((tg * tg).astype(BF16), pool_ref[...], preferred_element_type=F32)
        outs.append(tg * lax.rsqrt(ms + EPS))
    return jnp.concatenate(outs, axis=1) * gain_row


def _head_pool_matrix():
    idx = np.arange(POOL_W) // HEAD_DIM
    return jnp.asarray((idx[:, None] == idx[None, :]).astype(np.float32) / HEAD_DIM, BF16)


def _attn_causal():
    k_pos = lax.broadcasted_iota(jnp.int32, (ATT_T, ATT_T), 0)
    q_pos = lax.broadcasted_iota(jnp.int32, (ATT_T, ATT_T), 1)
    return k_pos < q_pos


def _attn_halves(z, masked, causal):
    h = ATT_T // 2
    if masked:
        return [(z[:h], causal[:h], 0), (z[h:, h:], causal[h:, h:], h)]
    return [(z[:h], None, 0), (z[h:], None, 0)]


def _attn_widen(x, q0):
    return x if q0 == 0 else jnp.concatenate([jnp.zeros((x.shape[0], q0), x.dtype), x], axis=1)


def _attn_scores(specs, q_heads, k_ref):
    zs = []
    for head, kj, _, _ in specs:
        pair = head // 2
        kb = k_ref[pl.ds(pl.multiple_of(kj * ATT_T, ATT_T), ATT_T), pair * LANES:(pair + 1) * LANES]
        zs.append(lax.dot_general(kb, q_heads[head], (((1,), (1,)), ((), ())),
                                  preferred_element_type=F32))
    return zs


def _attn_sums(zs, specs, tri_ref, causal):
    exts = []
    for z, (_, _, masked, _) in zip(zs, specs):
        ext = []
        for zp, mask, q0 in _attn_halves(z, masked, causal):
            sp = jnp.maximum(zp, 0.0) + jnp.log(1.0 + jnp.exp(-jnp.abs(zp)))
            sp16 = (sp if mask is None else jnp.where(mask, sp, 0.0)).astype(BF16)
            ext.append(jnp.dot(tri_ref[...], _attn_widen(sp16, q0), preferred_element_type=F32))
        exts.append(ext)
    return exts


def _attn_weights(zs, exts, specs, causal, chain_valid=None):
    h = ATT_T // 2
    ws, carries = [], []
    for z, (ext_early, ext_late), (_, _, masked, carry) in zip(zs, exts, specs):
        if isinstance(carry, int):
            carry = carries[carry]
            if chain_valid is not None:
                carry = jnp.where(chain_valid, carry, LOG_W_NONE)
        total_late = jnp.broadcast_to(ext_late[0:1], (SUBLANES, ATT_T))
        carry_early = total_late if carry is None else carry + total_late
        w_halves = []
        for (zp, mask, q0), ext, c in zip(_attn_halves(z, masked, causal), (ext_early, ext_late),
                                          (carry_early, carry)):
            log_w = zp + ext[:, q0:]
            if c is not None:
                log_w = log_w + c[0:1, q0:]
            w = jnp.exp(log_w)
            if mask is not None:
                w = jnp.where(mask, w, 0.0)
            w_halves.append(_attn_widen(w.astype(BF16), q0))
        ws.append(jnp.concatenate(w_halves, axis=0))
        carries.append(carry_early + jnp.broadcast_to(ext_early[0:1], (SUBLANES, ATT_T)))
    return ws, carries


def _attn_values(ws, specs, vt_ref):
    return [jnp.dot(vt_ref[head // 2, kj], w, preferred_element_type=F32)
            for w, (head, kj, _, _) in zip(ws, specs)]


def _attn_ffn_kernel(x_ref, q_ref, k_ref, vt_ref, tri_ref, wo_ref, g_ref, w13_ref, w2_ref, y_ref,
                     act_ref, acc_ref, carry_ref, *, tiles_per_seq, n_tiles):
    t = ATT_T
    n_heads = acc_ref.shape[0]
    n_chunks = w2_ref.shape[0] // FF_CHUNK
    step = pl.program_id(0)
    qi = jnp.minimum(step, n_tiles - 1) % tiles_per_seq

    @pl.when(step == 0)
    def _():
        acc_ref[...] = jnp.zeros_like(acc_ref)

    feat = lax.broadcasted_iota(jnp.int32, (LANES, t), 0)
    o_prev = jnp.concatenate(
        [jnp.where(feat < HEAD_DIM, acc_ref[2 * pair], acc_ref[2 * pair + 1]).T.astype(BF16)
         for pair in range(n_heads // 2)], axis=1)

    lane = lax.broadcasted_iota(jnp.int32, (t, LANES), 1)
    causal = _attn_causal()
    q_heads = []
    for pair in range(n_heads // 2):
        q = q_ref[:, pair * LANES:(pair + 1) * LANES]
        q_heads += [jnp.where(lane < HEAD_DIM, q, jnp.zeros_like(q)),
                    jnp.where(lane >= HEAD_DIM, q, jnp.zeros_like(q))]

    has_prev = qi > 0
    kj_prev = jnp.maximum(qi - 1, 0)
    groups = [range(g, g + ATT_GROUP) for g in range(0, n_heads, ATT_GROUP)]
    n_up = 2 * len(groups) - 1
    shares = [1] + [3, 1] * (len(groups) - 1)
    shares[-1] = 0
    cuts = [-(-n_chunks * sum(shares[:s]) // sum(shares)) for s in range(n_up + 1)]
    up_slices = [range(cuts[s], cuts[s + 1]) for s in range(n_up)]
    ffn_stage = iter(up_slices)
    x_in = xn = None
    results = {}
    for gidx, heads in enumerate(groups):
        specs = ([(head, qi, True, None) for head in heads]
                 + [(head, kj_prev, False, idx) for idx, head in enumerate(heads)])
        zs = _attn_scores(specs, q_heads, k_ref)
        if gidx == 0:
            x_in = x_ref[...] + jnp.dot(o_prev, wo_ref[...], preferred_element_type=F32)
            xn = _rms_norm_rows(x_in, g_ref[...]).astype(BF16)
        else:
            _swiglu_up(xn, w13_ref, act_ref, next(ffn_stage))
        exts = _attn_sums(zs, specs, tri_ref, causal)
        _swiglu_up(xn, w13_ref, act_ref, next(ffn_stage))
        if gidx == len(groups) - 1:
            y_ref[...] = x_in + 0.5 * jnp.dot(act_ref[...], w2_ref[...], preferred_element_type=F32)
        ws, carries = _attn_weights(zs, exts, specs, causal, chain_valid=has_prev)
        pvs = _attn_values(ws, specs, vt_ref)
        for idx, head in enumerate(heads):
            results[head] = (pvs[idx] + pvs[len(heads) + idx], carries[len(heads) + idx])
    for head, (pv, carry) in results.items():
        acc_ref[head] = pv
        carry_ref[head] = carry

    def max_carry(carries):
        top = carries[0]
        for carry in carries[1:]:
            top = jnp.maximum(top, carry)
        return jnp.max(top)

    def cond(state):
        kj, top = state
        return jnp.logical_and(kj >= 0, top > LOG_W_ZERO)

    def body(state):
        kj, _ = state
        specs = [(head, kj, False, carry_ref[head]) for head in range(n_heads)]
        zs = _attn_scores(specs, q_heads, k_ref)
        ws, carries = _attn_weights(zs, _attn_sums(zs, specs, tri_ref, causal), specs, causal)
        for head, pv in enumerate(_attn_values(ws, specs, vt_ref)):
            acc_ref[head] += pv
            carry_ref[head] = carries[head]
        return kj - 1, max_carry(carries)

    top = max_carry([carry for _, carry in results.values()])
    lax.while_loop(cond, body, (qi - 2, top))


def _suffix_sum_matrix():
    j = np.arange(ATT_T // 2)
    return jnp.asarray(-(j[None, :] >= j[:, None]).astype(np.float32), BF16)


def _attn_ffn(x2d, q2d, k, vt, tri, w_o, gain, w13, w2, casts):
    m, d = x2d.shape
    b, s, d_attn = k.shape
    d_ff = w2.shape[0]
    n_heads = d_attn // HEAD_DIM
    n_tiles = m // ATT_T
    tiles_per_seq = s // ATT_T
    cur = lambda i: jnp.minimum(i, n_tiles - 1)
    prev = lambda i: jnp.maximum(i - 1, 0)
    jobs = [_cast_job(src, lead, n_tiles + 1) for src, lead in casts]
    operands = [x2d, q2d, k, vt, tri, w_o, gain.reshape(1, d), w13, w2]
    in_specs = [
        pl.BlockSpec((ATT_T, d), lambda i: (prev(i), 0)),
        pl.BlockSpec((ATT_T, d_attn), lambda i: (cur(i), 0)),
        pl.BlockSpec((None, s, d_attn), lambda i: (cur(i) // tiles_per_seq, 0, 0),
                     pipeline_mode=pl.Buffered(1)),
        pl.BlockSpec((None,) + vt.shape[1:], lambda i: (cur(i) // tiles_per_seq, 0, 0, 0, 0),
                     pipeline_mode=pl.Buffered(1)),
    ] + [_const_spec(op.shape) for op in operands[4:]]
    out = pl.pallas_call(
        functools.partial(
            _ffn_body, n_in=len(in_specs), n_out=1, n_cast=len(jobs),
            core=functools.partial(_attn_ffn_kernel, tiles_per_seq=tiles_per_seq, n_tiles=n_tiles)),
        grid=(n_tiles + 1,),
        in_specs=in_specs + [job[0] for job in jobs],
        out_specs=[pl.BlockSpec((ATT_T, d), lambda i: (prev(i), 0))] + [job[1] for job in jobs],
        out_shape=[jax.ShapeDtypeStruct((m, d), F32)] + [job[2] for job in jobs],
        scratch_shapes=[
            pltpu.VMEM((ATT_T, d_ff), BF16),
            pltpu.VMEM((n_heads, LANES, ATT_T), F32),
            pltpu.VMEM((n_heads, SUBLANES, ATT_T), F32),
        ],
        compiler_params=pltpu.CompilerParams(
            dimension_semantics=("arbitrary",), vmem_limit_bytes=VMEM_LIMIT),
        name="attn_ffn",
    )(*operands, *(src for src, _ in casts))
    return out[:1], out[1:]


def kernel(x, ffn1_norm, ffn1_w13, ffn1_w2, mix_norm, a_w_in, a_conv_w, a_conv_b, a_w_r, a_b_r,
           a_w_i, a_b_i, a_lambda, a_w_out, kv_norm, w_kv, k_norm, b_w_q, q_norm, b_w_o,
           ffn2_norm, ffn2_w13, ffn2_w2):
    b, s, d = x.shape
    depth = ffn1_norm.shape[0]
    n_a = a_w_in.shape[0]
    m = b * s
    pool = _head_pool_matrix()
    tri = _suffix_sum_matrix()

    assert 1 <= n_a < depth and s % FFN_TM == 0 and FFN_TM % ATT_T == 0

    def ffn_sources(l, which):
        w13, w2 = (ffn1_w13, ffn1_w2) if which == 1 else (ffn2_w13, ffn2_w2)
        srcs = {"w13": (w13, l), "w2": (w2, l)}
        if which == 2 and l == n_a - 1:
            srcs["w_kv"] = (w_kv, None)
        if l >= n_a:
            srcs["w_q" if which == 1 else "w_o"] = ((b_w_q if which == 1 else b_w_o), l - n_a)
        return srcs

    def mixer_sources(l):
        if l >= n_a:
            return {}
        lru_w = a_w_out.shape[1]
        return {"w_in": (a_w_in, l), "w_r": (a_w_r.reshape(n_a, lru_w, -1), l),
                "w_i": (a_w_i.reshape(n_a, lru_w, -1), l), "w_out": (a_w_out, l)}

    order = [(l, which) for l in range(depth) for which in (1, 2)]
    ready = {(0, 1, name): (src[lead] if lead is not None else src).astype(BF16)
             for name, (src, lead) in ffn_sources(0, 1).items()}

    x = x.reshape(m, d)
    k_sh = vt_sh = q = None
    for idx, (l, which) in enumerate(order):
        jobs = {}
        if which == 1:
            jobs.update({("mix", l, name): v for name, v in mixer_sources(l).items()})
        if idx + 1 < len(order):
            nxt = order[idx + 1]
            jobs.update({(*nxt, name): v for name, v in ffn_sources(*nxt).items()})
        own = {name: ready.pop((l, which, name)) for name in ffn_sources(l, which)}
        gain = (ffn1_norm if which == 1 else ffn2_norm)[l]
        args = (x, gain, own["w13"], own["w2"], list(jobs.values()))
        if which == 1 and l >= n_a:
            (x, q), cast = _ffn_q(*args, mix_norm[l], own["w_q"],
                                  jnp.tile(q_norm[l - n_a], N_HEADS).reshape(1, -1), pool)
        elif which == 2 and l >= n_a:
            (x,), cast = _attn_ffn(x, q, k_sh, vt_sh, tri, own["w_o"], gain, own["w13"], own["w2"],
                                   list(jobs.values()))
        elif which == 2:
            mix = {name: ready.pop(("mix", l, name)) for name in mixer_sources(l)}
            lru = (mix_norm[l], mix["w_in"], a_conv_w[l], a_conv_b[l], mix["w_r"], mix["w_i"],
                   a_b_r[l], a_b_i[l], a_lambda[l], mix["w_out"])
            kv = None
            if l == n_a - 1:
                kv = (kv_norm, own["w_kv"], jnp.tile(k_norm, N_HEADS).reshape(1, -1), pool)
            outs, cast = _lru_ffn(x, s, lru, (gain, own["w13"], own["w2"]), list(jobs.values()), kv)
            x = outs[0]
            if kv is not None:
                k_sh, vt_sh = outs[1].reshape(b, s, -1), outs[2]
        else:
            (x,), cast = _ffn(*args)
        ready.update(zip(jobs.keys(), cast))
    return x.reshape(b, s, d)
```

```python
import functools

import jax
import jax.numpy as jnp
import numpy as np
from jax import lax
from jax.experimental import pallas as pl
from jax.experimental.pallas import tpu as pltpu

EPS = 1e-6
N_HEADS = 16
HEAD_DIM = 64
CONV_W = 4
LRU_C = 8.0
LANES = 128
SUBLANES = 8
BF16_ROWS = 16
VMEM_LIMIT = 56 * 1024 * 1024

FFN_TM = 1024
FF_CHUNK = 256
LRU_TS = 256
ATT_T = 256
ATT_GROUP = 4
ATT_FFN_SHARES = (2, 3, 0, 4, 0, 2, 0)
POOL_W = 256
LOG_W_ZERO = -104.0
LOG_W_NONE = -1e30

F32 = jnp.float32
BF16 = jnp.bfloat16


def _rms_norm_rows(x, gain_row):
    ms = jnp.mean(x * x, axis=-1, keepdims=True)
    return x * lax.rsqrt(ms + EPS) * gain_row


def _sigmoid(x):
    return 1.0 / (1.0 + jnp.exp(-x))


def _const_spec(shape):
    nd = len(shape)
    return pl.BlockSpec(shape, lambda *_: (0,) * nd, pipeline_mode=pl.Buffered(1))


def _swiglu_up(xn, w13_ref, act_ref, chunks):
    d_ff = act_ref.shape[1]
    for c in chunks:
        lo = c * FF_CHUNK
        gate = jnp.dot(xn, w13_ref[:, lo:lo + FF_CHUNK], preferred_element_type=F32)
        up = jnp.dot(xn, w13_ref[:, d_ff + lo:d_ff + lo + FF_CHUNK], preferred_element_type=F32)
        act_ref[:, lo:lo + FF_CHUNK] = (gate * _sigmoid(gate) * up).astype(BF16)


def _half_swiglu_step(x, g_ref, w13_ref, w2_ref, act_ref):
    xn = _rms_norm_rows(x, g_ref[...]).astype(BF16)
    _swiglu_up(xn, w13_ref, act_ref, range(w2_ref.shape[0] // FF_CHUNK))
    return x + 0.5 * jnp.dot(act_ref[...], w2_ref[...], preferred_element_type=F32)


def _ffn_kernel(x_ref, g_ref, w13_ref, w2_ref, y_ref, act_ref):
    y_ref[...] = _half_swiglu_step(x_ref[...], g_ref, w13_ref, w2_ref, act_ref)


def _kv_tail(y, ng_ref, wkv_ref, kg_ref, pool_ref, k_ref, vt_ref):
    d_attn = k_ref.shape[1]
    h = _rms_norm_rows(y, ng_ref[...]).astype(BF16)
    kv = jnp.dot(h, wkv_ref[...], preferred_element_type=F32)
    k_ref[...] = _head_rms_norm(kv[:, :d_attn], kg_ref[...], pool_ref).astype(BF16)
    for pair in range(vt_ref.shape[0]):
        for blk in range(vt_ref.shape[1]):
            v_blk = kv[blk * ATT_T:(blk + 1) * ATT_T,
                       d_attn + pair * LANES:d_attn + (pair + 1) * LANES]
            vt_ref[pair, blk] = v_blk.T.astype(BF16)


def _ffn_q_kernel(x_ref, g_ref, w13_ref, w2_ref, ng_ref, wq_ref, qg_ref, pool_ref,
                  y_ref, q_ref, act_ref):
    y = _half_swiglu_step(x_ref[...], g_ref, w13_ref, w2_ref, act_ref)
    y_ref[...] = y
    h = _rms_norm_rows(y, ng_ref[...]).astype(BF16)
    q = jnp.dot(h, wq_ref[...], preferred_element_type=F32)
    q_ref[...] = (_head_rms_norm(q, qg_ref[...], pool_ref) * (HEAD_DIM ** -0.5)).astype(BF16)


def _cast_job(src, lead, n_steps):
    rows, cols = src.shape[-2:]
    n_chunks = max(k for k in range(1, n_steps + 1)
                   if rows % k == 0 and (rows // k) % BF16_ROWS == 0)
    chunk = rows // n_chunks
    if lead is None:
        in_spec = pl.BlockSpec((chunk, cols), lambda i: (jnp.minimum(i, n_chunks - 1), 0))
    else:
        in_spec = pl.BlockSpec((None, chunk, cols),
                               lambda i: (lead, jnp.minimum(i, n_chunks - 1), 0))
    out_spec = pl.BlockSpec((chunk, cols), lambda i: (jnp.minimum(i, n_chunks - 1), 0))
    return in_spec, out_spec, jax.ShapeDtypeStruct((rows, cols), BF16)


def _ffn_body(*refs, core, n_in, n_out, n_cast):
    ins, rest = refs[:n_in], refs[n_in:]
    cast_in, rest = rest[:n_cast], rest[n_cast:]
    outs, rest = rest[:n_out], rest[n_out:]
    cast_out, scratch = rest[:n_cast], rest[n_cast:]
    for src_ref, dst_ref in zip(cast_in, cast_out):
        dst_ref[...] = src_ref[...].astype(BF16)
    core(*ins, *outs, *scratch)


def _ffn_call(body, name, x2d, gain, w13, w2, casts, extra_in=(), extra_in_specs=(),
              extra_out_shapes=(), extra_out_specs=()):
    m, d = x2d.shape
    d_ff = w2.shape[0]
    n_steps = m // FFN_TM
    row_spec = pl.BlockSpec((FFN_TM, d), lambda i: (i, 0))
    jobs = [_cast_job(src, lead, n_steps) for src, lead in casts]
    in_specs = [row_spec, _const_spec((1, d)), _const_spec((d, 2 * d_ff)),
                _const_spec((d_ff, d)), *extra_in_specs]
    n_out = 1 + len(extra_out_shapes)
    out = pl.pallas_call(
        functools.partial(_ffn_body, core=body, n_in=len(in_specs), n_out=n_out,
                          n_cast=len(jobs)),
        grid=(n_steps,),
        in_specs=in_specs + [job[0] for job in jobs],
        out_specs=[row_spec, *extra_out_specs] + [job[1] for job in jobs],
        out_shape=[jax.ShapeDtypeStruct((m, d), F32), *extra_out_shapes] + [job[2] for job in jobs],
        scratch_shapes=[pltpu.VMEM((FFN_TM, d_ff), BF16)],
        compiler_params=pltpu.CompilerParams(
            dimension_semantics=("arbitrary",), vmem_limit_bytes=VMEM_LIMIT),
        name=name,
    )(x2d, gain.reshape(1, d), w13, w2, *extra_in, *(src for src, _ in casts))
    return out[:n_out], out[n_out:]


def _ffn(x2d, gain, w13, w2, casts):
    return _ffn_call(_ffn_kernel, "ffn", x2d, gain, w13, w2, casts)


def _ffn_q(x2d, gain, w13, w2, casts, q_in_gain, w_q, q_gain_row, pool):
    m, d = x2d.shape
    d_attn = w_q.shape[1]
    return _ffn_call(
        _ffn_q_kernel, "ffn_q", x2d, gain, w13, w2, casts,
        extra_in=(q_in_gain.reshape(1, d), w_q, q_gain_row, pool),
        extra_in_specs=(_const_spec((1, d)), _const_spec((d, d_attn)),
                        _const_spec((1, d_attn)), _const_spec((POOL_W, POOL_W))),
        extra_out_shapes=(jax.ShapeDtypeStruct((m, d_attn), BF16),),
        extra_out_specs=(pl.BlockSpec((FFN_TM, d_attn), lambda i: (i, 0)),))


def _gelu_tanh(x):
    c = np.float32(np.sqrt(2.0 / np.pi))
    return x * (0.5 * (1.0 + jnp.tanh(c * (x + 0.044715 * (x * x * x)))))


def _lru_in(x, g_ref, win_ref):
    h = _rms_norm_rows(x, g_ref[...]).astype(BF16)
    return jnp.dot(h, win_ref[...], preferred_element_type=F32)


def _lru_gates(proj, cw_ref, cb_ref, wr_ref, wi_ref, br_ref, bi_ref, lam_ref,
               rec_ref, a_ref, u_ref, gate_ref):
    ts, w = gate_ref.shape
    bw = wr_ref.shape[1]
    n_blocks = w // bw
    gate_ref[...] = _gelu_tanh(proj[:, :w])
    rec = proj[:, w:]
    rec_ref[SUBLANES:SUBLANES + ts, :] = rec
    xc = cb_ref[...] + cw_ref[CONV_W - 1:CONV_W, :] * rec
    for k in range(CONV_W - 1):
        back = CONV_W - 1 - k
        xc = xc + cw_ref[k:k + 1, :] * rec_ref[SUBLANES - back:SUBLANES - back + ts, :]
    rec_ref[0:SUBLANES, :] = rec_ref[ts:ts + SUBLANES, :]

    xcb = xc.astype(BF16)
    sp_lam = jnp.maximum(-lam_ref[...], 0.0) + jnp.log(1.0 + jnp.exp(-jnp.abs(lam_ref[...])))
    for n in range(n_blocks):
        sl = slice(n * bw, (n + 1) * bw)
        r = _sigmoid(jnp.dot(xcb[:, sl], wr_ref[sl, :], preferred_element_type=F32) + br_ref[:, sl])
        i = _sigmoid(jnp.dot(xcb[:, sl], wi_ref[sl, :], preferred_element_type=F32) + bi_ref[:, sl])
        log_a = (-LRU_C) * r * sp_lam[:, sl]
        a = jnp.exp(log_a)
        a_ref[:, sl] = a
        u_ref[:, sl] = jnp.sqrt(-jnp.tanh(log_a) * (1.0 + a * a)) * (i * xc[:, sl])


def _lru_scan(a_ref, u_ref, hcar_ref):
    ts, w = u_ref.shape
    row = lax.broadcasted_iota(jnp.int32, (SUBLANES, w), 0)
    hprev = hcar_ref[...]
    for gidx in range(ts // SUBLANES):
        rows = slice(gidx * SUBLANES, (gidx + 1) * SUBLANES)
        a = a_ref[rows, :]
        u = u_ref[rows, :]
        for sh in (1, 2, 4):
            keep = row >= sh
            u_new = jnp.where(keep, u + a * pltpu.roll(u, sh, 0), u)
            a = jnp.where(keep, a * pltpu.roll(a, sh, 0), a)
            u = u_new
        hs = u + a * hprev
        u_ref[rows, :] = hs
        hprev = hs[SUBLANES - 1:SUBLANES, :]
    hcar_ref[...] = hprev


def _lru_out(x, u_ref, gate_ref, wout_ref):
    y = (u_ref[...] * gate_ref[...]).astype(BF16)
    return x + jnp.dot(y, wout_ref[...], preferred_element_type=F32)


N_LRU_IN = 10
N_FFN_IN = 3


def _lru_ffn_kernel(*refs, tiles_per_seq, with_kv):
    x_ref, refs = refs[0], refs[1:]
    lru_in, refs = refs[:N_LRU_IN], refs[N_LRU_IN:]
    ffn_in, refs = refs[:N_FFN_IN], refs[N_FFN_IN:]
    if with_kv:
        kv_in, refs = refs[:4], refs[4:]
        (y_ref, k_ref, vt_ref), refs = refs[:3], refs[3:]
    else:
        y_ref, refs = refs[0], refs[1:]
    act_ref, mid_ref, rec_ref, a_ref, u_ref, gate_ref, hcar_ref = refs
    step = pl.program_id(0)

    @pl.when(step == 0)
    def _():
        mid_ref[...] = jnp.zeros_like(mid_ref)

    @pl.when(step % tiles_per_seq == 0)
    def _():
        rec_ref[0:SUBLANES, :] = jnp.zeros((SUBLANES, rec_ref.shape[1]), F32)
        hcar_ref[...] = jnp.zeros_like(hcar_ref)

    mg_ref, win_ref, cw_ref, cb_ref, wr_ref, wi_ref, br_ref, bi_ref, lam_ref, wout_ref = lru_in
    g_ref, w13_ref, w2_ref = ffn_in
    n_chunks = w2_ref.shape[0] // FF_CHUNK
    x = x_ref[...]
    x_prev = mid_ref[...]
    cut = [0, (n_chunks + 1) // 4, n_chunks - 1, n_chunks]
    xn = _rms_norm_rows(x_prev, g_ref[...]).astype(BF16)
    _swiglu_up(xn, w13_ref, act_ref, range(cut[0], cut[1]))
    proj = _lru_in(x, mg_ref, win_ref)
    _swiglu_up(xn, w13_ref, act_ref, range(cut[1], cut[2]))
    _lru_gates(proj, cw_ref, cb_ref, wr_ref, wi_ref, br_ref, bi_ref, lam_ref,
               rec_ref, a_ref, u_ref, gate_ref)
    _swiglu_up(xn, w13_ref, act_ref, range(cut[2], cut[3]))
    y = x_prev + 0.5 * jnp.dot(act_ref[...], w2_ref[...], preferred_element_type=F32)
    y_ref[...] = y
    _lru_scan(a_ref, u_ref, hcar_ref)
    mid_ref[...] = _lru_out(x, u_ref, gate_ref, wout_ref)
    if with_kv:
        _kv_tail(y, *kv_in, k_ref, vt_ref)


def _lru_ffn(x2d, seq_len, lru, ffn, casts, kv=None):
    m, d = x2d.shape
    gain, w_in, conv_w, conv_b, w_r, w_i, b_r, b_i, lam, w_out = lru
    f_gain, w13, w2 = ffn
    w = w_out.shape[0]
    d_ff = w2.shape[0]
    n_tiles = m // LRU_TS
    tiles_per_seq = seq_len // LRU_TS
    last = n_tiles - 1
    prev = lambda i: jnp.maximum(i - 1, 0)
    jobs = [_cast_job(src, lead, n_tiles + 1) for src, lead in casts]
    row_vec = lambda v: v.reshape(1, -1)
    operands = [x2d, row_vec(gain), w_in, conv_w, row_vec(conv_b), w_r, w_i, row_vec(b_r),
                row_vec(b_i), row_vec(lam), w_out, row_vec(f_gain), w13, w2]
    in_specs = [pl.BlockSpec((LRU_TS, d), lambda i: (jnp.minimum(i, last), 0))]
    in_specs += [_const_spec(op.shape) for op in operands[1:]]
    out_specs = [pl.BlockSpec((LRU_TS, d), lambda i: (prev(i), 0))]
    out_shape = [jax.ShapeDtypeStruct((m, d), F32)]
    if kv is not None:
        kv_gain, w_kv, k_gain_row, pool = kv
        d_attn = w_kv.shape[1] // 2
        n_pairs = d_attn // LANES
        kv_ops = [row_vec(kv_gain), w_kv, k_gain_row, pool]
        operands += kv_ops
        in_specs += [_const_spec(op.shape) for op in kv_ops]
        out_specs += [
            pl.BlockSpec((LRU_TS, d_attn), lambda i: (prev(i), 0)),
            pl.BlockSpec((None, n_pairs, LRU_TS // ATT_T, LANES, ATT_T),
                         lambda i: (prev(i) // tiles_per_seq, 0, prev(i) % tiles_per_seq, 0, 0))]
        out_shape += [
            jax.ShapeDtypeStruct((m, d_attn), BF16),
            jax.ShapeDtypeStruct((m // seq_len, n_pairs, seq_len // ATT_T, LANES, ATT_T), BF16)]
    n_out = len(out_shape)
    out = pl.pallas_call(
        functools.partial(
            _ffn_body, n_in=len(in_specs), n_out=n_out, n_cast=len(jobs),
            core=functools.partial(_lru_ffn_kernel, tiles_per_seq=tiles_per_seq,
                                   with_kv=kv is not None)),
        grid=(n_tiles + 1,),
        in_specs=in_specs + [job[0] for job in jobs],
        out_specs=out_specs + [job[1] for job in jobs],
        out_shape=out_shape + [job[2] for job in jobs],
        scratch_shapes=[
            pltpu.VMEM((LRU_TS, d_ff), BF16),
            pltpu.VMEM((LRU_TS, d), F32),
            pltpu.VMEM((LRU_TS + SUBLANES, w), F32),
            pltpu.VMEM((LRU_TS, w), F32),
            pltpu.VMEM((LRU_TS, w), F32),
            pltpu.VMEM((LRU_TS, w), F32),
            pltpu.VMEM((1, w), F32),
        ],
        compiler_params=pltpu.CompilerParams(
            dimension_semantics=("arbitrary",), vmem_limit_bytes=VMEM_LIMIT),
        name="lru_ffn",
    )(*operands, *(src for src, _ in casts))
    return out[:n_out], out[n_out:]


def _head_rms_norm(t, gain_row, pool_ref):
    outs = []
    for g in range(t.shape[1] // POOL_W):
        tg = t[:, g * POOL_W:(g + 1) * POOL_W]
        ms = jnp.dot((tg * tg).astype(BF16), pool_ref[...], preferred_element_type=F32)
        outs.append(tg * lax.rsqrt(ms + EPS))
    return jnp.concatenate(outs, axis=1) * gain_row


def _head_pool_matrix():
    idx = np.arange(POOL_W) // HEAD_DIM
    return jnp.asarray((idx[:, None] == idx[None, :]).astype(np.float32) / HEAD_DIM, BF16)


def _attn_causal():
    k_pos = lax.broadcasted_iota(jnp.int32, (ATT_T, ATT_T), 0)
    q_pos = lax.broadcasted_iota(jnp.int32, (ATT_T, ATT_T), 1)
    return k_pos < q_pos


def _attn_halves(z, masked, causal):
    h = ATT_T // 2
    if masked:
        return [(z[:h], causal[:h], 0), (z[h:, h:], causal[h:, h:], h)]
    return [(z[:h], None, 0), (z[h:], None, 0)]


def _attn_widen(x, q0):
    return x if q0 == 0 else jnp.concatenate([jnp.zeros((x.shape[0], q0), x.dtype), x], axis=1)


def _attn_scores(specs, q_heads, k_ref):
    zs = []
    for head, kj, _, _ in specs:
        pair = head // 2
        kb = k_ref[pl.ds(pl.multiple_of(kj * ATT_T, ATT_T), ATT_T), pair * LANES:(pair + 1) * LANES]
        zs.append(lax.dot_general(kb, q_heads[head], (((1,), (1,)), ((), ())),
                                  preferred_element_type=F32))
    return zs


def _attn_sums(zs, specs, tri_ref, causal):
    exts = []
    for z, (_, _, masked, _) in zip(zs, specs):
        ext = []
        for zp, mask, q0 in _attn_halves(z, masked, causal):
            sp = jnp.maximum(zp, 0.0) + jnp.log(1.0 + jnp.exp(-jnp.abs(zp)))
            sp16 = (sp if mask is None else jnp.where(mask, sp, 0.0)).astype(BF16)
            ext.append(jnp.dot(tri_ref[...], _attn_widen(sp16, q0), preferred_element_type=F32))
        exts.append(ext)
    return exts


def _attn_weights(zs, exts, specs, causal, chain_valid=None):
    h = ATT_T // 2
    ws, carries = [], []
    for z, (ext_early, ext_late), (_, _, masked, carry) in zip(zs, exts, specs):
        if isinstance(carry, int):
            carry = carries[carry]
            if chain_valid is not None:
                carry = jnp.where(chain_valid, carry, LOG_W_NONE)
        total_late = jnp.broadcast_to(ext_late[0:1], (SUBLANES, ATT_T))
        carry_early = total_late if carry is None else carry + total_late
        w_halves = []
        for (zp, mask, q0), ext, c in zip(_attn_halves(z, masked, causal), (ext_early, ext_late),
                                          (carry_early, carry)):
            log_w = zp + ext[:, q0:]
            if c is not None:
                log_w = log_w + c[0:1, q0:]
            w = jnp.exp(log_w)
            if mask is not None:
                w = jnp.where(mask, w, 0.0)
            w_halves.append(_attn_widen(w.astype(BF16), q0))
        ws.append(jnp.concatenate(w_halves, axis=0))
        carries.append(carry_early + jnp.broadcast_to(ext_early[0:1], (SUBLANES, ATT_T)))
    return ws, carries


def _attn_values(ws, specs, vt_ref):
    return [jnp.dot(vt_ref[head // 2, kj], w, preferred_element_type=F32)
            for w, (head, kj, _, _) in zip(ws, specs)]


def _attn_ffn_kernel(x_ref, q_ref, k_ref, vt_ref, tri_ref, wo_ref, g_ref, w13_ref, w2_ref, y_ref,
                     act_ref, acc_ref, carry_ref, *, tiles_per_seq, n_tiles):
    t = ATT_T
    n_heads = acc_ref.shape[0]
    n_chunks = w2_ref.shape[0] // FF_CHUNK
    step = pl.program_id(0)
    qi = jnp.minimum(step, n_tiles - 1) % tiles_per_seq

    @pl.when(step == 0)
    def _():
        acc_ref[...] = jnp.zeros_like(acc_ref)

    feat = lax.broadcasted_iota(jnp.int32, (LANES, t), 0)
    o_prev = jnp.concatenate(
        [jnp.where(feat < HEAD_DIM, acc_ref[2 * pair], acc_ref[2 * pair + 1]).T.astype(BF16)
         for pair in range(n_heads // 2)], axis=1)

    lane = lax.broadcasted_iota(jnp.int32, (t, LANES), 1)
    causal = _attn_causal()
    q_heads = []
    for pair in range(n_heads // 2):
        q = q_ref[:, pair * LANES:(pair + 1) * LANES]
        q_heads += [jnp.where(lane < HEAD_DIM, q, jnp.zeros_like(q)),
                    jnp.where(lane >= HEAD_DIM, q, jnp.zeros_like(q))]

    has_prev = qi > 0
    kj_prev = jnp.maximum(qi - 1, 0)
    groups = [range(g, g + ATT_GROUP) for g in range(0, n_heads, ATT_GROUP)]
    n_up = 2 * len(groups) - 1
    shares = ATT_FFN_SHARES if len(ATT_FFN_SHARES) == n_up else (1,) * n_up
    cuts = [-(-n_chunks * sum(shares[:s]) // sum(shares)) for s in range(n_up + 1)]
    up_slices = [range(cuts[s], cuts[s + 1]) for s in range(n_up)]
    ffn_stage = iter(up_slices)
    x_in = xn = None
    results = {}
    for gidx, heads in enumerate(groups):
        specs = ([(head, qi, True, None) for head in heads]
                 + [(head, kj_prev, False, idx) for idx, head in enumerate(heads)])
        zs = _attn_scores(specs, q_heads, k_ref)
        if gidx == 0:
            x_in = x_ref[...] + jnp.dot(o_prev, wo_ref[...], preferred_element_type=F32)
            xn = _rms_norm_rows(x_in, g_ref[...]).astype(BF16)
        else:
            _swiglu_up(xn, w13_ref, act_ref, next(ffn_stage))
        exts = _attn_sums(zs, specs, tri_ref, causal)
        _swiglu_up(xn, w13_ref, act_ref, next(ffn_stage))
        if gidx == len(groups) - 1:
            y_ref[...] = x_in + 0.5 * jnp.dot(act_ref[...], w2_ref[...], preferred_element_type=F32)
        ws, carries = _attn_weights(zs, exts, specs, causal, chain_valid=has_prev)
        pvs = _attn_values(ws, specs, vt_ref)
        for idx, head in enumerate(heads):
            results[head] = (pvs[idx] + pvs[len(heads) + idx], carries[len(heads) + idx])
    for head, (pv, carry) in results.items():
        acc_ref[head] = pv
        carry_ref[head] = carry

    def max_carry(carries):
        top = carries[0]
        for carry in carries[1:]:
            top = jnp.maximum(top, carry)
        return jnp.max(top)

    def cond(state):
        kj, top = state
        return jnp.logical_and(kj >= 0, top > LOG_W_ZERO)

    def body(state):
        kj, _ = state
        specs = [(head, kj, False, carry_ref[head]) for head in range(n_heads)]
        zs = _attn_scores(specs, q_heads, k_ref)
        ws, carries = _attn_weights(zs, _attn_sums(zs, specs, tri_ref, causal), specs, causal)
        for head, pv in enumerate(_attn_values(ws, specs, vt_ref)):
            acc_ref[head] += pv
            carry_ref[head] = carries[head]
        return kj - 1, max_carry(carries)

    top = max_carry([carry for _, carry in results.values()])
    lax.while_loop(cond, body, (qi - 2, top))


def _suffix_sum_matrix():
    j = np.arange(ATT_T // 2)
    return jnp.asarray(-(j[None, :] >= j[:, None]).astype(np.float32), BF16)


def _attn_ffn(x2d, q2d, k, vt, tri, w_o, gain, w13, w2, casts):
    m, d = x2d.shape
    b, s, d_attn = k.shape
    d_ff = w2.shape[0]
    n_heads = d_attn // HEAD_DIM
    n_tiles = m // ATT_T
    tiles_per_seq = s // ATT_T
    cur = lambda i: jnp.minimum(i, n_tiles - 1)
    prev = lambda i: jnp.maximum(i - 1, 0)
    jobs = [_cast_job(src, lead, n_tiles + 1) for src, lead in casts]
    operands = [x2d, q2d, k, vt, tri, w_o, gain.reshape(1, d), w13, w2]
    in_specs = [
        pl.BlockSpec((ATT_T, d), lambda i: (prev(i), 0)),
        pl.BlockSpec((ATT_T, d_attn), lambda i: (cur(i), 0)),
        pl.BlockSpec((None, s, d_attn), lambda i: (cur(i) // tiles_per_seq, 0, 0),
                     pipeline_mode=pl.Buffered(1)),
        pl.BlockSpec((None,) + vt.shape[1:], lambda i: (cur(i) // tiles_per_seq, 0, 0, 0, 0),
                     pipeline_mode=pl.Buffered(1)),
    ] + [_const_spec(op.shape) for op in operands[4:]]
    out = pl.pallas_call(
        functools.partial(
            _ffn_body, n_in=len(in_specs), n_out=1, n_cast=len(jobs),
            core=functools.partial(_attn_ffn_kernel, tiles_per_seq=tiles_per_seq, n_tiles=n_tiles)),
        grid=(n_tiles + 1,),
        in_specs=in_specs + [job[0] for job in jobs],
        out_specs=[pl.BlockSpec((ATT_T, d), lambda i: (prev(i), 0))] + [job[1] for job in jobs],
        out_shape=[jax.ShapeDtypeStruct((m, d), F32)] + [job[2] for job in jobs],
        scratch_shapes=[
            pltpu.VMEM((ATT_T, d_ff), BF16),
            pltpu.VMEM((n_heads, LANES, ATT_T), F32),
            pltpu.VMEM((n_heads, SUBLANES, ATT_T), F32),
        ],
        compiler_params=pltpu.CompilerParams(
            dimension_semantics=("arbitrary",), vmem_limit_bytes=VMEM_LIMIT),
        name="attn_ffn",
    )(*operands, *(src for src, _ in casts))
    return out[:1], out[1:]


def kernel(x, ffn1_norm, ffn1_w13, ffn1_w2, mix_norm, a_w_in, a_conv_w, a_conv_b, a_w_r, a_b_r,
           a_w_i, a_b_i, a_lambda, a_w_out, kv_norm, w_kv, k_norm, b_w_q, q_norm, b_w_o,
           ffn2_norm, ffn2_w13, ffn2_w2):
    b, s, d = x.shape
    depth = ffn1_norm.shape[0]
    n_a = a_w_in.shape[0]
    m = b * s
    pool = _head_pool_matrix()
    tri = _suffix_sum_matrix()

    assert 1 <= n_a < depth and s % FFN_TM == 0 and FFN_TM % ATT_T == 0

    def ffn_sources(l, which):
        w13, w2 = (ffn1_w13, ffn1_w2) if which == 1 else (ffn2_w13, ffn2_w2)
        srcs = {"w13": (w13, l), "w2": (w2, l)}
        if which == 2 and l == n_a - 1:
            srcs["w_kv"] = (w_kv, None)
        if l >= n_a:
            srcs["w_q" if which == 1 else "w_o"] = ((b_w_q if which == 1 else b_w_o), l - n_a)
        return srcs

    def mixer_sources(l):
        if l >= n_a:
            return {}
        lru_w = a_w_out.shape[1]
        return {"w_in": (a_w_in, l), "w_r": (a_w_r.reshape(n_a, lru_w, -1), l),
                "w_i": (a_w_i.reshape(n_a, lru_w, -1), l), "w_out": (a_w_out, l)}

    order = [(l, which) for l in range(depth) for which in (1, 2)]
    ready = {(0, 1, name): (src[lead] if lead is not None else src).astype(BF16)
             for name, (src, lead) in ffn_sources(0, 1).items()}

    x = x.reshape(m, d)
    k_sh = vt_sh = q = None
    for idx, (l, which) in enumerate(order):
        jobs = {}
        if which == 1:
            jobs.update({("mix", l, name): v for name, v in mixer_sources(l).items()})
        if idx + 1 < len(order):
            nxt = order[idx + 1]
            jobs.update({(*nxt, name): v for name, v in ffn_sources(*nxt).items()})
        own = {name: ready.pop((l, which, name)) for name in ffn_sources(l, which)}
        gain = (ffn1_norm if which == 1 else ffn2_norm)[l]
        args = (x, gain, own["w13"], own["w2"], list(jobs.values()))
        if which == 1 and l >= n_a:
            (x, q), cast = _ffn_q(*args, mix_norm[l], own["w_q"],
                                  jnp.tile(q_norm[l - n_a], N_HEADS).reshape(1, -1), pool)
        elif which == 2 and l >= n_a:
            (x,), cast = _attn_ffn(x, q, k_sh, vt_sh, tri, own["w_o"], gain, own["w13"], own["w2"],
                                   list(jobs.values()))
        elif which == 2:
            mix = {name: ready.pop(("mix", l, name)) for name in mixer_sources(l)}
            lru = (mix_norm[l], mix["w_in"], a_conv_w[l], a_conv_b[l], mix["w_r"], mix["w_i"],
                   a_b_r[l], a_b_i[l], a_lambda[l], mix["w_out"])
            kv = None
            if l == n_a - 1:
                kv = (kv_norm, own["w_kv"], jnp.tile(k_norm, N_HEADS).reshape(1, -1), pool)
            outs, cast = _lru_ffn(x, s, lru, (gain, own["w13"], own["w2"]), list(jobs.values()), kv)
            x = outs[0]
            if kv is not None:
                k_sh, vt_sh = outs[1].reshape(b, s, -1), outs[2]
        else:
            (x,), cast = _ffn(*args)
        ready.update(zip(jobs.keys(), cast))
    return x.reshape(b, s, d)
```

```python
import functools

import jax
import jax.numpy as jnp
import numpy as np
from jax import lax
from jax.experimental import pallas as pl
from jax.experimental.pallas import tpu as pltpu

EPS = 1e-6
N_HEADS = 16
HEAD_DIM = 64
CONV_W = 4
LRU_C = 8.0
LANES = 128
SUBLANES = 8
BF16_ROWS = 16
VMEM_LIMIT = 56 * 1024 * 1024

FFN_TM = 1024
FF_CHUNK = 256
LRU_TS = 256
ATT_T = 256
ATT_GROUP = 4
ATT_FFN_SHARES = (1, 3, 1, 4, 0, 2, 0)
POOL_W = 256
LOG_W_ZERO = -104.0
LOG_W_NONE = -1e30

F32 = jnp.float32
BF16 = jnp.bfloat16


def _rms_norm_rows(x, gain_row):
    ms = jnp.mean(x * x, axis=-1, keepdims=True)
    return x * lax.rsqrt(ms + EPS) * gain_row


def _sigmoid(x):
    return 1.0 / (1.0 + jnp.exp(-x))


def _const_spec(shape):
    nd = len(shape)
    return pl.BlockSpec(shape, lambda *_: (0,) * nd, pipeline_mode=pl.Buffered(1))


def _swiglu_up(xn, w13_ref, act_ref, chunks):
    d_ff = act_ref.shape[1]
    for c in chunks:
        lo = c * FF_CHUNK
        gate = jnp.dot(xn, w13_ref[:, lo:lo + FF_CHUNK], preferred_element_type=F32)
        up = jnp.dot(xn, w13_ref[:, d_ff + lo:d_ff + lo + FF_CHUNK], preferred_element_type=F32)
        act_ref[:, lo:lo + FF_CHUNK] = (gate * _sigmoid(gate) * up).astype(BF16)


def _half_swiglu_step(x, g_ref, w13_ref, w2_ref, act_ref):
    xn = _rms_norm_rows(x, g_ref[...]).astype(BF16)
    _swiglu_up(xn, w13_ref, act_ref, range(w2_ref.shape[0] // FF_CHUNK))
    return x + 0.5 * jnp.dot(act_ref[...], w2_ref[...], preferred_element_type=F32)


def _ffn_kernel(x_ref, g_ref, w13_ref, w2_ref, y_ref, act_ref):
    y_ref[...] = _half_swiglu_step(x_ref[...], g_ref, w13_ref, w2_ref, act_ref)


def _kv_tail(y, ng_ref, wkv_ref, kg_ref, pool_ref, k_ref, vt_ref):
    d_attn = k_ref.shape[1]
    h = _rms_norm_rows(y, ng_ref[...]).astype(BF16)
    kv = jnp.dot(h, wkv_ref[...], preferred_element_type=F32)
    k_ref[...] = _head_rms_norm(kv[:, :d_attn], kg_ref[...], pool_ref).astype(BF16)
    for pair in range(vt_ref.shape[0]):
        for blk in range(vt_ref.shape[1]):
            v_blk = kv[blk * ATT_T:(blk + 1) * ATT_T,
                       d_attn + pair * LANES:d_attn + (pair + 1) * LANES]
            vt_ref[pair, blk] = v_blk.T.astype(BF16)


def _ffn_q_kernel(x_ref, g_ref, w13_ref, w2_ref, ng_ref, wq_ref, qg_ref, pool_ref,
                  y_ref, q_ref, act_ref):
    y = _half_swiglu_step(x_ref[...], g_ref, w13_ref, w2_ref, act_ref)
    y_ref[...] = y
    h = _rms_norm_rows(y, ng_ref[...]).astype(BF16)
    q = jnp.dot(h, wq_ref[...], preferred_element_type=F32)
    q_ref[...] = (_head_rms_norm(q, qg_ref[...], pool_ref) * (HEAD_DIM ** -0.5)).astype(BF16)


def _cast_job(src, lead, n_steps):
    rows, cols = src.shape[-2:]
    n_chunks = max(k for k in range(1, n_steps + 1)
                   if rows % k == 0 and (rows // k) % BF16_ROWS == 0)
    chunk = rows // n_chunks
    if lead is None:
        in_spec = pl.BlockSpec((chunk, cols), lambda i: (jnp.minimum(i, n_chunks - 1), 0))
    else:
        in_spec = pl.BlockSpec((None, chunk, cols),
                               lambda i: (lead, jnp.minimum(i, n_chunks - 1), 0))
    out_spec = pl.BlockSpec((chunk, cols), lambda i: (jnp.minimum(i, n_chunks - 1), 0))
    return in_spec, out_spec, jax.ShapeDtypeStruct((rows, cols), BF16)


def _ffn_body(*refs, core, n_in, n_out, n_cast):
    ins, rest = refs[:n_in], refs[n_in:]
    cast_in, rest = rest[:n_cast], rest[n_cast:]
    outs, rest = rest[:n_out], rest[n_out:]
    cast_out, scratch = rest[:n_cast], rest[n_cast:]
    for src_ref, dst_ref in zip(cast_in, cast_out):
        dst_ref[...] = src_ref[...].astype(BF16)
    core(*ins, *outs, *scratch)


def _ffn_call(body, name, x2d, gain, w13, w2, casts, extra_in=(), extra_in_specs=(),
              extra_out_shapes=(), extra_out_specs=()):
    m, d = x2d.shape
    d_ff = w2.shape[0]
    n_steps = m // FFN_TM
    row_spec = pl.BlockSpec((FFN_TM, d), lambda i: (i, 0))
    jobs = [_cast_job(src, lead, n_steps) for src, lead in casts]
    in_specs = [row_spec, _const_spec((1, d)), _const_spec((d, 2 * d_ff)),
                _const_spec((d_ff, d)), *extra_in_specs]
    n_out = 1 + len(extra_out_shapes)
    out = pl.pallas_call(
        functools.partial(_ffn_body, core=body, n_in=len(in_specs), n_out=n_out,
                          n_cast=len(jobs)),
        grid=(n_steps,),
        in_specs=in_specs + [job[0] for job in jobs],
        out_specs=[row_spec, *extra_out_specs] + [job[1] for job in jobs],
        out_shape=[jax.ShapeDtypeStruct((m, d), F32), *extra_out_shapes] + [job[2] for job in jobs],
        scratch_shapes=[pltpu.VMEM((FFN_TM, d_ff), BF16)],
        compiler_params=pltpu.CompilerParams(
            dimension_semantics=("arbitrary",), vmem_limit_bytes=VMEM_LIMIT),
        name=name,
    )(x2d, gain.reshape(1, d), w13, w2, *extra_in, *(src for src, _ in casts))
    return out[:n_out], out[n_out:]


def _ffn(x2d, gain, w13, w2, casts):
    return _ffn_call(_ffn_kernel, "ffn", x2d, gain, w13, w2, casts)


def _ffn_q(x2d, gain, w13, w2, casts, q_in_gain, w_q, q_gain_row, pool):
    m, d = x2d.shape
    d_attn = w_q.shape[1]
    return _ffn_call(
        _ffn_q_kernel, "ffn_q", x2d, gain, w13, w2, casts,
        extra_in=(q_in_gain.reshape(1, d), w_q, q_gain_row, pool),
        extra_in_specs=(_const_spec((1, d)), _const_spec((d, d_attn)),
                        _const_spec((1, d_attn)), _const_spec((POOL_W, POOL_W))),
        extra_out_shapes=(jax.ShapeDtypeStruct((m, d_attn), BF16),),
        extra_out_specs=(pl.BlockSpec((FFN_TM, d_attn), lambda i: (i, 0)),))


def _gelu_tanh(x):
    c = np.float32(np.sqrt(2.0 / np.pi))
    return x * (0.5 * (1.0 + jnp.tanh(c * (x + 0.044715 * (x * x * x)))))


def _lru_in(x, g_ref, win_ref):
    h = _rms_norm_rows(x, g_ref[...]).astype(BF16)
    return jnp.dot(h, win_ref[...], preferred_element_type=F32)


def _lru_gates(proj, cw_ref, cb_ref, wr_ref, wi_ref, br_ref, bi_ref, lam_ref,
               rec_ref, a_ref, u_ref, gate_ref):
    ts, w = gate_ref.shape
    bw = wr_ref.shape[1]
    n_blocks = w // bw
    gate_ref[...] = _gelu_tanh(proj[:, :w])
    rec = proj[:, w:]
    rec_ref[SUBLANES:SUBLANES + ts, :] = rec
    xc = cb_ref[...] + cw_ref[CONV_W - 1:CONV_W, :] * rec
    for k in range(CONV_W - 1):
        back = CONV_W - 1 - k
        xc = xc + cw_ref[k:k + 1, :] * rec_ref[SUBLANES - back:SUBLANES - back + ts, :]
    rec_ref[0:SUBLANES, :] = rec_ref[ts:ts + SUBLANES, :]

    xcb = xc.astype(BF16)
    sp_lam = jnp.maximum(-lam_ref[...], 0.0) + jnp.log(1.0 + jnp.exp(-jnp.abs(lam_ref[...])))
    for n in range(n_blocks):
        sl = slice(n * bw, (n + 1) * bw)
        r = _sigmoid(jnp.dot(xcb[:, sl], wr_ref[sl, :], preferred_element_type=F32) + br_ref[:, sl])
        i = _sigmoid(jnp.dot(xcb[:, sl], wi_ref[sl, :], preferred_element_type=F32) + bi_ref[:, sl])
        log_a = (-LRU_C) * r * sp_lam[:, sl]
        a = jnp.exp(log_a)
        a_ref[:, sl] = a
        u_ref[:, sl] = jnp.sqrt(-jnp.tanh(log_a) * (1.0 + a * a)) * (i * xc[:, sl])


def _lru_scan(a_ref, u_ref, hcar_ref):
    ts, w = u_ref.shape
    row = lax.broadcasted_iota(jnp.int32, (SUBLANES, w), 0)
    hprev = hcar_ref[...]
    for gidx in range(ts // SUBLANES):
        rows = slice(gidx * SUBLANES, (gidx + 1) * SUBLANES)
        a = a_ref[rows, :]
        u = u_ref[rows, :]
        for sh in (1, 2, 4):
            keep = row >= sh
            u_new = jnp.where(keep, u + a * pltpu.roll(u, sh, 0), u)
            a = jnp.where(keep, a * pltpu.roll(a, sh, 0), a)
            u = u_new
        hs = u + a * hprev
        u_ref[rows, :] = hs
        hprev = hs[SUBLANES - 1:SUBLANES, :]
    hcar_ref[...] = hprev


def _lru_out(x, u_ref, gate_ref, wout_ref):
    y = (u_ref[...] * gate_ref[...]).astype(BF16)
    return x + jnp.dot(y, wout_ref[...], preferred_element_type=F32)


N_LRU_IN = 10
N_FFN_IN = 3


def _lru_ffn_kernel(*refs, tiles_per_seq, with_kv):
    x_ref, refs = refs[0], refs[1:]
    lru_in, refs = refs[:N_LRU_IN], refs[N_LRU_IN:]
    ffn_in, refs = refs[:N_FFN_IN], refs[N_FFN_IN:]
    if with_kv:
        kv_in, refs = refs[:4], refs[4:]
        (y_ref, k_ref, vt_ref), refs = refs[:3], refs[3:]
    else:
        y_ref, refs = refs[0], refs[1:]
    act_ref, mid_ref, rec_ref, a_ref, u_ref, gate_ref, hcar_ref = refs
    step = pl.program_id(0)

    @pl.when(step == 0)
    def _():
        mid_ref[...] = jnp.zeros_like(mid_ref)

    @pl.when(step % tiles_per_seq == 0)
    def _():
        rec_ref[0:SUBLANES, :] = jnp.zeros((SUBLANES, rec_ref.shape[1]), F32)
        hcar_ref[...] = jnp.zeros_like(hcar_ref)

    mg_ref, win_ref, cw_ref, cb_ref, wr_ref, wi_ref, br_ref, bi_ref, lam_ref, wout_ref = lru_in
    g_ref, w13_ref, w2_ref = ffn_in
    n_chunks = w2_ref.shape[0] // FF_CHUNK
    x = x_ref[...]
    x_prev = mid_ref[...]
    cut = [0, (n_chunks + 1) // 4, n_chunks - 1, n_chunks]
    xn = _rms_norm_rows(x_prev, g_ref[...]).astype(BF16)
    _swiglu_up(xn, w13_ref, act_ref, range(cut[0], cut[1]))
    proj = _lru_in(x, mg_ref, win_ref)
    _swiglu_up(xn, w13_ref, act_ref, range(cut[1], cut[2]))
    _lru_gates(proj, cw_ref, cb_ref, wr_ref, wi_ref, br_ref, bi_ref, lam_ref,
               rec_ref, a_ref, u_ref, gate_ref)
    _swiglu_up(xn, w13_ref, act_ref, range(cut[2], cut[3]))
    y = x_prev + 0.5 * jnp.dot(act_ref[...], w2_ref[...], preferred_element_type=F32)
    y_ref[...] = y
    _lru_scan(a_ref, u_ref, hcar_ref)
    mid_ref[...] = _lru_out(x, u_ref, gate_ref, wout_ref)
    if with_kv:
        _kv_tail(y, *kv_in, k_ref, vt_ref)


def _lru_ffn(x2d, seq_len, lru, ffn, casts, kv=None):
    m, d = x2d.shape
    gain, w_in, conv_w, conv_b, w_r, w_i, b_r, b_i, lam, w_out = lru
    f_gain, w13, w2 = ffn
    w = w_out.shape[0]
    d_ff = w2.shape[0]
    n_tiles = m // LRU_TS
    tiles_per_seq = seq_len // LRU_TS
    last = n_tiles - 1
    prev = lambda i: jnp.maximum(i - 1, 0)
    jobs = [_cast_job(src, lead, n_tiles + 1) for src, lead in casts]
    row_vec = lambda v: v.reshape(1, -1)
    operands = [x2d, row_vec(gain), w_in, conv_w, row_vec(conv_b), w_r, w_i, row_vec(b_r),
                row_vec(b_i), row_vec(lam), w_out, row_vec(f_gain), w13, w2]
    in_specs = [pl.BlockSpec((LRU_TS, d), lambda i: (jnp.minimum(i, last), 0))]
    in_specs += [_const_spec(op.shape) for op in operands[1:]]
    out_specs = [pl.BlockSpec((LRU_TS, d), lambda i: (prev(i), 0))]
    out_shape = [jax.ShapeDtypeStruct((m, d), F32)]
    if kv is not None:
        kv_gain, w_kv, k_gain_row, pool = kv
        d_attn = w_kv.shape[1] // 2
        n_pairs = d_attn // LANES
        kv_ops = [row_vec(kv_gain), w_kv, k_gain_row, pool]
        operands += kv_ops
        in_specs += [_const_spec(op.shape) for op in kv_ops]
        out_specs += [
            pl.BlockSpec((LRU_TS, d_attn), lambda i: (prev(i), 0)),
            pl.BlockSpec((None, n_pairs, LRU_TS // ATT_T, LANES, ATT_T),
                         lambda i: (prev(i) // tiles_per_seq, 0, prev(i) % tiles_per_seq, 0, 0))]
        out_shape += [
            jax.ShapeDtypeStruct((m, d_attn), BF16),
            jax.ShapeDtypeStruct((m // seq_len, n_pairs, seq_len // ATT_T, LANES, ATT_T), BF16)]
    n_out = len(out_shape)
    out = pl.pallas_call(
        functools.partial(
            _ffn_body, n_in=len(in_specs), n_out=n_out, n_cast=len(jobs),
            core=functools.partial(_lru_ffn_kernel, tiles_per_seq=tiles_per_seq,
                                   with_kv=kv is not None)),
        grid=(n_tiles + 1,),
        in_specs=in_specs + [job[0] for job in jobs],
        out_specs=out_specs + [job[1] for job in jobs],
        out_shape=out_shape + [job[2] for job in jobs],
        scratch_shapes=[
            pltpu.VMEM((LRU_TS, d_ff), BF16),
            pltpu.VMEM((LRU_TS, d), F32),
            pltpu.VMEM((LRU_TS + SUBLANES, w), F32),
            pltpu.VMEM((LRU_TS, w), F32),
            pltpu.VMEM((LRU_TS, w), F32),
            pltpu.VMEM((LRU_TS, w), F32),
            pltpu.VMEM((1, w), F32),
        ],
        compiler_params=pltpu.CompilerParams(
            dimension_semantics=("arbitrary",), vmem_limit_bytes=VMEM_LIMIT),
        name="lru_ffn",
    )(*operands, *(src for src, _ in casts))
    return out[:n_out], out[n_out:]


def _head_rms_norm(t, gain_row, pool_ref):
    outs = []
    for g in range(t.shape[1] // POOL_W):
        tg = t[:, g * POOL_W:(g + 1) * POOL_W]
        ms = jnp.dot((tg * tg).astype(BF16), pool_ref[...], preferred_element_type=F32)
        outs.append(tg * lax.rsqrt(ms + EPS))
    return jnp.concatenate(outs, axis=1) * gain_row


def _head_pool_matrix():
    idx = np.arange(POOL_W) // HEAD_DIM
    return jnp.asarray((idx[:, None] == idx[None, :]).astype(np.float32) / HEAD_DIM, BF16)


def _attn_causal():
    k_pos = lax.broadcasted_iota(jnp.int32, (ATT_T, ATT_T), 0)
    q_pos = lax.broadcasted_iota(jnp.int32, (ATT_T, ATT_T), 1)
    return k_pos < q_pos


def _attn_halves(z, masked, causal):
    h = ATT_T // 2
    if masked:
        return [(z[:h], causal[:h], 0), (z[h:, h:], causal[h:, h:], h)]
    return [(z[:h], None, 0), (z[h:], None, 0)]


def _attn_widen(x, q0):
    return x if q0 == 0 else jnp.concatenate([jnp.zeros((x.shape[0], q0), x.dtype), x], axis=1)


def _attn_scores(specs, q_heads, k_ref):
    zs = []
    for head, kj, _, _ in specs:
        pair = head // 2
        kb = k_ref[pl.ds(pl.multiple_of(kj * ATT_T, ATT_T), ATT_T), pair * LANES:(pair + 1) * LANES]
        zs.append(lax.dot_general(kb, q_heads[head], (((1,), (1,)), ((), ())),
                                  preferred_element_type=F32))
    return zs


def _attn_sums(zs, specs, tri_ref, causal):
    exts = []
    for z, (_, _, masked, _) in zip(zs, specs):
        ext = []
        for zp, mask, q0 in _attn_halves(z, masked, causal):
            sp = jnp.maximum(zp, 0.0) + jnp.log(1.0 + jnp.exp(-jnp.abs(zp)))
            sp16 = (sp if mask is None else jnp.where(mask, sp, 0.0)).astype(BF16)
            ext.append(jnp.dot(tri_ref[...], _attn_widen(sp16, q0), preferred_element_type=F32))
        exts.append(ext)
    return exts


def _attn_weights(zs, exts, specs, causal, chain_valid=None):
    h = ATT_T // 2
    ws, carries = [], []
    for z, (ext_early, ext_late), (_, _, masked, carry) in zip(zs, exts, specs):
        if isinstance(carry, int):
            carry = carries[carry]
            if chain_valid is not None:
                carry = jnp.where(chain_valid, carry, LOG_W_NONE)
        total_late = jnp.broadcast_to(ext_late[0:1], (SUBLANES, ATT_T))
        carry_early = total_late if carry is None else carry + total_late
        w_halves = []
        for (zp, mask, q0), ext, c in zip(_attn_halves(z, masked, causal), (ext_early, ext_late),
                                          (carry_early, carry)):
            log_w = zp + ext[:, q0:]
            if c is not None:
                log_w = log_w + c[0:1, q0:]
            w = jnp.exp(log_w)
            if mask is not None:
                w = jnp.where(mask, w, 0.0)
            w_halves.append(_attn_widen(w.astype(BF16), q0))
        ws.append(jnp.concatenate(w_halves, axis=0))
        carries.append(carry_early + jnp.broadcast_to(ext_early[0:1], (SUBLANES, ATT_T)))
    return ws, carries


def _attn_values(ws, specs, vt_ref):
    return [jnp.dot(vt_ref[head // 2, kj], w, preferred_element_type=F32)
            for w, (head, kj, _, _) in zip(ws, specs)]


def _attn_ffn_kernel(x_ref, q_ref, k_ref, vt_ref, tri_ref, wo_ref, g_ref, w13_ref, w2_ref, y_ref,
                     act_ref, acc_ref, carry_ref, *, tiles_per_seq, n_tiles):
    t = ATT_T
    n_heads = acc_ref.shape[0]
    n_chunks = w2_ref.shape[0] // FF_CHUNK
    step = pl.program_id(0)
    qi = jnp.minimum(step, n_tiles - 1) % tiles_per_seq

    @pl.when(step == 0)
    def _():
        acc_ref[...] = jnp.zeros_like(acc_ref)

    feat = lax.broadcasted_iota(jnp.int32, (LANES, t), 0)
    o_prev = jnp.concatenate(
        [jnp.where(feat < HEAD_DIM, acc_ref[2 * pair], acc_ref[2 * pair + 1]).T.astype(BF16)
         for pair in range(n_heads // 2)], axis=1)

    lane = lax.broadcasted_iota(jnp.int32, (t, LANES), 1)
    causal = _attn_causal()
    q_heads = []
    for pair in range(n_heads // 2):
        q = q_ref[:, pair * LANES:(pair + 1) * LANES]
        q_heads += [jnp.where(lane < HEAD_DIM, q, jnp.zeros_like(q)),
                    jnp.where(lane >= HEAD_DIM, q, jnp.zeros_like(q))]

    has_prev = qi > 0
    kj_prev = jnp.maximum(qi - 1, 0)
    groups = [range(g, g + ATT_GROUP) for g in range(0, n_heads, ATT_GROUP)]
    n_up = 2 * len(groups) - 1
    shares = ATT_FFN_SHARES if len(ATT_FFN_SHARES) == n_up else (1,) * n_up
    cuts = [-(-n_chunks * sum(shares[:s]) // sum(shares)) for s in range(n_up + 1)]
    up_slices = [range(cuts[s], cuts[s + 1]) for s in range(n_up)]
    ffn_stage = iter(up_slices)
    x_in = xn = None
    results = {}
    def finish(heads, specs, ws, carries):
        pvs = _attn_values(ws, specs, vt_ref)
        for idx, head in enumerate(heads):
            results[head] = (pvs[idx] + pvs[len(heads) + idx], carries[len(heads) + idx])

    pending = None
    for gidx, heads in enumerate(groups):
        specs = ([(head, qi, True, None) for head in heads]
                 + [(head, kj_prev, False, idx) for idx, head in enumerate(heads)])
        zs = _attn_scores(specs, q_heads, k_ref)
        if pending is not None:
            finish(*pending)
        if gidx == 0:
            x_in = x_ref[...] + jnp.dot(o_prev, wo_ref[...], preferred_element_type=F32)
            xn = _rms_norm_rows(x_in, g_ref[...]).astype(BF16)
        else:
            _swiglu_up(xn, w13_ref, act_ref, next(ffn_stage))
        exts = _attn_sums(zs, specs, tri_ref, causal)
        _swiglu_up(xn, w13_ref, act_ref, next(ffn_stage))
        if gidx == len(groups) - 1:
            y_ref[...] = x_in + 0.5 * jnp.dot(act_ref[...], w2_ref[...], preferred_element_type=F32)
        ws, carries = _attn_weights(zs, exts, specs, causal, chain_valid=has_prev)
        pending = (heads, specs, ws, carries)
    finish(*pending)
    for head, (pv, carry) in results.items():
        acc_ref[head] = pv
        carry_ref[head] = carry

    def max_carry(carries):
        top = carries[0]
        for carry in carries[1:]:
            top = jnp.maximum(top, carry)
        return jnp.max(top)

    def cond(state):
        kj, top = state
        return jnp.logical_and(kj >= 0, top > LOG_W_ZERO)

    def body(state):
        kj, _ = state
        specs = [(head, kj, False, carry_ref[head]) for head in range(n_heads)]
        zs = _attn_scores(specs, q_heads, k_ref)
        ws, carries = _attn_weights(zs, _attn_sums(zs, specs, tri_ref, causal), specs, causal)
        for head, pv in enumerate(_attn_values(ws, specs, vt_ref)):
            acc_ref[head] += pv
            carry_ref[head] = carries[head]
        return kj - 1, max_carry(carries)

    top = max_carry([carry for _, carry in results.values()])
    lax.while_loop(cond, body, (qi - 2, top))


def _suffix_sum_matrix():
    j = np.arange(ATT_T // 2)
    return jnp.asarray(-(j[None, :] >= j[:, None]).astype(np.float32), BF16)


def _attn_ffn(x2d, q2d, k, vt, tri, w_o, gain, w13, w2, casts):
    m, d = x2d.shape
    b, s, d_attn = k.shape
    d_ff = w2.shape[0]
    n_heads = d_attn // HEAD_DIM
    n_tiles = m // ATT_T
    tiles_per_seq = s // ATT_T
    cur = lambda i: jnp.minimum(i, n_tiles - 1)
    prev = lambda i: jnp.maximum(i - 1, 0)
    jobs = [_cast_job(src, lead, n_tiles + 1) for src, lead in casts]
    operands = [x2d, q2d, k, vt, tri, w_o, gain.reshape(1, d), w13, w2]
    in_specs = [
        pl.BlockSpec((ATT_T, d), lambda i: (prev(i), 0)),
        pl.BlockSpec((ATT_T, d_attn), lambda i: (cur(i), 0)),
        pl.BlockSpec((None, s, d_attn), lambda i: (cur(i) // tiles_per_seq, 0, 0),
                     pipeline_mode=pl.Buffered(1)),
        pl.BlockSpec((None,) + vt.shape[1:], lambda i: (cur(i) // tiles_per_seq, 0, 0, 0, 0),
                     pipeline_mode=pl.Buffered(1)),
    ] + [_const_spec(op.shape) for op in operands[4:]]
    out = pl.pallas_call(
        functools.partial(
            _ffn_body, n_in=len(in_specs), n_out=1, n_cast=len(jobs),
            core=functools.partial(_attn_ffn_kernel, tiles_per_seq=tiles_per_seq, n_tiles=n_tiles)),
        grid=(n_tiles + 1,),
        in_specs=in_specs + [job[0] for job in jobs],
        out_specs=[pl.BlockSpec((ATT_T, d), lambda i: (prev(i), 0))] + [job[1] for job in jobs],
        out_shape=[jax.ShapeDtypeStruct((m, d), F32)] + [job[2] for job in jobs],
        scratch_shapes=[
            pltpu.VMEM((ATT_T, d_ff), BF16),
            pltpu.VMEM((n_heads, LANES, ATT_T), F32),
            pltpu.VMEM((n_heads, SUBLANES, ATT_T), F32),
        ],
        compiler_params=pltpu.CompilerParams(
            dimension_semantics=("arbitrary",), vmem_limit_bytes=VMEM_LIMIT),
        name="attn_ffn",
    )(*operands, *(src for src, _ in casts))
    return out[:1], out[1:]


def kernel(x, ffn1_norm, ffn1_w13, ffn1_w2, mix_norm, a_w_in, a_conv_w, a_conv_b, a_w_r, a_b_r,
           a_w_i, a_b_i, a_lambda, a_w_out, kv_norm, w_kv, k_norm, b_w_q, q_norm, b_w_o,
           ffn2_norm, ffn2_w13, ffn2_w2):
    b, s, d = x.shape
    depth = ffn1_norm.shape[0]
    n_a = a_w_in.shape[0]
    m = b * s
    pool = _head_pool_matrix()
    tri = _suffix_sum_matrix()

    assert 1 <= n_a < depth and s % FFN_TM == 0 and FFN_TM % ATT_T == 0

    def ffn_sources(l, which):
        w13, w2 = (ffn1_w13, ffn1_w2) if which == 1 else (ffn2_w13, ffn2_w2)
        srcs = {"w13": (w13, l), "w2": (w2, l)}
        if which == 2 and l == n_a - 1:
            srcs["w_kv"] = (w_kv, None)
        if l >= n_a:
            srcs["w_q" if which == 1 else "w_o"] = ((b_w_q if which == 1 else b_w_o), l - n_a)
        return srcs

    def mixer_sources(l):
        if l >= n_a:
            return {}
        lru_w = a_w_out.shape[1]
        return {"w_in": (a_w_in, l), "w_r": (a_w_r.reshape(n_a, lru_w, -1), l),
                "w_i": (a_w_i.reshape(n_a, lru_w, -1), l), "w_out": (a_w_out, l)}

    order = [(l, which) for l in range(depth) for which in (1, 2)]
    ready = {(0, 1, name): (src[lead] if lead is not None else src).astype(BF16)
             for name, (src, lead) in ffn_sources(0, 1).items()}

    x = x.reshape(m, d)
    k_sh = vt_sh = q = None
    for idx, (l, which) in enumerate(order):
        jobs = {}
        if which == 1:
            jobs.update({("mix", l, name): v for name, v in mixer_sources(l).items()})
        if idx + 1 < len(order):
            nxt = order[idx + 1]
            jobs.update({(*nxt, name): v for name, v in ffn_sources(*nxt).items()})
        own = {name: ready.pop((l, which, name)) for name in ffn_sources(l, which)}
        gain = (ffn1_norm if which == 1 else ffn2_norm)[l]
        args = (x, gain, own["w13"], own["w2"], list(jobs.values()))
        if which == 1 and l >= n_a:
            (x, q), cast = _ffn_q(*args, mix_norm[l], own["w_q"],
                                  jnp.tile(q_norm[l - n_a], N_HEADS).reshape(1, -1), pool)
        elif which == 2 and l >= n_a:
            (x,), cast = _attn_ffn(x, q, k_sh, vt_sh, tri, own["w_o"], gain, own["w13"], own["w2"],
                                   list(jobs.values()))
        elif which == 2:
            mix = {name: ready.pop(("mix", l, name)) for name in mixer_sources(l)}
            lru = (mix_norm[l], mix["w_in"], a_conv_w[l], a_conv_b[l], mix["w_r"], mix["w_i"],
                   a_b_r[l], a_b_i[l], a_lambda[l], mix["w_out"])
            kv = None
            if l == n_a - 1:
                kv = (kv_norm, own["w_kv"], jnp.tile(k_norm, N_HEADS).reshape(1, -1), pool)
            outs, cast = _lru_ffn(x, s, lru, (gain, own["w13"], own["w2"]), list(jobs.values()), kv)
            x = outs[0]
            if kv is not None:
                k_sh, vt_sh = outs[1].reshape(b, s, -1), outs[2]
        else:
            (x,), cast = _ffn(*args)
        ready.update(zip(jobs.keys(), cast))
    return x.reshape(b, s, d)
```

```python
import functools

import jax
import jax.numpy as jnp
import numpy as np
from jax import lax
from jax.experimental import pallas as pl
from jax.experimental.pallas import tpu as pltpu

EPS = 1e-6
N_HEADS = 16
HEAD_DIM = 64
CONV_W = 4
LRU_C = 8.0
LANES = 128
SUBLANES = 8
BF16_ROWS = 16
VMEM_LIMIT = 56 * 1024 * 1024

FFN_TM = 1024
FF_CHUNK = 256
LRU_TS = 256
ATT_T = 256
ATT_GROUP = 4
ATT_FFN_SHARES = (2, 3, 1, 4, 0, 1, 0)
POOL_W = 256
LOG_W_ZERO = -104.0
LOG_W_NONE = -1e30

F32 = jnp.float32
BF16 = jnp.bfloat16


def _rms_norm_rows(x, gain_row):
    ms = jnp.mean(x * x, axis=-1, keepdims=True)
    return x * lax.rsqrt(ms + EPS) * gain_row


def _sigmoid(x):
    return 1.0 / (1.0 + jnp.exp(-x))


def _const_spec(shape):
    nd = len(shape)
    return pl.BlockSpec(shape, lambda *_: (0,) * nd, pipeline_mode=pl.Buffered(1))


def _swiglu_up(xn, w13_ref, act_ref, chunks):
    d_ff = act_ref.shape[1]
    for c in chunks:
        lo = c * FF_CHUNK
        gate = jnp.dot(xn, w13_ref[:, lo:lo + FF_CHUNK], preferred_element_type=F32)
        up = jnp.dot(xn, w13_ref[:, d_ff + lo:d_ff + lo + FF_CHUNK], preferred_element_type=F32)
        act_ref[:, lo:lo + FF_CHUNK] = (gate * _sigmoid(gate) * up).astype(BF16)


def _half_swiglu_step(x, g_ref, w13_ref, w2_ref, act_ref):
    xn = _rms_norm_rows(x, g_ref[...]).astype(BF16)
    _swiglu_up(xn, w13_ref, act_ref, range(w2_ref.shape[0] // FF_CHUNK))
    return x + 0.5 * jnp.dot(act_ref[...], w2_ref[...], preferred_element_type=F32)


def _ffn_kernel(x_ref, g_ref, w13_ref, w2_ref, y_ref, act_ref):
    y_ref[...] = _half_swiglu_step(x_ref[...], g_ref, w13_ref, w2_ref, act_ref)


def _kv_tail(y, ng_ref, wkv_ref, kg_ref, pool_ref, k_ref, vt_ref):
    d_attn = k_ref.shape[1]
    h = _rms_norm_rows(y, ng_ref[...]).astype(BF16)
    kv = jnp.dot(h, wkv_ref[...], preferred_element_type=F32)
    k_ref[...] = _head_rms_norm(kv[:, :d_attn], kg_ref[...], pool_ref).astype(BF16)
    for pair in range(vt_ref.shape[0]):
        for blk in range(vt_ref.shape[1]):
            v_blk = kv[blk * ATT_T:(blk + 1) * ATT_T,
                       d_attn + pair * LANES:d_attn + (pair + 1) * LANES]
            vt_ref[pair, blk] = v_blk.T.astype(BF16)


def _ffn_q_kernel(x_ref, g_ref, w13_ref, w2_ref, ng_ref, wq_ref, qg_ref, pool_ref,
                  y_ref, q_ref, act_ref):
    y = _half_swiglu_step(x_ref[...], g_ref, w13_ref, w2_ref, act_ref)
    y_ref[...] = y
    h = _rms_norm_rows(y, ng_ref[...]).astype(BF16)
    q = jnp.dot(h, wq_ref[...], preferred_element_type=F32)
    q_ref[...] = (_head_rms_norm(q, qg_ref[...], pool_ref) * (HEAD_DIM ** -0.5)).astype(BF16)


def _cast_job(src, lead, n_steps):
    rows, cols = src.shape[-2:]
    n_chunks = max(k for k in range(1, n_steps + 1)
                   if rows % k == 0 and (rows // k) % BF16_ROWS == 0)
    chunk = rows // n_chunks
    if lead is None:
        in_spec = pl.BlockSpec((chunk, cols), lambda i: (jnp.minimum(i, n_chunks - 1), 0))
    else:
        in_spec = pl.BlockSpec((None, chunk, cols),
                               lambda i: (lead, jnp.minimum(i, n_chunks - 1), 0))
    out_spec = pl.BlockSpec((chunk, cols), lambda i: (jnp.minimum(i, n_chunks - 1), 0))
    return in_spec, out_spec, jax.ShapeDtypeStruct((rows, cols), BF16)


def _ffn_body(*refs, core, n_in, n_out, n_cast):
    ins, rest = refs[:n_in], refs[n_in:]
    cast_in, rest = rest[:n_cast], rest[n_cast:]
    outs, rest = rest[:n_out], rest[n_out:]
    cast_out, scratch = rest[:n_cast], rest[n_cast:]
    for src_ref, dst_ref in zip(cast_in, cast_out):
        dst_ref[...] = src_ref[...].astype(BF16)
    core(*ins, *outs, *scratch)


def _ffn_call(body, name, x2d, gain, w13, w2, casts, extra_in=(), extra_in_specs=(),
              extra_out_shapes=(), extra_out_specs=()):
    m, d = x2d.shape
    d_ff = w2.shape[0]
    n_steps = m // FFN_TM
    row_spec = pl.BlockSpec((FFN_TM, d), lambda i: (i, 0))
    jobs = [_cast_job(src, lead, n_steps) for src, lead in casts]
    in_specs = [row_spec, _const_spec((1, d)), _const_spec((d, 2 * d_ff)),
                _const_spec((d_ff, d)), *extra_in_specs]
    n_out = 1 + len(extra_out_shapes)
    out = pl.pallas_call(
        functools.partial(_ffn_body, core=body, n_in=len(in_specs), n_out=n_out,
                          n_cast=len(jobs)),
        grid=(n_steps,),
        in_specs=in_specs + [job[0] for job in jobs],
        out_specs=[row_spec, *extra_out_specs] + [job[1] for job in jobs],
        out_shape=[jax.ShapeDtypeStruct((m, d), F32), *extra_out_shapes] + [job[2] for job in jobs],
        scratch_shapes=[pltpu.VMEM((FFN_TM, d_ff), BF16)],
        compiler_params=pltpu.CompilerParams(
            dimension_semantics=("arbitrary",), vmem_limit_bytes=VMEM_LIMIT),
        name=name,
    )(x2d, gain.reshape(1, d), w13, w2, *extra_in, *(src for src, _ in casts))
    return out[:n_out], out[n_out:]


def _ffn(x2d, gain, w13, w2, casts):
    return _ffn_call(_ffn_kernel, "ffn", x2d, gain, w13, w2, casts)


def _ffn_q(x2d, gain, w13, w2, casts, q_in_gain, w_q, q_gain_row, pool):
    m, d = x2d.shape
    d_attn = w_q.shape[1]
    return _ffn_call(
        _ffn_q_kernel, "ffn_q", x2d, gain, w13, w2, casts,
        extra_in=(q_in_gain.reshape(1, d), w_q, q_gain_row, pool),
        extra_in_specs=(_const_spec((1, d)), _const_spec((d, d_attn)),
                        _const_spec((1, d_attn)), _const_spec((POOL_W, POOL_W))),
        extra_out_shapes=(jax.ShapeDtypeStruct((m, d_attn), BF16),),
        extra_out_specs=(pl.BlockSpec((FFN_TM, d_attn), lambda i: (i, 0)),))


def _gelu_tanh(x):
    c = np.float32(np.sqrt(2.0 / np.pi))
    return x * (0.5 * (1.0 + jnp.tanh(c * (x + 0.044715 * (x * x * x)))))


def _lru_in(x, g_ref, win_ref):
    h = _rms_norm_rows(x, g_ref[...]).astype(BF16)
    return jnp.dot(h, win_ref[...], preferred_element_type=F32)


def _lru_gates(proj, cw_ref, cb_ref, wr_ref, wi_ref, br_ref, bi_ref, lam_ref,
               rec_ref, a_ref, u_ref, gate_ref):
    ts, w = gate_ref.shape
    bw = wr_ref.shape[1]
    n_blocks = w // bw
    gate_ref[...] = _gelu_tanh(proj[:, :w])
    rec = proj[:, w:]
    rec_ref[SUBLANES:SUBLANES + ts, :] = rec
    xc = cb_ref[...] + cw_ref[CONV_W - 1:CONV_W, :] * rec
    for k in range(CONV_W - 1):
        back = CONV_W - 1 - k
        xc = xc + cw_ref[k:k + 1, :] * rec_ref[SUBLANES - back:SUBLANES - back + ts, :]
    rec_ref[0:SUBLANES, :] = rec_ref[ts:ts + SUBLANES, :]

    xcb = xc.astype(BF16)
    sp_lam = jnp.maximum(-lam_ref[...], 0.0) + jnp.log(1.0 + jnp.exp(-jnp.abs(lam_ref[...])))
    for n in range(n_blocks):
        sl = slice(n * bw, (n + 1) * bw)
        r = _sigmoid(jnp.dot(xcb[:, sl], wr_ref[sl, :], preferred_element_type=F32) + br_ref[:, sl])
        i = _sigmoid(jnp.dot(xcb[:, sl], wi_ref[sl, :], preferred_element_type=F32) + bi_ref[:, sl])
        log_a = (-LRU_C) * r * sp_lam[:, sl]
        a = jnp.exp(log_a)
        a_ref[:, sl] = a
        u_ref[:, sl] = jnp.sqrt(-jnp.tanh(log_a) * (1.0 + a * a)) * (i * xc[:, sl])


def _lru_scan(a_ref, u_ref, hcar_ref):
    ts, w = u_ref.shape
    row = lax.broadcasted_iota(jnp.int32, (SUBLANES, w), 0)
    hprev = hcar_ref[...]
    for gidx in range(ts // SUBLANES):
        rows = slice(gidx * SUBLANES, (gidx + 1) * SUBLANES)
        a = a_ref[rows, :]
        u = u_ref[rows, :]
        for sh in (1, 2, 4):
            keep = row >= sh
            u_new = jnp.where(keep, u + a * pltpu.roll(u, sh, 0), u)
            a = jnp.where(keep, a * pltpu.roll(a, sh, 0), a)
            u = u_new
        hs = u + a * hprev
        u_ref[rows, :] = hs
        hprev = hs[SUBLANES - 1:SUBLANES, :]
    hcar_ref[...] = hprev


def _lru_out(x, u_ref, gate_ref, wout_ref):
    y = (u_ref[...] * gate_ref[...]).astype(BF16)
    return x + jnp.dot(y, wout_ref[...], preferred_element_type=F32)


N_LRU_IN = 10
N_FFN_IN = 3


def _lru_ffn_kernel(*refs, tiles_per_seq, with_kv):
    x_ref, refs = refs[0], refs[1:]
    lru_in, refs = refs[:N_LRU_IN], refs[N_LRU_IN:]
    ffn_in, refs = refs[:N_FFN_IN], refs[N_FFN_IN:]
    if with_kv:
        kv_in, refs = refs[:4], refs[4:]
        (y_ref, k_ref, vt_ref), refs = refs[:3], refs[3:]
    else:
        y_ref, refs = refs[0], refs[1:]
    act_ref, mid_ref, rec_ref, a_ref, u_ref, gate_ref, hcar_ref = refs
    step = pl.program_id(0)

    @pl.when(step == 0)
    def _():
        mid_ref[...] = jnp.zeros_like(mid_ref)

    @pl.when(step % tiles_per_seq == 0)
    def _():
        rec_ref[0:SUBLANES, :] = jnp.zeros((SUBLANES, rec_ref.shape[1]), F32)
        hcar_ref[...] = jnp.zeros_like(hcar_ref)

    mg_ref, win_ref, cw_ref, cb_ref, wr_ref, wi_ref, br_ref, bi_ref, lam_ref, wout_ref = lru_in
    g_ref, w13_ref, w2_ref = ffn_in
    n_chunks = w2_ref.shape[0] // FF_CHUNK
    x = x_ref[...]
    x_prev = mid_ref[...]
    cut = [0, (n_chunks + 1) // 4, n_chunks - 1, n_chunks]
    xn = _rms_norm_rows(x_prev, g_ref[...]).astype(BF16)
    _swiglu_up(xn, w13_ref, act_ref, range(cut[0], cut[1]))
    proj = _lru_in(x, mg_ref, win_ref)
    _swiglu_up(xn, w13_ref, act_ref, range(cut[1], cut[2]))
    _lru_gates(proj, cw_ref, cb_ref, wr_ref, wi_ref, br_ref, bi_ref, lam_ref,
               rec_ref, a_ref, u_ref, gate_ref)
    _swiglu_up(xn, w13_ref, act_ref, range(cut[2], cut[3]))
    y = x_prev + 0.5 * jnp.dot(act_ref[...], w2_ref[...], preferred_element_type=F32)
    y_ref[...] = y
    _lru_scan(a_ref, u_ref, hcar_ref)
    mid_ref[...] = _lru_out(x, u_ref, gate_ref, wout_ref)
    if with_kv:
        _kv_tail(y, *kv_in, k_ref, vt_ref)


def _lru_ffn(x2d, seq_len, lru, ffn, casts, kv=None):
    m, d = x2d.shape
    gain, w_in, conv_w, conv_b, w_r, w_i, b_r, b_i, lam, w_out = lru
    f_gain, w13, w2 = ffn
    w = w_out.shape[0]
    d_ff = w2.shape[0]
    n_tiles = m // LRU_TS
    tiles_per_seq = seq_len // LRU_TS
    last = n_tiles - 1
    prev = lambda i: jnp.maximum(i - 1, 0)
    jobs = [_cast_job(src, lead, n_tiles + 1) for src, lead in casts]
    row_vec = lambda v: v.reshape(1, -1)
    operands = [x2d, row_vec(gain), w_in, conv_w, row_vec(conv_b), w_r, w_i, row_vec(b_r),
                row_vec(b_i), row_vec(lam), w_out, row_vec(f_gain), w13, w2]
    in_specs = [pl.BlockSpec((LRU_TS, d), lambda i: (jnp.minimum(i, last), 0))]
    in_specs += [_const_spec(op.shape) for op in operands[1:]]
    out_specs = [pl.BlockSpec((LRU_TS, d), lambda i: (prev(i), 0))]
    out_shape = [jax.ShapeDtypeStruct((m, d), F32)]
    if kv is not None:
        kv_gain, w_kv, k_gain_row, pool = kv
        d_attn = w_kv.shape[1] // 2
        n_pairs = d_attn // LANES
        kv_ops = [row_vec(kv_gain), w_kv, k_gain_row, pool]
        operands += kv_ops
        in_specs += [_const_spec(op.shape) for op in kv_ops]
        out_specs += [
            pl.BlockSpec((LRU_TS, d_attn), lambda i: (prev(i), 0)),
            pl.BlockSpec((None, n_pairs, LRU_TS // ATT_T, LANES, ATT_T),
                         lambda i: (prev(i) // tiles_per_seq, 0, prev(i) % tiles_per_seq, 0, 0))]
        out_shape += [
            jax.ShapeDtypeStruct((m, d_attn), BF16),
            jax.ShapeDtypeStruct((m // seq_len, n_pairs, seq_len // ATT_T, LANES, ATT_T), BF16)]
    n_out = len(out_shape)
    out = pl.pallas_call(
        functools.partial(
            _ffn_body, n_in=len(in_specs), n_out=n_out, n_cast=len(jobs),
            core=functools.partial(_lru_ffn_kernel, tiles_per_seq=tiles_per_seq,
                                   with_kv=kv is not None)),
        grid=(n_tiles + 1,),
        in_specs=in_specs + [job[0] for job in jobs],
        out_specs=out_specs + [job[1] for job in jobs],
        out_shape=out_shape + [job[2] for job in jobs],
        scratch_shapes=[
            pltpu.VMEM((LRU_TS, d_ff), BF16),
            pltpu.VMEM((LRU_TS, d), F32),
            pltpu.VMEM((LRU_TS + SUBLANES, w), F32),
            pltpu.VMEM((LRU_TS, w), F32),
            pltpu.VMEM((LRU_TS, w), F32),
            pltpu.VMEM((LRU_TS, w), F32),
            pltpu.VMEM((1, w), F32),
        ],
        compiler_params=pltpu.CompilerParams(
            dimension_semantics=("arbitrary",), vmem_limit_bytes=VMEM_LIMIT),
        name="lru_ffn",
    )(*operands, *(src for src, _ in casts))
    return out[:n_out], out[n_out:]


def _head_rms_norm(t, gain_row, pool_ref):
    outs = []
    for g in range(t.shape[1] // POOL_W):
        tg = t[:, g * POOL_W:(g + 1) * POOL_W]
        ms = jnp.dot((tg * tg).astype(BF16), pool_ref[...], preferred_element_type=F32)
        outs.append(tg * lax.rsqrt(ms + EPS))
    return jnp.concatenate(outs, axis=1) * gain_row


def _head_pool_matrix():
    idx = np.arange(POOL_W) // HEAD_DIM
    return jnp.asarray((idx[:, None] == idx[None, :]).astype(np.float32) / HEAD_DIM, BF16)


def _attn_causal():
    k_pos = lax.broadcasted_iota(jnp.int32, (ATT_T, ATT_T), 0)
    q_pos = lax.broadcasted_iota(jnp.int32, (ATT_T, ATT_T), 1)
    return k_pos < q_pos


def _attn_halves(z, masked, causal):
    h = ATT_T // 2
    if masked:
        return [(z[:h], causal[:h], 0), (z[h:, h:], causal[h:, h:], h)]
    return [(z[:h], None, 0), (z[h:], None, 0)]


def _attn_widen(x, q0):
    return x if q0 == 0 else jnp.concatenate([jnp.zeros((x.shape[0], q0), x.dtype), x], axis=1)


def _attn_scores(specs, q_heads, k_ref):
    zs = []
    for head, kj, _, _ in specs:
        pair = head // 2
        kb = k_ref[pl.ds(pl.multiple_of(kj * ATT_T, ATT_T), ATT_T), pair * LANES:(pair + 1) * LANES]
        zs.append(lax.dot_general(kb, q_heads[head], (((1,), (1,)), ((), ())),
                                  preferred_element_type=F32))
    return zs


def _attn_sums(zs, specs, tri_ref, causal):
    exts = []
    for z, (_, _, masked, _) in zip(zs, specs):
        ext = []
        for zp, mask, q0 in _attn_halves(z, masked, causal):
            sp = jnp.maximum(zp, 0.0) + jnp.log(1.0 + jnp.exp(-jnp.abs(zp)))
            sp16 = (sp if mask is None else jnp.where(mask, sp, 0.0)).astype(BF16)
            ext.append(jnp.dot(tri_ref[...], _attn_widen(sp16, q0), preferred_element_type=F32))
        exts.append(ext)
    return exts


def _attn_weights(zs, exts, specs, causal, chain_valid=None):
    h = ATT_T // 2
    ws, carries = [], []
    for z, (ext_early, ext_late), (_, _, masked, carry) in zip(zs, exts, specs):
        if isinstance(carry, int):
            carry = carries[carry]
            if chain_valid is not None:
                carry = jnp.where(chain_valid, carry, LOG_W_NONE)
        total_late = jnp.broadcast_to(ext_late[0:1], (SUBLANES, ATT_T))
        carry_early = total_late if carry is None else carry + total_late
        w_halves = []
        for (zp, mask, q0), ext, c in zip(_attn_halves(z, masked, causal), (ext_early, ext_late),
                                          (carry_early, carry)):
            log_w = zp + ext[:, q0:]
            if c is not None:
                log_w = log_w + c[0:1, q0:]
            w = jnp.exp(log_w)
            if mask is not None:
                w = jnp.where(mask, w, 0.0)
            w_halves.append(_attn_widen(w.astype(BF16), q0))
        ws.append(jnp.concatenate(w_halves, axis=0))
        carries.append(carry_early + jnp.broadcast_to(ext_early[0:1], (SUBLANES, ATT_T)))
    return ws, carries


def _attn_values(ws, specs, vt_ref):
    return [jnp.dot(vt_ref[head // 2, kj], w, preferred_element_type=F32)
            for w, (head, kj, _, _) in zip(ws, specs)]


def _attn_ffn_kernel(x_ref, q_ref, k_ref, vt_ref, tri_ref, wo_ref, g_ref, w13_ref, w2_ref, y_ref,
                     act_ref, acc_ref, carry_ref, *, tiles_per_seq, n_tiles):
    t = ATT_T
    n_heads = acc_ref.shape[0]
    n_chunks = w2_ref.shape[0] // FF_CHUNK
    step = pl.program_id(0)
    qi = jnp.minimum(step, n_tiles - 1) % tiles_per_seq

    @pl.when(step == 0)
    def _():
        acc_ref[...] = jnp.zeros_like(acc_ref)

    feat = lax.broadcasted_iota(jnp.int32, (LANES, t), 0)
    o_prev = jnp.concatenate(
        [jnp.where(feat < HEAD_DIM, acc_ref[2 * pair], acc_ref[2 * pair + 1]).T.astype(BF16)
         for pair in range(n_heads // 2)], axis=1)

    lane = lax.broadcasted_iota(jnp.int32, (t, LANES), 1)
    causal = _attn_causal()
    q_heads = []
    for pair in range(n_heads // 2):
        q = q_ref[:, pair * LANES:(pair + 1) * LANES]
        q_heads += [jnp.where(lane < HEAD_DIM, q, jnp.zeros_like(q)),
                    jnp.where(lane >= HEAD_DIM, q, jnp.zeros_like(q))]

    has_prev = qi > 0
    kj_prev = jnp.maximum(qi - 1, 0)
    groups = [range(g, g + ATT_GROUP) for g in range(0, n_heads, ATT_GROUP)]
    n_up = 2 * len(groups) - 1
    shares = ATT_FFN_SHARES if len(ATT_FFN_SHARES) == n_up else (1,) * n_up
    cuts = [-(-n_chunks * sum(shares[:s]) // sum(shares)) for s in range(n_up + 1)]
    up_slices = [range(cuts[s], cuts[s + 1]) for s in range(n_up)]
    ffn_stage = iter(up_slices)
    x_in = xn = None
    results = {}
    for gidx, heads in enumerate(groups):
        specs = ([(head, qi, True, None) for head in heads]
                 + [(head, kj_prev, False, idx) for idx, head in enumerate(heads)])
        zs = _attn_scores(specs, q_heads, k_ref)
        if gidx == 0:
            x_in = x_ref[...] + jnp.dot(o_prev, wo_ref[...], preferred_element_type=F32)
            xn = _rms_norm_rows(x_in, g_ref[...]).astype(BF16)
        else:
            _swiglu_up(xn, w13_ref, act_ref, next(ffn_stage))
        exts = _attn_sums(zs, specs, tri_ref, causal)
        _swiglu_up(xn, w13_ref, act_ref, next(ffn_stage))
        if gidx == len(groups) - 1:
            y_ref[...] = x_in + 0.5 * jnp.dot(act_ref[...], w2_ref[...], preferred_element_type=F32)
        ws, carries = _attn_weights(zs, exts, specs, causal, chain_valid=has_prev)
        pvs = _attn_values(ws, specs, vt_ref)
        for idx, head in enumerate(heads):
            results[head] = (pvs[idx] + pvs[len(heads) + idx], carries[len(heads) + idx])
    for head, (pv, carry) in results.items():
        acc_ref[head] = pv
        carry_ref[head] = carry

    def max_carry(carries):
        top = carries[0]
        for carry in carries[1:]:
            top = jnp.maximum(top, carry)
        return jnp.max(top)

    def cond(state):
        kj, top = state
        return jnp.logical_and(kj >= 0, top > LOG_W_ZERO)

    def body(state):
        kj, _ = state
        specs = [(head, kj, False, carry_ref[head]) for head in range(n_heads)]
        zs = _attn_scores(specs, q_heads, k_ref)
        ws, carries = _attn_weights(zs, _attn_sums(zs, specs, tri_ref, causal), specs, causal)
        for head, pv in enumerate(_attn_values(ws, specs, vt_ref)):
            acc_ref[head] += pv
            carry_ref[head] = carries[head]
        return kj - 1, max_carry(carries)

    top = max_carry([carry for _, carry in results.values()])
    lax.while_loop(cond, body, (qi - 2, top))


def _suffix_sum_matrix():
    j = np.arange(ATT_T // 2)
    return jnp.asarray(-(j[None, :] >= j[:, None]).astype(np.float32), BF16)


def _attn_ffn(x2d, q2d, k, vt, tri, w_o, gain, w13, w2, casts):
    m, d = x2d.shape
    b, s, d_attn = k.shape
    d_ff = w2.shape[0]
    n_heads = d_attn // HEAD_DIM
    n_tiles = m // ATT_T
    tiles_per_seq = s // ATT_T
    cur = lambda i: jnp.minimum(i, n_tiles - 1)
    prev = lambda i: jnp.maximum(i - 1, 0)
    jobs = [_cast_job(src, lead, n_tiles + 1) for src, lead in casts]
    operands = [x2d, q2d, k, vt, tri, w_o, gain.reshape(1, d), w13, w2]
    in_specs = [
        pl.BlockSpec((ATT_T, d), lambda i: (prev(i), 0)),
        pl.BlockSpec((ATT_T, d_attn), lambda i: (cur(i), 0)),
        pl.BlockSpec((None, s, d_attn), lambda i: (cur(i) // tiles_per_seq, 0, 0),
                     pipeline_mode=pl.Buffered(1)),
        pl.BlockSpec((None,) + vt.shape[1:], lambda i: (cur(i) // tiles_per_seq, 0, 0, 0, 0),
                     pipeline_mode=pl.Buffered(1)),
    ] + [_const_spec(op.shape) for op in operands[4:]]
    out = pl.pallas_call(
        functools.partial(
            _ffn_body, n_in=len(in_specs), n_out=1, n_cast=len(jobs),
            core=functools.partial(_attn_ffn_kernel, tiles_per_seq=tiles_per_seq, n_tiles=n_tiles)),
        grid=(n_tiles + 1,),
        in_specs=in_specs + [job[0] for job in jobs],
        out_specs=[pl.BlockSpec((ATT_T, d), lambda i: (prev(i), 0))] + [job[1] for job in jobs],
        out_shape=[jax.ShapeDtypeStruct((m, d), F32)] + [job[2] for job in jobs],
        scratch_shapes=[
            pltpu.VMEM((ATT_T, d_ff), BF16),
            pltpu.VMEM((n_heads, LANES, ATT_T), F32),
            pltpu.VMEM((n_heads, SUBLANES, ATT_T), F32),
        ],
        compiler_params=pltpu.CompilerParams(
            dimension_semantics=("arbitrary",), vmem_limit_bytes=VMEM_LIMIT),
        name="attn_ffn",
    )(*operands, *(src for src, _ in casts))
    return out[:1], out[1:]


def kernel(x, ffn1_norm, ffn1_w13, ffn1_w2, mix_norm, a_w_in, a_conv_w, a_conv_b, a_w_r, a_b_r,
           a_w_i, a_b_i, a_lambda, a_w_out, kv_norm, w_kv, k_norm, b_w_q, q_norm, b_w_o,
           ffn2_norm, ffn2_w13, ffn2_w2):
    b, s, d = x.shape
    depth = ffn1_norm.shape[0]
    n_a = a_w_in.shape[0]
    m = b * s
    pool = _head_pool_matrix()
    tri = _suffix_sum_matrix()

    assert 1 <= n_a < depth and s % FFN_TM == 0 and FFN_TM % ATT_T == 0

    def ffn_sources(l, which):
        w13, w2 = (ffn1_w13, ffn1_w2) if which == 1 else (ffn2_w13, ffn2_w2)
        srcs = {"w13": (w13, l), "w2": (w2, l)}
        if which == 2 and l == n_a - 1:
            srcs["w_kv"] = (w_kv, None)
        if l >= n_a:
            srcs["w_q" if which == 1 else "w_o"] = ((b_w_q if which == 1 else b_w_o), l - n_a)
        return srcs

    def mixer_sources(l):
        if l >= n_a:
            return {}
        lru_w = a_w_out.shape[1]
        return {"w_in": (a_w_in, l), "w_r": (a_w_r.reshape(n_a, lru_w, -1), l),
                "w_i": (a_w_i.reshape(n_a, lru_w, -1), l), "w_out": (a_w_out, l)}

    order = [(l, which) for l in range(depth) for which in (1, 2)]
    ready = {(0, 1, name): (src[lead] if lead is not None else src).astype(BF16)
             for name, (src, lead) in ffn_sources(0, 1).items()}

    x = x.reshape(m, d)
    k_sh = vt_sh = q = None
    for idx, (l, which) in enumerate(order):
        jobs = {}
        if which == 1:
            jobs.update({("mix", l, name): v for name, v in mixer_sources(l).items()})
        if idx + 1 < len(order):
            nxt = order[idx + 1]
            jobs.update({(*nxt, name): v for name, v in ffn_sources(*nxt).items()})
        own = {name: ready.pop((l, which, name)) for name in ffn_sources(l, which)}
        gain = (ffn1_norm if which == 1 else ffn2_norm)[l]
        args = (x, gain, own["w13"], own["w2"], list(jobs.values()))
        if which == 1 and l >= n_a:
            (x, q), cast = _ffn_q(*args, mix_norm[l], own["w_q"],
                                  jnp.tile(q_norm[l - n_a], N_HEADS).reshape(1, -1), pool)
        elif which == 2 and l >= n_a:
            (x,), cast = _attn_ffn(x, q, k_sh, vt_sh, tri, own["w_o"], gain, own["w13"], own["w2"],
                                   list(jobs.values()))
        elif which == 2:
            mix = {name: ready.pop(("mix", l, name)) for name in mixer_sources(l)}
            lru = (mix_norm[l], mix["w_in"], a_conv_w[l], a_conv_b[l], mix["w_r"], mix["w_i"],
                   a_b_r[l], a_b_i[l], a_lambda[l], mix["w_out"])
            kv = None
            if l == n_a - 1:
                kv = (kv_norm, own["w_kv"], jnp.tile(k_norm, N_HEADS).reshape(1, -1), pool)
            outs, cast = _lru_ffn(x, s, lru, (gain, own["w13"], own["w2"]), list(jobs.values()), kv)
            x = outs[0]
            if kv is not None:
                k_sh, vt_sh = outs[1].reshape(b, s, -1), outs[2]
        else:
            (x,), cast = _ffn(*args)
        ready.update(zip(jobs.keys(), cast))
    return x.reshape(b, s, d)
```

```python
import functools

import jax
import jax.numpy as jnp
import numpy as np
from jax import lax
from jax.experimental import pallas as pl
from jax.experimental.pallas import tpu as pltpu

EPS = 1e-6
N_HEADS = 16
HEAD_DIM = 64
CONV_W = 4
LRU_C = 8.0
LANES = 128
SUBLANES = 8
BF16_ROWS = 16
VMEM_LIMIT = 56 * 1024 * 1024

FFN_TM = 1024
FF_CHUNK = 256
LRU_TS = 256
ATT_T = 256
ATT_GROUP = 4
ATT_FFN_SHARES = (2, 3, 0, 4, 0, 2, 0)
POOL_W = 256
LOG_W_ZERO = -104.0
LOG_W_NONE = -1e30

F32 = jnp.float32
BF16 = jnp.bfloat16


def _rms_norm_rows(x, gain_row):
    ms = jnp.mean(x * x, axis=-1, keepdims=True)
    return x * lax.rsqrt(ms + EPS) * gain_row


def _sigmoid(x):
    return 1.0 / (1.0 + jnp.exp(-x))


def _const_spec(shape):
    nd = len(shape)
    return pl.BlockSpec(shape, lambda *_: (0,) * nd, pipeline_mode=pl.Buffered(1))


def _swiglu_up(xn, w13_ref, act_ref, chunks):
    d_ff = act_ref.shape[1]
    for c in chunks:
        lo = c * FF_CHUNK
        gate = jnp.dot(xn, w13_ref[:, lo:lo + FF_CHUNK], preferred_element_type=F32)
        up = jnp.dot(xn, w13_ref[:, d_ff + lo:d_ff + lo + FF_CHUNK], preferred_element_type=F32)
        act_ref[:, lo:lo + FF_CHUNK] = (gate * _sigmoid(gate) * up).astype(BF16)


def _half_swiglu_step(x, g_ref, w13_ref, w2_ref, act_ref):
    xn = _rms_norm_rows(x, g_ref[...]).astype(BF16)
    _swiglu_up(xn, w13_ref, act_ref, range(w2_ref.shape[0] // FF_CHUNK))
    return x + 0.5 * jnp.dot(act_ref[...], w2_ref[...], preferred_element_type=F32)


def _ffn_kernel(x_ref, g_ref, w13_ref, w2_ref, y_ref, act_ref):
    y_ref[...] = _half_swiglu_step(x_ref[...], g_ref, w13_ref, w2_ref, act_ref)


def _kv_tail(y, ng_ref, wkv_ref, kg_ref, pool_ref, k_ref, vt_ref):
    d_attn = k_ref.shape[1]
    h = _rms_norm_rows(y, ng_ref[...]).astype(BF16)
    kv = jnp.dot(h, wkv_ref[...], preferred_element_type=F32)
    k_ref[...] = _head_rms_norm(kv[:, :d_attn], kg_ref[...], pool_ref).astype(BF16)
    for pair in range(vt_ref.shape[0]):
        for blk in range(vt_ref.shape[1]):
            v_blk = kv[blk * ATT_T:(blk + 1) * ATT_T,
                       d_attn + pair * LANES:d_attn + (pair + 1) * LANES]
            vt_ref[pair, blk] = v_blk.T.astype(BF16)


def _ffn_q_kernel(x_ref, g_ref, w13_ref, w2_ref, ng_ref, wq_ref, qg_ref, pool_ref,
                  y_ref, q_ref, act_ref):
    y = _half_swiglu_step(x_ref[...], g_ref, w13_ref, w2_ref, act_ref)
    y_ref[...] = y
    h = _rms_norm_rows(y, ng_ref[...]).astype(BF16)
    q = jnp.dot(h, wq_ref[...], preferred_element_type=F32)
    q_ref[...] = (_head_rms_norm(q, qg_ref[...], pool_ref) * (HEAD_DIM ** -0.5)).astype(BF16)


def _cast_job(src, lead, n_steps):
    rows, cols = src.shape[-2:]
    n_chunks = max(k for k in range(1, n_steps + 1)
                   if rows % k == 0 and (rows // k) % BF16_ROWS == 0)
    chunk = rows // n_chunks
    if lead is None:
        in_spec = pl.BlockSpec((chunk, cols), lambda i: (jnp.minimum(i, n_chunks - 1), 0))
    else:
        in_spec = pl.BlockSpec((None, chunk, cols),
                               lambda i: (lead, jnp.minimum(i, n_chunks - 1), 0))
    out_spec = pl.BlockSpec((chunk, cols), lambda i: (jnp.minimum(i, n_chunks - 1), 0))
    return in_spec, out_spec, jax.ShapeDtypeStruct((rows, cols), BF16)


def _ffn_body(*refs, core, n_in, n_out, n_cast):
    ins, rest = refs[:n_in], refs[n_in:]
    cast_in, rest = rest[:n_cast], rest[n_cast:]
    outs, rest = rest[:n_out], rest[n_out:]
    cast_out, scratch = rest[:n_cast], rest[n_cast:]
    for src_ref, dst_ref in zip(cast_in, cast_out):
        dst_ref[...] = src_ref[...].astype(BF16)
    core(*ins, *outs, *scratch)


def _ffn_call(body, name, x2d, gain, w13, w2, casts, extra_in=(), extra_in_specs=(),
              extra_out_shapes=(), extra_out_specs=()):
    m, d = x2d.shape
    d_ff = w2.shape[0]
    n_steps = m // FFN_TM
    row_spec = pl.BlockSpec((FFN_TM, d), lambda i: (i, 0))
    jobs = [_cast_job(src, lead, n_steps) for src, lead in casts]
    in_specs = [row_spec, _const_spec((1, d)), _const_spec((d, 2 * d_ff)),
                _const_spec((d_ff, d)), *extra_in_specs]
    n_out = 1 + len(extra_out_shapes)
    out = pl.pallas_call(
        functools.partial(_ffn_body, core=body, n_in=len(in_specs), n_out=n_out,
                          n_cast=len(jobs)),
        grid=(n_steps,),
        in_specs=in_specs + [job[0] for job in jobs],
        out_specs=[row_spec, *extra_out_specs] + [job[1] for job in jobs],
        out_shape=[jax.ShapeDtypeStruct((m, d), F32), *extra_out_shapes] + [job[2] for job in jobs],
        scratch_shapes=[pltpu.VMEM((FFN_TM, d_ff), BF16)],
        compiler_params=pltpu.CompilerParams(
            dimension_semantics=("arbitrary",), vmem_limit_bytes=VMEM_LIMIT),
        name=name,
    )(x2d, gain.reshape(1, d), w13, w2, *extra_in, *(src for src, _ in casts))
    return out[:n_out], out[n_out:]


def _ffn(x2d, gain, w13, w2, casts):
    return _ffn_call(_ffn_kernel, "ffn", x2d, gain, w13, w2, casts)


def _ffn_q(x2d, gain, w13, w2, casts, q_in_gain, w_q, q_gain_row, pool):
    m, d = x2d.shape
    d_attn = w_q.shape[1]
    return _ffn_call(
        _ffn_q_kernel, "ffn_q", x2d, gain, w13, w2, casts,
        extra_in=(q_in_gain.reshape(1, d), w_q, q_gain_row, pool),
        extra_in_specs=(_const_spec((1, d)), _const_spec((d, d_attn)),
                        _const_spec((1, d_attn)), _const_spec((POOL_W, POOL_W))),
        extra_out_shapes=(jax.ShapeDtypeStruct((m, d_attn), BF16),),
        extra_out_specs=(pl.BlockSpec((FFN_TM, d_attn), lambda i: (i, 0)),))


def _gelu_tanh(x):
    c = np.float32(np.sqrt(2.0 / np.pi))
    return x * (0.5 * (1.0 + jnp.tanh(c * (x + 0.044715 * (x * x * x)))))


def _lru_in(x, g_ref, win_ref):
    h = _rms_norm_rows(x, g_ref[...]).astype(BF16)
    return jnp.dot(h, win_ref[...], preferred_element_type=F32)


def _lru_gates(proj, cw_ref, cb_ref, wr_ref, wi_ref, br_ref, bi_ref, lam_ref,
               rec_ref, a_ref, u_ref, gate_ref):
    ts, w = gate_ref.shape
    bw = wr_ref.shape[1]
    n_blocks = w // bw
    gate_ref[...] = _gelu_tanh(proj[:, :w])
    rec = proj[:, w:]
    rec_ref[SUBLANES:SUBLANES + ts, :] = rec
    xc = cb_ref[...] + cw_ref[CONV_W - 1:CONV_W, :] * rec
    for k in range(CONV_W - 1):
        back = CONV_W - 1 - k
        xc = xc + cw_ref[k:k + 1, :] * rec_ref[SUBLANES - back:SUBLANES - back + ts, :]
    rec_ref[0:SUBLANES, :] = rec_ref[ts:ts + SUBLANES, :]

    xcb = xc.astype(BF16)
    sp_lam = jnp.maximum(-lam_ref[...], 0.0) + jnp.log(1.0 + jnp.exp(-jnp.abs(lam_ref[...])))
    for n in range(n_blocks):
        sl = slice(n * bw, (n + 1) * bw)
        r = _sigmoid(jnp.dot(xcb[:, sl], wr_ref[sl, :], preferred_element_type=F32) + br_ref[:, sl])
        i = _sigmoid(jnp.dot(xcb[:, sl], wi_ref[sl, :], preferred_element_type=F32) + bi_ref[:, sl])
        log_a = (-LRU_C) * r * sp_lam[:, sl]
        a = jnp.exp(log_a)
        a_ref[:, sl] = a
        u_ref[:, sl] = jnp.sqrt(-jnp.tanh(log_a) * (1.0 + a * a)) * (i * xc[:, sl])


def _lru_scan(a_ref, u_ref, hcar_ref):
    ts, w = u_ref.shape
    row = lax.broadcasted_iota(jnp.int32, (SUBLANES, w), 0)
    hprev = hcar_ref[...]
    for gidx in range(ts // SUBLANES):
        rows = slice(gidx * SUBLANES, (gidx + 1) * SUBLANES)
        a = a_ref[rows, :]
        u = u_ref[rows, :]
        for sh in (1, 2, 4):
            keep = row >= sh
            u_new = jnp.where(keep, u + a * pltpu.roll(u, sh, 0), u)
            a = jnp.where(keep, a * pltpu.roll(a, sh, 0), a)
            u = u_new
        hs = u + a * hprev
        u_ref[rows, :] = hs
        hprev = hs[SUBLANES - 1:SUBLANES, :]
    hcar_ref[...] = hprev


def _lru_out(x, u_ref, gate_ref, wout_ref):
    y = (u_ref[...] * gate_ref[...]).astype(BF16)
    return x + jnp.dot(y, wout_ref[...], preferred_element_type=F32)


N_LRU_IN = 10
N_FFN_IN = 3


def _lru_ffn_kernel(*refs, tiles_per_seq, with_kv):
    x_ref, refs = refs[0], refs[1:]
    lru_in, refs = refs[:N_LRU_IN], refs[N_LRU_IN:]
    ffn_in, refs = refs[:N_FFN_IN], refs[N_FFN_IN:]
    if with_kv:
        kv_in, refs = refs[:4], refs[4:]
        (y_ref, k_ref, vt_ref), refs = refs[:3], refs[3:]
    else:
        y_ref, refs = refs[0], refs[1:]
    act_ref, mid_ref, rec_ref, a_ref, u_ref, gate_ref, hcar_ref = refs
    step = pl.program_id(0)

    @pl.when(step == 0)
    def _():
        mid_ref[...] = jnp.zeros_like(mid_ref)

    @pl.when(step % tiles_per_seq == 0)
    def _():
        rec_ref[0:SUBLANES, :] = jnp.zeros((SUBLANES, rec_ref.shape[1]), F32)
        hcar_ref[...] = jnp.zeros_like(hcar_ref)

    mg_ref, win_ref, cw_ref, cb_ref, wr_ref, wi_ref, br_ref, bi_ref, lam_ref, wout_ref = lru_in
    g_ref, w13_ref, w2_ref = ffn_in
    n_chunks = w2_ref.shape[0] // FF_CHUNK
    x = x_ref[...]
    x_prev = mid_ref[...]
    cut = [0, (n_chunks + 5) // 4, n_chunks - 1, n_chunks]
    xn = _rms_norm_rows(x_prev, g_ref[...]).astype(BF16)
    _swiglu_up(xn, w13_ref, act_ref, range(cut[0], cut[1]))
    proj = _lru_in(x, mg_ref, win_ref)
    _swiglu_up(xn, w13_ref, act_ref, range(cut[1], cut[2]))
    _lru_gates(proj, cw_ref, cb_ref, wr_ref, wi_ref, br_ref, bi_ref, lam_ref,
               rec_ref, a_ref, u_ref, gate_ref)
    _swiglu_up(xn, w13_ref, act_ref, range(cut[2], cut[3]))
    y = x_prev + 0.5 * jnp.dot(act_ref[...], w2_ref[...], preferred_element_type=F32)
    y_ref[...] = y
    _lru_scan(a_ref, u_ref, hcar_ref)
    mid_ref[...] = _lru_out(x, u_ref, gate_ref, wout_ref)
    if with_kv:
        _kv_tail(y, *kv_in, k_ref, vt_ref)


def _lru_ffn(x2d, seq_len, lru, ffn, casts, kv=None):
    m, d = x2d.shape
    gain, w_in, conv_w, conv_b, w_r, w_i, b_r, b_i, lam, w_out = lru
    f_gain, w13, w2 = ffn
    w = w_out.shape[0]
    d_ff = w2.shape[0]
    n_tiles = m // LRU_TS
    tiles_per_seq = seq_len // LRU_TS
    last = n_tiles - 1
    prev = lambda i: jnp.maximum(i - 1, 0)
    jobs = [_cast_job(src, lead, n_tiles + 1) for src, lead in casts]
    row_vec = lambda v: v.reshape(1, -1)
    operands = [x2d, row_vec(gain), w_in, conv_w, row_vec(conv_b), w_r, w_i, row_vec(b_r),
                row_vec(b_i), row_vec(lam), w_out, row_vec(f_gain), w13, w2]
    in_specs = [pl.BlockSpec((LRU_TS, d), lambda i: (jnp.minimum(i, last), 0))]
    in_specs += [_const_spec(op.shape) for op in operands[1:]]
    out_specs = [pl.BlockSpec((LRU_TS, d), lambda i: (prev(i), 0))]
    out_shape = [jax.ShapeDtypeStruct((m, d), F32)]
    if kv is not None:
        kv_gain, w_kv, k_gain_row, pool = kv
        d_attn = w_kv.shape[1] // 2
        n_pairs = d_attn // LANES
        kv_ops = [row_vec(kv_gain), w_kv, k_gain_row, pool]
        operands += kv_ops
        in_specs += [_const_spec(op.shape) for op in kv_ops]
        out_specs += [
            pl.BlockSpec((LRU_TS, d_attn), lambda i: (prev(i), 0)),
            pl.BlockSpec((None, n_pairs, LRU_TS // ATT_T, LANES, ATT_T),
                         lambda i: (prev(i) // tiles_per_seq, 0, prev(i) % tiles_per_seq, 0, 0))]
        out_shape += [
            jax.ShapeDtypeStruct((m, d_attn), BF16),
            jax.ShapeDtypeStruct((m // seq_len, n_pairs, seq_len // ATT_T, LANES, ATT_T), BF16)]
    n_out = len(out_shape)
    out = pl.pallas_call(
        functools.partial(
            _ffn_body, n_in=len(in_specs), n_out=n_out, n_cast=len(jobs),
            core=functools.partial(_lru_ffn_kernel, tiles_per_seq=tiles_per_seq,
                                   with_kv=kv is not None)),
        grid=(n_tiles + 1,),
        in_specs=in_specs + [job[0] for job in jobs],
        out_specs=out_specs + [job[1] for job in jobs],
        out_shape=out_shape + [job[2] for job in jobs],
        scratch_shapes=[
            pltpu.VMEM((LRU_TS, d_ff), BF16),
            pltpu.VMEM((LRU_TS, d), F32),
            pltpu.VMEM((LRU_TS + SUBLANES, w), F32),
            pltpu.VMEM((LRU_TS, w), F32),
            pltpu.VMEM((LRU_TS, w), F32),
            pltpu.VMEM((LRU_TS, w), F32),
            pltpu.VMEM((1, w), F32),
        ],
        compiler_params=pltpu.CompilerParams(
            dimension_semantics=("arbitrary",), vmem_limit_bytes=VMEM_LIMIT),
        name="lru_ffn",
    )(*operands, *(src for src, _ in casts))
    return out[:n_out], out[n_out:]


def _head_rms_norm(t, gain_row, pool_ref):
    outs = []
    for g in range(t.shape[1] // POOL_W):
        tg = t[:, g * POOL_W:(g + 1) * POOL_W]
        ms = jnp.dot((tg * tg).astype(BF16), pool_ref[...], preferred_element_type=F32)
        outs.append(tg * lax.rsqrt(ms + EPS))
    return jnp.concatenate(outs, axis=1) * gain_row


def _head_pool_matrix():
    idx = np.arange(POOL_W) // HEAD_DIM
    return jnp.asarray((idx[:, None] == idx[None, :]).astype(np.float32) / HEAD_DIM, BF16)


def _attn_causal():
    k_pos = lax.broadcasted_iota(jnp.int32, (ATT_T, ATT_T), 0)
    q_pos = lax.broadcasted_iota(jnp.int32, (ATT_T, ATT_T), 1)
    return k_pos < q_pos


def _attn_halves(z, masked, causal):
    h = ATT_T // 2
    if masked:
        return [(z[:h], causal[:h], 0), (z[h:, h:], causal[h:, h:], h)]
    return [(z[:h], None, 0), (z[h:], None, 0)]


def _attn_widen(x, q0):
    return x if q0 == 0 else jnp.concatenate([jnp.zeros((x.shape[0], q0), x.dtype), x], axis=1)


def _attn_scores(specs, q_heads, k_ref):
    zs = []
    for head, kj, _, _ in specs:
        pair = head // 2
        kb = k_ref[pl.ds(pl.multiple_of(kj * ATT_T, ATT_T), ATT_T), pair * LANES:(pair + 1) * LANES]
        zs.append(lax.dot_general(kb, q_heads[head], (((1,), (1,)), ((), ())),
                                  preferred_element_type=F32))
    return zs


def _attn_sums(zs, specs, tri_ref, causal):
    exts = []
    for z, (_, _, masked, _) in zip(zs, specs):
        ext = []
        for zp, mask, q0 in _attn_halves(z, masked, causal):
            sp = jnp.maximum(zp, 0.0) + jnp.log(1.0 + jnp.exp(-jnp.abs(zp)))
            sp16 = (sp if mask is None else jnp.where(mask, sp, 0.0)).astype(BF16)
            ext.append(jnp.dot(tri_ref[...], _attn_widen(sp16, q0), preferred_element_type=F32))
        exts.append(ext)
    return exts


def _attn_weights(zs, exts, specs, causal, chain_valid=None):
    h = ATT_T // 2
    ws, carries = [], []
    for z, (ext_early, ext_late), (_, _, masked, carry) in zip(zs, exts, specs):
        if isinstance(carry, int):
            carry = carries[carry]
            if chain_valid is not None:
                carry = jnp.where(chain_valid, carry, LOG_W_NONE)
        total_late = jnp.broadcast_to(ext_late[0:1], (SUBLANES, ATT_T))
        carry_early = total_late if carry is None else carry + total_late
        w_halves = []
        for (zp, mask, q0), ext, c in zip(_attn_halves(z, masked, causal), (ext_early, ext_late),
                                          (carry_early, carry)):
            log_w = zp + ext[:, q0:]
            if c is not None:
                log_w = log_w + c[0:1, q0:]
            w = jnp.exp(log_w)
            if mask is not None:
                w = jnp.where(mask, w, 0.0)
            w_halves.append(_attn_widen(w.astype(BF16), q0))
        ws.append(jnp.concatenate(w_halves, axis=0))
        carries.append(carry_early + jnp.broadcast_to(ext_early[0:1], (SUBLANES, ATT_T)))
    return ws, carries


def _attn_values(ws, specs, vt_ref):
    return [jnp.dot(vt_ref[head // 2, kj], w, preferred_element_type=F32)
            for w, (head, kj, _, _) in zip(ws, specs)]


def _attn_ffn_kernel(x_ref, q_ref, k_ref, vt_ref, tri_ref, wo_ref, g_ref, w13_ref, w2_ref, y_ref,
                     act_ref, acc_ref, carry_ref, *, tiles_per_seq, n_tiles):
    t = ATT_T
    n_heads = acc_ref.shape[0]
    n_chunks = w2_ref.shape[0] // FF_CHUNK
    step = pl.program_id(0)
    qi = jnp.minimum(step, n_tiles - 1) % tiles_per_seq

    @pl.when(step == 0)
    def _():
        acc_ref[...] = jnp.zeros_like(acc_ref)

    feat = lax.broadcasted_iota(jnp.int32, (LANES, t), 0)
    o_prev = jnp.concatenate(
        [jnp.where(feat < HEAD_DIM, acc_ref[2 * pair], acc_ref[2 * pair + 1]).T.astype(BF16)
         for pair in range(n_heads // 2)], axis=1)

    lane = lax.broadcasted_iota(jnp.int32, (t, LANES), 1)
    causal = _attn_causal()
    q_heads = []
    for pair in range(n_heads // 2):
        q = q_ref[:, pair * LANES:(pair + 1) * LANES]
        q_heads += [jnp.where(lane < HEAD_DIM, q, jnp.zeros_like(q)),
                    jnp.where(lane >= HEAD_DIM, q, jnp.zeros_like(q))]

    has_prev = qi > 0
    kj_prev = jnp.maximum(qi - 1, 0)
    groups = [range(g, g + ATT_GROUP) for g in range(0, n_heads, ATT_GROUP)]
    n_up = 2 * len(groups) - 1
    shares = ATT_FFN_SHARES if len(ATT_FFN_SHARES) == n_up else (1,) * n_up
    cuts = [-(-n_chunks * sum(shares[:s]) // sum(shares)) for s in range(n_up + 1)]
    up_slices = [range(cuts[s], cuts[s + 1]) for s in range(n_up)]
    ffn_stage = iter(up_slices)
    x_in = xn = None
    results = {}
    for gidx, heads in enumerate(groups):
        specs = ([(head, qi, True, None) for head in heads]
                 + [(head, kj_prev, False, idx) for idx, head in enumerate(heads)])
        zs = _attn_scores(specs, q_heads, k_ref)
        if gidx == 0:
            x_in = x_ref[...] + jnp.dot(o_prev, wo_ref[...], preferred_element_type=F32)
            xn = _rms_norm_rows(x_in, g_ref[...]).astype(BF16)
        else:
            _swiglu_up(xn, w13_ref, act_ref, next(ffn_stage))
        exts = _attn_sums(zs, specs, tri_ref, causal)
        _swiglu_up(xn, w13_ref, act_ref, next(ffn_stage))
        if gidx == len(groups) - 1:
            y_ref[...] = x_in + 0.5 * jnp.dot(act_ref[...], w2_ref[...], preferred_element_type=F32)
        ws, carries = _attn_weights(zs, exts, specs, causal, chain_valid=has_prev)
        pvs = _attn_values(ws, specs, vt_ref)
        for idx, head in enumerate(heads):
            results[head] = (pvs[idx] + pvs[len(heads) + idx], carries[len(heads) + idx])
    for head, (pv, carry) in results.items():
        acc_ref[head] = pv
        carry_ref[head] = carry

    def max_carry(carries):
        top = carries[0]
        for carry in carries[1:]:
            top = jnp.maximum(top, carry)
        return jnp.max(top)

    def cond(state):
        kj, top = state
        return jnp.logical_and(kj >= 0, top > LOG_W_ZERO)

    def body(state):
        kj, _ = state
        specs = [(head, kj, False, carry_ref[head]) for head in range(n_heads)]
        zs = _attn_scores(specs, q_heads, k_ref)
        ws, carries = _attn_weights(zs, _attn_sums(zs, specs, tri_ref, causal), specs, causal)
        for head, pv in enumerate(_attn_values(ws, specs, vt_ref)):
            acc_ref[head] += pv
            carry_ref[head] = carries[head]
        return kj - 1, max_carry(carries)

    top = max_carry([carry for _, carry in results.values()])
    lax.while_loop(cond, body, (qi - 2, top))


def _suffix_sum_matrix():
    j = np.arange(ATT_T // 2)
    return jnp.asarray(-(j[None, :] >= j[:, None]).astype(np.float32), BF16)


def _attn_ffn(x2d, q2d, k, vt, tri, w_o, gain, w13, w2, casts):
    m, d = x2d.shape
    b, s, d_attn = k.shape
    d_ff = w2.shape[0]
    n_heads = d_attn // HEAD_DIM
    n_tiles = m // ATT_T
    tiles_per_seq = s // ATT_T
    cur = lambda i: jnp.minimum(i, n_tiles - 1)
    prev = lambda i: jnp.maximum(i - 1, 0)
    jobs = [_cast_job(src, lead, n_tiles + 1) for src, lead in casts]
    operands = [x2d, q2d, k, vt, tri, w_o, gain.reshape(1, d), w13, w2]
    in_specs = [
        pl.BlockSpec((ATT_T, d), lambda i: (prev(i), 0)),
        pl.BlockSpec((ATT_T, d_attn), lambda i: (cur(i), 0)),
        pl.BlockSpec((None, s, d_attn), lambda i: (cur(i) // tiles_per_seq, 0, 0),
                     pipeline_mode=pl.Buffered(1)),
        pl.BlockSpec((None,) + vt.shape[1:], lambda i: (cur(i) // tiles_per_seq, 0, 0, 0, 0),
                     pipeline_mode=pl.Buffered(1)),
    ] + [_const_spec(op.shape) for op in operands[4:]]
    out = pl.pallas_call(
        functools.partial(
            _ffn_body, n_in=len(in_specs), n_out=1, n_cast=len(jobs),
            core=functools.partial(_attn_ffn_kernel, tiles_per_seq=tiles_per_seq, n_tiles=n_tiles)),
        grid=(n_tiles + 1,),
        in_specs=in_specs + [job[0] for job in jobs],
        out_specs=[pl.BlockSpec((ATT_T, d), lambda i: (prev(i), 0))] + [job[1] for job in jobs],
        out_shape=[jax.ShapeDtypeStruct((m, d), F32)] + [job[2] for job in jobs],
        scratch_shapes=[
            pltpu.VMEM((ATT_T, d_ff), BF16),
            pltpu.VMEM((n_heads, LANES, ATT_T), F32),
            pltpu.VMEM((n_heads, SUBLANES, ATT_T), F32),
        ],
        compiler_params=pltpu.CompilerParams(
            dimension_semantics=("arbitrary",), vmem_limit_bytes=VMEM_LIMIT),
        name="attn_ffn",
    )(*operands, *(src for src, _ in casts))
    return out[:1], out[1:]


def kernel(x, ffn1_norm, ffn1_w13, ffn1_w2, mix_norm, a_w_in, a_conv_w, a_conv_b, a_w_r, a_b_r,
           a_w_i, a_b_i, a_lambda, a_w_out, kv_norm, w_kv, k_norm, b_w_q, q_norm, b_w_o,
           ffn2_norm, ffn2_w13, ffn2_w2):
    b, s, d = x.shape
    depth = ffn1_norm.shape[0]
    n_a = a_w_in.shape[0]
    m = b * s
    pool = _head_pool_matrix()
    tri = _suffix_sum_matrix()

    assert 1 <= n_a < depth and s % FFN_TM == 0 and FFN_TM % ATT_T == 0

    def ffn_sources(l, which):
        w13, w2 = (ffn1_w13, ffn1_w2) if which == 1 else (ffn2_w13, ffn2_w2)
        srcs = {"w13": (w13, l), "w2": (w2, l)}
        if which == 2 and l == n_a - 1:
            srcs["w_kv"] = (w_kv, None)
        if l >= n_a:
            srcs["w_q" if which == 1 else "w_o"] = ((b_w_q if which == 1 else b_w_o), l - n_a)
        return srcs

    def mixer_sources(l):
        if l >= n_a:
            return {}
        lru_w = a_w_out.shape[1]
        return {"w_in": (a_w_in, l), "w_r": (a_w_r.reshape(n_a, lru_w, -1), l),
                "w_i": (a_w_i.reshape(n_a, lru_w, -1), l), "w_out": (a_w_out, l)}

    order = [(l, which) for l in range(depth) for which in (1, 2)]
    ready = {(0, 1, name): (src[lead] if lead is not None else src).astype(BF16)
             for name, (src, lead) in ffn_sources(0, 1).items()}

    x = x.reshape(m, d)
    k_sh = vt_sh = q = None
    for idx, (l, which) in enumerate(order):
        jobs = {}
        if which == 1:
            jobs.update({("mix", l, name): v for name, v in mixer_sources(l).items()})
        if idx + 1 < len(order):
            nxt = order[idx + 1]
            jobs.update({(*nxt, name): v for name, v in ffn_sources(*nxt).items()})
        own = {name: ready.pop((l, which, name)) for name in ffn_sources(l, which)}
        gain = (ffn1_norm if which == 1 else ffn2_norm)[l]
        args = (x, gain, own["w13"], own["w2"], list(jobs.values()))
        if which == 1 and l >= n_a:
            (x, q), cast = _ffn_q(*args, mix_norm[l], own["w_q"],
                                  jnp.tile(q_norm[l - n_a], N_HEADS).reshape(1, -1), pool)
        elif which == 2 and l >= n_a:
            (x,), cast = _attn_ffn(x, q, k_sh, vt_sh, tri, own["w_o"], gain, own["w13"], own["w2"],
                                   list(jobs.values()))
        elif which == 2:
            mix = {name: ready.pop(("mix", l, name)) for name in mixer_sources(l)}
            lru = (mix_norm[l], mix["w_in"], a_conv_w[l], a_conv_b[l], mix["w_r"], mix["w_i"],
                   a_b_r[l], a_b_i[l], a_lambda[l], mix["w_out"])
            kv = None
            if l == n_a - 1:
                kv = (kv_norm, own["w_kv"], jnp.tile(k_norm, N_HEADS).reshape(1, -1), pool)
            outs, cast = _lru_ffn(x, s, lru, (gain, own["w13"], own["w2"]), list(jobs.values()), kv)
            x = outs[0]
            if kv is not None:
                k_sh, vt_sh = outs[1].reshape(b, s, -1), outs[2]
        else:
            (x,), cast = _ffn(*args)
        ready.update(zip(jobs.keys(), cast))
    return x.reshape(b, s, d)
```

```python
import functools

import jax
import jax.numpy as jnp
import numpy as np
from jax import lax
from jax.experimental import pallas as pl
from jax.experimental.pallas import tpu as pltpu

EPS = 1e-6
N_HEADS = 16
HEAD_DIM = 64
CONV_W = 4
LRU_C = 8.0
LANES = 128
SUBLANES = 8
BF16_ROWS = 16
VMEM_LIMIT = 56 * 1024 * 1024

FFN_TM = 1024
FF_CHUNK = 256
LRU_TS = 256
ATT_T = 256
ATT_GROUP = 4
ATT_FFN_SHARES = (2, 3, 0, 4, 0, 2, 0)
POOL_W = 256
LOG_W_ZERO = -104.0
LOG_W_NONE = -1e30

F32 = jnp.float32
BF16 = jnp.bfloat16


def _rms_norm_rows(x, gain_row):
    ms = jnp.mean(x * x, axis=-1, keepdims=True)
    return x * lax.rsqrt(ms + EPS) * gain_row


def _sigmoid(x):
    return 1.0 / (1.0 + jnp.exp(-x))


def _const_spec(shape):
    nd = len(shape)
    return pl.BlockSpec(shape, lambda *_: (0,) * nd, pipeline_mode=pl.Buffered(1))


def _swiglu_up(xn, w13_ref, act_ref, chunks):
    d_ff = act_ref.shape[1]
    for c in chunks:
        lo = c * FF_CHUNK
        gate = jnp.dot(xn, w13_ref[:, lo:lo + FF_CHUNK], preferred_element_type=F32)
        up = jnp.dot(xn, w13_ref[:, d_ff + lo:d_ff + lo + FF_CHUNK], preferred_element_type=F32)
        act_ref[:, lo:lo + FF_CHUNK] = (gate * _sigmoid(gate) * up).astype(BF16)


def _half_swiglu_step(x, g_ref, w13_ref, w2_ref, act_ref):
    xn = _rms_norm_rows(x, g_ref[...]).astype(BF16)
    _swiglu_up(xn, w13_ref, act_ref, range(w2_ref.shape[0] // FF_CHUNK))
    return x + 0.5 * jnp.dot(act_ref[...], w2_ref[...], preferred_element_type=F32)


def _ffn_kernel(x_ref, g_ref, w13_ref, w2_ref, y_ref, act_ref):
    y_ref[...] = _half_swiglu_step(x_ref[...], g_ref, w13_ref, w2_ref, act_ref)


def _kv_tail(y, ng_ref, wkv_ref, kg_ref, pool_ref, k_ref, vt_ref):
    d_attn = k_ref.shape[1]
    h = _rms_norm_rows(y, ng_ref[...]).astype(BF16)
    kv = jnp.dot(h, wkv_ref[...], preferred_element_type=F32)
    k_ref[...] = _head_rms_norm(kv[:, :d_attn], kg_ref[...], pool_ref).astype(BF16)
    for pair in range(vt_ref.shape[0]):
        for blk in range(vt_ref.shape[1]):
            v_blk = kv[blk * ATT_T:(blk + 1) * ATT_T,
                       d_attn + pair * LANES:d_attn + (pair + 1) * LANES]
            vt_ref[pair, blk] = v_blk.T.astype(BF16)


def _ffn_q_kernel(x_ref, g_ref, w13_ref, w2_ref, ng_ref, wq_ref, qg_ref, pool_ref,
                  y_ref, q_ref, act_ref):
    y = _half_swiglu_step(x_ref[...], g_ref, w13_ref, w2_ref, act_ref)
    y_ref[...] = y
    h = _rms_norm_rows(y, ng_ref[...]).astype(BF16)
    q = jnp.dot(h, wq_ref[...], preferred_element_type=F32)
    q_ref[...] = (_head_rms_norm(q, qg_ref[...], pool_ref) * (HEAD_DIM ** -0.5)).astype(BF16)


def _cast_job(src, lead, n_steps):
    rows, cols = src.shape[-2:]
    n_chunks = max(k for k in range(1, n_steps + 1)
                   if rows % k == 0 and (rows // k) % BF16_ROWS == 0)
    chunk = rows // n_chunks
    if lead is None:
        in_spec = pl.BlockSpec((chunk, cols), lambda i: (jnp.minimum(i, n_chunks - 1), 0))
    else:
        in_spec = pl.BlockSpec((None, chunk, cols),
                               lambda i: (lead, jnp.minimum(i, n_chunks - 1), 0))
    out_spec = pl.BlockSpec((chunk, cols), lambda i: (jnp.minimum(i, n_chunks - 1), 0))
    return in_spec, out_spec, jax.ShapeDtypeStruct((rows, cols), BF16)


def _ffn_body(*refs, core, n_in, n_out, n_cast):
    ins, rest = refs[:n_in], refs[n_in:]
    cast_in, rest = rest[:n_cast], rest[n_cast:]
    outs, rest = rest[:n_out], rest[n_out:]
    cast_out, scratch = rest[:n_cast], rest[n_cast:]
    for src_ref, dst_ref in zip(cast_in, cast_out):
        dst_ref[...] = src_ref[...].astype(BF16)
    core(*ins, *outs, *scratch)


def _ffn_call(body, name, x2d, gain, w13, w2, casts, extra_in=(), extra_in_specs=(),
              extra_out_shapes=(), extra_out_specs=()):
    m, d = x2d.shape
    d_ff = w2.shape[0]
    n_steps = m // FFN_TM
    row_spec = pl.BlockSpec((FFN_TM, d), lambda i: (i, 0))
    jobs = [_cast_job(src, lead, n_steps) for src, lead in casts]
    in_specs = [row_spec, _const_spec((1, d)), _const_spec((d, 2 * d_ff)),
                _const_spec((d_ff, d)), *extra_in_specs]
    n_out = 1 + len(extra_out_shapes)
    out = pl.pallas_call(
        functools.partial(_ffn_body, core=body, n_in=len(in_specs), n_out=n_out,
                          n_cast=len(jobs)),
        grid=(n_steps,),
        in_specs=in_specs + [job[0] for job in jobs],
        out_specs=[row_spec, *extra_out_specs] + [job[1] for job in jobs],
        out_shape=[jax.ShapeDtypeStruct((m, d), F32), *extra_out_shapes] + [job[2] for job in jobs],
        scratch_shapes=[pltpu.VMEM((FFN_TM, d_ff), BF16)],
        compiler_params=pltpu.CompilerParams(
            dimension_semantics=("arbitrary",), vmem_limit_bytes=VMEM_LIMIT,
            allow_input_fusion=[i in (2, 3) for i in range(len(in_specs) + len(jobs))]),
        name=name,
    )(x2d, gain.reshape(1, d), w13, w2, *extra_in, *(src for src, _ in casts))
    return out[:n_out], out[n_out:]


def _ffn(x2d, gain, w13, w2, casts):
    return _ffn_call(_ffn_kernel, "ffn", x2d, gain, w13, w2, casts)


def _ffn_q(x2d, gain, w13, w2, casts, q_in_gain, w_q, q_gain_row, pool):
    m, d = x2d.shape
    d_attn = w_q.shape[1]
    return _ffn_call(
        _ffn_q_kernel, "ffn_q", x2d, gain, w13, w2, casts,
        extra_in=(q_in_gain.reshape(1, d), w_q, q_gain_row, pool),
        extra_in_specs=(_const_spec((1, d)), _const_spec((d, d_attn)),
                        _const_spec((1, d_attn)), _const_spec((POOL_W, POOL_W))),
        extra_out_shapes=(jax.ShapeDtypeStruct((m, d_attn), BF16),),
        extra_out_specs=(pl.BlockSpec((FFN_TM, d_attn), lambda i: (i, 0)),))


def _gelu_tanh(x):
    c = np.float32(np.sqrt(2.0 / np.pi))
    return x * (0.5 * (1.0 + jnp.tanh(c * (x + 0.044715 * (x * x * x)))))


def _lru_in(x, g_ref, win_ref):
    h = _rms_norm_rows(x, g_ref[...]).astype(BF16)
    return jnp.dot(h, win_ref[...], preferred_element_type=F32)


def _lru_gates(proj, cw_ref, cb_ref, wr_ref, wi_ref, br_ref, bi_ref, lam_ref,
               rec_ref, a_ref, u_ref, gate_ref):
    ts, w = gate_ref.shape
    bw = wr_ref.shape[1]
    n_blocks = w // bw
    gate_ref[...] = _gelu_tanh(proj[:, :w])
    rec = proj[:, w:]
    rec_ref[SUBLANES:SUBLANES + ts, :] = rec
    xc = cb_ref[...] + cw_ref[CONV_W - 1:CONV_W, :] * rec
    for k in range(CONV_W - 1):
        back = CONV_W - 1 - k
        xc = xc + cw_ref[k:k + 1, :] * rec_ref[SUBLANES - back:SUBLANES - back + ts, :]
    rec_ref[0:SUBLANES, :] = rec_ref[ts:ts + SUBLANES, :]

    xcb = xc.astype(BF16)
    sp_lam = jnp.maximum(-lam_ref[...], 0.0) + jnp.log(1.0 + jnp.exp(-jnp.abs(lam_ref[...])))
    for n in range(n_blocks):
        sl = slice(n * bw, (n + 1) * bw)
        r = _sigmoid(jnp.dot(xcb[:, sl], wr_ref[sl, :], preferred_element_type=F32) + br_ref[:, sl])
        i = _sigmoid(jnp.dot(xcb[:, sl], wi_ref[sl, :], preferred_element_type=F32) + bi_ref[:, sl])
        log_a = (-LRU_C) * r * sp_lam[:, sl]
        a = jnp.exp(log_a)
        a_ref[:, sl] = a
        u_ref[:, sl] = jnp.sqrt(-jnp.tanh(log_a) * (1.0 + a * a)) * (i * xc[:, sl])


def _lru_scan(a_ref, u_ref, hcar_ref):
    ts, w = u_ref.shape
    row = lax.broadcasted_iota(jnp.int32, (SUBLANES, w), 0)
    hprev = hcar_ref[...]
    for gidx in range(ts // SUBLANES):
        rows = slice(gidx * SUBLANES, (gidx + 1) * SUBLANES)
        a = a_ref[rows, :]
        u = u_ref[rows, :]
        for sh in (1, 2, 4):
            keep = row >= sh
            u_new = jnp.where(keep, u + a * pltpu.roll(u, sh, 0), u)
            a = jnp.where(keep, a * pltpu.roll(a, sh, 0), a)
            u = u_new
        hs = u + a * hprev
        u_ref[rows, :] = hs
        hprev = hs[SUBLANES - 1:SUBLANES, :]
    hcar_ref[...] = hprev


def _lru_out(x, u_ref, gate_ref, wout_ref):
    y = (u_ref[...] * gate_ref[...]).astype(BF16)
    return x + jnp.dot(y, wout_ref[...], preferred_element_type=F32)


N_LRU_IN = 10
N_FFN_IN = 3


def _lru_ffn_kernel(*refs, tiles_per_seq, with_kv):
    x_ref, refs = refs[0], refs[1:]
    lru_in, refs = refs[:N_LRU_IN], refs[N_LRU_IN:]
    ffn_in, refs = refs[:N_FFN_IN], refs[N_FFN_IN:]
    if with_kv:
        kv_in, refs = refs[:4], refs[4:]
        (y_ref, k_ref, vt_ref), refs = refs[:3], refs[3:]
    else:
        y_ref, refs = refs[0], refs[1:]
    act_ref, mid_ref, rec_ref, a_ref, u_ref, gate_ref, hcar_ref = refs
    step = pl.program_id(0)

    @pl.when(step == 0)
    def _():
        mid_ref[...] = jnp.zeros_like(mid_ref)

    @pl.when(step % tiles_per_seq == 0)
    def _():
        rec_ref[0:SUBLANES, :] = jnp.zeros((SUBLANES, rec_ref.shape[1]), F32)
        hcar_ref[...] = jnp.zeros_like(hcar_ref)

    mg_ref, win_ref, cw_ref, cb_ref, wr_ref, wi_ref, br_ref, bi_ref, lam_ref, wout_ref = lru_in
    g_ref, w13_ref, w2_ref = ffn_in
    n_chunks = w2_ref.shape[0] // FF_CHUNK
    x = x_ref[...]
    x_prev = mid_ref[...]
    cut = [0, (n_chunks + 1) // 4, n_chunks - 1, n_chunks]
    xn = _rms_norm_rows(x_prev, g_ref[...]).astype(BF16)
    _swiglu_up(xn, w13_ref, act_ref, range(cut[0], cut[1]))
    proj = _lru_in(x, mg_ref, win_ref)
    _swiglu_up(xn, w13_ref, act_ref, range(cut[1], cut[2]))
    _lru_gates(proj, cw_ref, cb_ref, wr_ref, wi_ref, br_ref, bi_ref, lam_ref,
               rec_ref, a_ref, u_ref, gate_ref)
    _swiglu_up(xn, w13_ref, act_ref, range(cut[2], cut[3]))
    y = x_prev + 0.5 * jnp.dot(act_ref[...], w2_ref[...], preferred_element_type=F32)
    y_ref[...] = y
    _lru_scan(a_ref, u_ref, hcar_ref)
    mid_ref[...] = _lru_out(x, u_ref, gate_ref, wout_ref)
    if with_kv:
        _kv_tail(y, *kv_in, k_ref, vt_ref)


def _lru_ffn(x2d, seq_len, lru, ffn, casts, kv=None):
    m, d = x2d.shape
    gain, w_in, conv_w, conv_b, w_r, w_i, b_r, b_i, lam, w_out = lru
    f_gain, w13, w2 = ffn
    w = w_out.shape[0]
    d_ff = w2.shape[0]
    n_tiles = m // LRU_TS
    tiles_per_seq = seq_len // LRU_TS
    last = n_tiles - 1
    prev = lambda i: jnp.maximum(i - 1, 0)
    jobs = [_cast_job(src, lead, n_tiles + 1) for src, lead in casts]
    row_vec = lambda v: v.reshape(1, -1)
    operands = [x2d, row_vec(gain), w_in, conv_w, row_vec(conv_b), w_r, w_i, row_vec(b_r),
                row_vec(b_i), row_vec(lam), w_out, row_vec(f_gain), w13, w2]
    in_specs = [pl.BlockSpec((LRU_TS, d), lambda i: (jnp.minimum(i, last), 0))]
    in_specs += [_const_spec(op.shape) for op in operands[1:]]
    out_specs = [pl.BlockSpec((LRU_TS, d), lambda i: (prev(i), 0))]
    out_shape = [jax.ShapeDtypeStruct((m, d), F32)]
    if kv is not None:
        kv_gain, w_kv, k_gain_row, pool = kv
        d_attn = w_kv.shape[1] // 2
        n_pairs = d_attn // LANES
        kv_ops = [row_vec(kv_gain), w_kv, k_gain_row, pool]
        operands += kv_ops
        in_specs += [_const_spec(op.shape) for op in kv_ops]
        out_specs += [
            pl.BlockSpec((LRU_TS, d_attn), lambda i: (prev(i), 0)),
            pl.BlockSpec((None, n_pairs, LRU_TS // ATT_T, LANES, ATT_T),
                         lambda i: (prev(i) // tiles_per_seq, 0, prev(i) % tiles_per_seq, 0, 0))]
        out_shape += [
            jax.ShapeDtypeStruct((m, d_attn), BF16),
            jax.ShapeDtypeStruct((m // seq_len, n_pairs, seq_len // ATT_T, LANES, ATT_T), BF16)]
    n_out = len(out_shape)
    out = pl.pallas_call(
        functools.partial(
            _ffn_body, n_in=len(in_specs), n_out=n_out, n_cast=len(jobs),
            core=functools.partial(_lru_ffn_kernel, tiles_per_seq=tiles_per_seq,
                                   with_kv=kv is not None)),
        grid=(n_tiles + 1,),
        in_specs=in_specs + [job[0] for job in jobs],
        out_specs=out_specs + [job[1] for job in jobs],
        out_shape=out_shape + [job[2] for job in jobs],
        scratch_shapes=[
            pltpu.VMEM((LRU_TS, d_ff), BF16),
            pltpu.VMEM((LRU_TS, d), F32),
            pltpu.VMEM((LRU_TS + SUBLANES, w), F32),
            pltpu.VMEM((LRU_TS, w), F32),
            pltpu.VMEM((LRU_TS, w), F32),
            pltpu.VMEM((LRU_TS, w), F32),
            pltpu.VMEM((1, w), F32),
        ],
        compiler_params=pltpu.CompilerParams(
            dimension_semantics=("arbitrary",), vmem_limit_bytes=VMEM_LIMIT),
        name="lru_ffn",
    )(*operands, *(src for src, _ in casts))
    return out[:n_out], out[n_out:]


def _head_rms_norm(t, gain_row, pool_ref):
    outs = []
    for g in range(t.shape[1] // POOL_W):
        tg = t[:, g * POOL_W:(g + 1) * POOL_W]
        ms = jnp.dot((tg * tg).astype(BF16), pool_ref[...], preferred_element_type=F32)
        outs.append(tg * lax.rsqrt(ms + EPS))
    return jnp.concatenate(outs, axis=1) * gain_row


def _head_pool_matrix():
    idx = np.arange(POOL_W) // HEAD_DIM
    return jnp.asarray((idx[:, None] == idx[None, :]).astype(np.float32) / HEAD_DIM, BF16)


def _attn_causal():
    k_pos = lax.broadcasted_iota(jnp.int32, (ATT_T, ATT_T), 0)
    q_pos = lax.broadcasted_iota(jnp.int32, (ATT_T, ATT_T), 1)
    return k_pos < q_pos


def _attn_halves(z, masked, causal):
    h = ATT_T // 2
    if masked:
        return [(z[:h], causal[:h], 0), (z[h:, h:], causal[h:, h:], h)]
    return [(z[:h], None, 0), (z[h:], None, 0)]


def _attn_widen(x, q0):
    return x if q0 == 0 else jnp.concatenate([jnp.zeros((x.shape[0], q0), x.dtype), x], axis=1)


def _attn_scores(specs, q_heads, k_ref):
    zs = []
    for head, kj, _, _ in specs:
        pair = head // 2
        kb = k_ref[pl.ds(pl.multiple_of(kj * ATT_T, ATT_T), ATT_T), pair * LANES:(pair + 1) * LANES]
        zs.append(lax.dot_general(kb, q_heads[head], (((1,), (1,)), ((), ())),
                                  preferred_element_type=F32))
    return zs


def _attn_sums(zs, specs, tri_ref, causal):
    exts = []
    for z, (_, _, masked, _) in zip(zs, specs):
        ext = []
        for zp, mask, q0 in _attn_halves(z, masked, causal):
            sp = jnp.maximum(zp, 0.0) + jnp.log(1.0 + jnp.exp(-jnp.abs(zp)))
            sp16 = (sp if mask is None else jnp.where(mask, sp, 0.0)).astype(BF16)
            ext.append(jnp.dot(tri_ref[...], _attn_widen(sp16, q0), preferred_element_type=F32))
        exts.append(ext)
    return exts


def _attn_weights(zs, exts, specs, causal, chain_valid=None):
    h = ATT_T // 2
    ws, carries = [], []
    for z, (ext_early, ext_late), (_, _, masked, carry) in zip(zs, exts, specs):
        if isinstance(carry, int):
            carry = carries[carry]
            if chain_valid is not None:
                carry = jnp.where(chain_valid, carry, LOG_W_NONE)
        total_late = jnp.broadcast_to(ext_late[0:1], (SUBLANES, ATT_T))
        carry_early = total_late if carry is None else carry + total_late
        w_halves = []
        for (zp, mask, q0), ext, c in zip(_attn_halves(z, masked, causal), (ext_early, ext_late),
                                          (carry_early, carry)):
            log_w = zp + ext[:, q0:]
            if c is not None:
                log_w = log_w + c[0:1, q0:]
            w = jnp.exp(log_w)
            if mask is not None:
                w = jnp.where(mask, w, 0.0)
            w_halves.append(_attn_widen(w.astype(BF16), q0))
        ws.append(jnp.concatenate(w_halves, axis=0))
        carries.append(carry_early + jnp.broadcast_to(ext_early[0:1], (SUBLANES, ATT_T)))
    return ws, carries


def _attn_values(ws, specs, vt_ref):
    return [jnp.dot(vt_ref[head // 2, kj], w, preferred_element_type=F32)
            for w, (head, kj, _, _) in zip(ws, specs)]


def _attn_ffn_kernel(x_ref, q_ref, k_ref, vt_ref, tri_ref, wo_ref, g_ref, w13_ref, w2_ref, y_ref,
                     act_ref, acc_ref, carry_ref, *, tiles_per_seq, n_tiles):
    t = ATT_T
    n_heads = acc_ref.shape[0]
    n_chunks = w2_ref.shape[0] // FF_CHUNK
    step = pl.program_id(0)
    qi = jnp.minimum(step, n_tiles - 1) % tiles_per_seq

    @pl.when(step == 0)
    def _():
        acc_ref[...] = jnp.zeros_like(acc_ref)

    feat = lax.broadcasted_iota(jnp.int32, (LANES, t), 0)
    o_prev = jnp.concatenate(
        [jnp.where(feat < HEAD_DIM, acc_ref[2 * pair], acc_ref[2 * pair + 1]).T.astype(BF16)
         for pair in range(n_heads // 2)], axis=1)

    lane = lax.broadcasted_iota(jnp.int32, (t, LANES), 1)
    causal = _attn_causal()
    q_heads = []
    for pair in range(n_heads // 2):
        q = q_ref[:, pair * LANES:(pair + 1) * LANES]
        q_heads += [jnp.where(lane < HEAD_DIM, q, jnp.zeros_like(q)),
                    jnp.where(lane >= HEAD_DIM, q, jnp.zeros_like(q))]

    has_prev = qi > 0
    kj_prev = jnp.maximum(qi - 1, 0)
    groups = [range(g, g + ATT_GROUP) for g in range(0, n_heads, ATT_GROUP)]
    n_up = 2 * len(groups) - 1
    shares = ATT_FFN_SHARES if len(ATT_FFN_SHARES) == n_up else (1,) * n_up
    cuts = [-(-n_chunks * sum(shares[:s]) // sum(shares)) for s in range(n_up + 1)]
    up_slices = [range(cuts[s], cuts[s + 1]) for s in range(n_up)]
    ffn_stage = iter(up_slices)
    x_in = xn = None
    results = {}
    for gidx, heads in enumerate(groups):
        specs = ([(head, qi, True, None) for head in heads]
                 + [(head, kj_prev, False, idx) for idx, head in enumerate(heads)])
        zs = _attn_scores(specs, q_heads, k_ref)
        if gidx == 0:
            x_in = x_ref[...] + jnp.dot(o_prev, wo_ref[...], preferred_element_type=F32)
            xn = _rms_norm_rows(x_in, g_ref[...]).astype(BF16)
        else:
            _swiglu_up(xn, w13_ref, act_ref, next(ffn_stage))
        exts = _attn_sums(zs, specs, tri_ref, causal)
        _swiglu_up(xn, w13_ref, act_ref, next(ffn_stage))
        if gidx == len(groups) - 1:
            y_ref[...] = x_in + 0.5 * jnp.dot(act_ref[...], w2_ref[...], preferred_element_type=F32)
        ws, carries = _attn_weights(zs, exts, specs, causal, chain_valid=has_prev)
        pvs = _attn_values(ws, specs, vt_ref)
        for idx, head in enumerate(heads):
            results[head] = (pvs[idx] + pvs[len(heads) + idx], carries[len(heads) + idx])
    for head, (pv, carry) in results.items():
        acc_ref[head] = pv
        carry_ref[head] = carry

    def max_carry(carries):
        top = carries[0]
        for carry in carries[1:]:
            top = jnp.maximum(top, carry)
        return jnp.max(top)

    def cond(state):
        kj, top = state
        return jnp.logical_and(kj >= 0, top > LOG_W_ZERO)

    def body(state):
        kj, _ = state
        specs = [(head, kj, False, carry_ref[head]) for head in range(n_heads)]
        zs = _attn_scores(specs, q_heads, k_ref)
        ws, carries = _attn_weights(zs, _attn_sums(zs, specs, tri_ref, causal), specs, causal)
        for head, pv in enumerate(_attn_values(ws, specs, vt_ref)):
            acc_ref[head] += pv
            carry_ref[head] = carries[head]
        return kj - 1, max_carry(carries)

    top = max_carry([carry for _, carry in results.values()])
    lax.while_loop(cond, body, (qi - 2, top))


def _suffix_sum_matrix():
    j = np.arange(ATT_T // 2)
    return jnp.asarray(-(j[None, :] >= j[:, None]).astype(np.float32), BF16)


def _attn_ffn(x2d, q2d, k, vt, tri, w_o, gain, w13, w2, casts):
    m, d = x2d.shape
    b, s, d_attn = k.shape
    d_ff = w2.shape[0]
    n_heads = d_attn // HEAD_DIM
    n_tiles = m // ATT_T
    tiles_per_seq = s // ATT_T
    cur = lambda i: jnp.minimum(i, n_tiles - 1)
    prev = lambda i: jnp.maximum(i - 1, 0)
    jobs = [_cast_job(src, lead, n_tiles + 1) for src, lead in casts]
    operands = [x2d, q2d, k, vt, tri, w_o, gain.reshape(1, d), w13, w2]
    in_specs = [
        pl.BlockSpec((ATT_T, d), lambda i: (prev(i), 0)),
        pl.BlockSpec((ATT_T, d_attn), lambda i: (cur(i), 0)),
        pl.BlockSpec((None, s, d_attn), lambda i: (cur(i) // tiles_per_seq, 0, 0),
                     pipeline_mode=pl.Buffered(1)),
        pl.BlockSpec((None,) + vt.shape[1:], lambda i: (cur(i) // tiles_per_seq, 0, 0, 0, 0),
                     pipeline_mode=pl.Buffered(1)),
    ] + [_const_spec(op.shape) for op in operands[4:]]
    out = pl.pallas_call(
        functools.partial(
            _ffn_body, n_in=len(in_specs), n_out=1, n_cast=len(jobs),
            core=functools.partial(_attn_ffn_kernel, tiles_per_seq=tiles_per_seq, n_tiles=n_tiles)),
        grid=(n_tiles + 1,),
        in_specs=in_specs + [job[0] for job in jobs],
        out_specs=[pl.BlockSpec((ATT_T, d), lambda i: (prev(i), 0))] + [job[1] for job in jobs],
        out_shape=[jax.ShapeDtypeStruct((m, d), F32)] + [job[2] for job in jobs],
        scratch_shapes=[
            pltpu.VMEM((ATT_T, d_ff), BF16),
            pltpu.VMEM((n_heads, LANES, ATT_T), F32),
            pltpu.VMEM((n_heads, SUBLANES, ATT_T), F32),
        ],
        compiler_params=pltpu.CompilerParams(
            dimension_semantics=("arbitrary",), vmem_limit_bytes=VMEM_LIMIT),
        name="attn_ffn",
    )(*operands, *(src for src, _ in casts))
    return out[:1], out[1:]


def kernel(x, ffn1_norm, ffn1_w13, ffn1_w2, mix_norm, a_w_in, a_conv_w, a_conv_b, a_w_r, a_b_r,
           a_w_i, a_b_i, a_lambda, a_w_out, kv_norm, w_kv, k_norm, b_w_q, q_norm, b_w_o,
           ffn2_norm, ffn2_w13, ffn2_w2):
    b, s, d = x.shape
    depth = ffn1_norm.shape[0]
    n_a = a_w_in.shape[0]
    m = b * s
    pool = _head_pool_matrix()
    tri = _suffix_sum_matrix()

    assert 1 <= n_a < depth and s % FFN_TM == 0 and FFN_TM % ATT_T == 0

    def ffn_sources(l, which):
        w13, w2 = (ffn1_w13, ffn1_w2) if which == 1 else (ffn2_w13, ffn2_w2)
        srcs = {"w13": (w13, l), "w2": (w2, l)}
        if which == 2 and l == n_a - 1:
            srcs["w_kv"] = (w_kv, None)
        if l >= n_a:
            srcs["w_q" if which == 1 else "w_o"] = ((b_w_q if which == 1 else b_w_o), l - n_a)
        return srcs

    def mixer_sources(l):
        if l >= n_a:
            return {}
        lru_w = a_w_out.shape[1]
        return {"w_in": (a_w_in, l), "w_r": (a_w_r.reshape(n_a, lru_w, -1), l),
                "w_i": (a_w_i.reshape(n_a, lru_w, -1), l), "w_out": (a_w_out, l)}

    order = [(l, which) for l in range(depth) for which in (1, 2)]
    ready = {(0, 1, name): (src[lead] if lead is not None else src).astype(BF16)
             for name, (src, lead) in ffn_sources(0, 1).items()}

    x = x.reshape(m, d)
    k_sh = vt_sh = q = None
    for idx, (l, which) in enumerate(order):
        jobs = {}
        if which == 1:
            jobs.update({("mix", l, name): v for name, v in mixer_sources(l).items()})
        if idx + 1 < len(order):
            nxt = order[idx + 1]
            jobs.update({(*nxt, name): v for name, v in ffn_sources(*nxt).items()})
        own = {name: ready.pop((l, which, name)) for name in ffn_sources(l, which)}
        gain = (ffn1_norm if which == 1 else ffn2_norm)[l]
        args = (x, gain, own["w13"], own["w2"], list(jobs.values()))
        if which == 1 and l >= n_a:
            (x, q), cast = _ffn_q(*args, mix_norm[l], own["w_q"],
                                  jnp.tile(q_norm[l - n_a], N_HEADS).reshape(1, -1), pool)
        elif which == 2 and l >= n_a:
            (x,), cast = _attn_ffn(x, q, k_sh, vt_sh, tri, own["w_o"], gain, own["w13"], own["w2"],
                                   list(jobs.values()))
        elif which == 2:
            mix = {name: ready.pop(("mix", l, name)) for name in mixer_sources(l)}
            lru = (mix_norm[l], mix["w_in"], a_conv_w[l], a_conv_b[l], mix["w_r"], mix["w_i"],
                   a_b_r[l], a_b_i[l], a_lambda[l], mix["w_out"])
            kv = None
            if l == n_a - 1:
                kv = (kv_norm, own["w_kv"], jnp.tile(k_norm, N_HEADS).reshape(1, -1), pool)
            outs, cast = _lru_ffn(x, s, lru, (gain, own["w13"], own["w2"]), list(jobs.values()), kv)
            x = outs[0]
            if kv is not None:
                k_sh, vt_sh = outs[1].reshape(b, s, -1), outs[2]
        else:
            (x,), cast = _ffn(*args)
        ready.update(zip(jobs.keys(), cast))
    return x.reshape(b, s, d)
```
